```python
import math
import jax, jax.numpy as jnp
from jax import lax
import numpy as np

D_MODEL = 4096
BATCH = 4
SEQ = 2048
DEPTH = 2
DEC_BATCH = 8
DEC_SEQ = 4
PAST_LEN = 16384
PAGE_SIZE = 128

PLE_DIM = 256
N_A = (DEPTH + 1) // 2
N_C = DEPTH // 2
NSA_HEADS = 16
NSA_KV_HEADS = 4
NSA_GROUP = NSA_HEADS // NSA_KV_HEADS
NSA_HEAD_DIM = 128
NSA_Q = NSA_HEADS * NSA_HEAD_DIM
NSA_KV_W = 6 * NSA_KV_HEADS * NSA_HEAD_DIM
NSA_GATE_W = 3 * NSA_HEADS
CMP_BLOCK = 32
CMP_STRIDE = 16
CMP_HIDDEN = 256
SLC_BLOCK = 64
N_SELECT = 16
WINDOW = 512
SLC_QBLOCK = 32
SWA_QBLOCK = 128
FORCE_SCORE = 1e4
HG_HEADS = 16
HG_DK = 128
HG_DV = (D_MODEL // 2) // HG_HEADS
HG_QF_W = HG_HEADS * HG_DK
HG_V_W = HG_HEADS * HG_DV
HG_CHUNK = 32
AB_SIZES = (NSA_Q, NSA_KV_W, NSA_GATE_W, HG_QF_W, HG_QF_W, HG_V_W, HG_V_W)
AB_IN = NSA_Q + NSA_KV_W + NSA_GATE_W + 2 * HG_QF_W + 2 * HG_V_W
AB_MIX = NSA_Q + HG_V_W
SSM_D_INNER = 2 * D_MODEL
SSM_HEAD_DIM = 64
SSM_HEADS = SSM_D_INNER // SSM_HEAD_DIM
SSM_GROUPS = 8
SSM_STATE = 128
SSM_CONV = 4
SSM_CONV_DIM = SSM_D_INNER + 2 * SSM_GROUPS * SSM_STATE
SSM_IN = SSM_D_INNER + SSM_CONV_DIM + SSM_HEADS
SSM_CHUNK = 64
D_FF = ((8 * D_MODEL + 3 * 256 - 1) // (3 * 256)) * 256
EPS = 1e-6
NEG = -1e30

kernel_name = 'nsa_hgrn2_mamba2_hybrid_step'


def rms_norm(x, g):
    xf = x.astype(jnp.float32)
    y = xf * lax.rsqrt(jnp.mean(xf * xf, axis=-1, keepdims=True) + EPS)
    return (y * g.astype(jnp.float32)).astype(x.dtype)


def split_cols(a, sizes):
    out, o = [], 0
    for s in sizes:
        out.append(a[..., o:o + s])
        o += s
    return out


def to_chunks(a, c):
    b, l = a.shape[:2]
    pad = (-l) % c
    a = jnp.pad(a.astype(jnp.float32), [(0, 0), (0, pad)] + [(0, 0)] * (a.ndim - 2))
    return jnp.moveaxis(a.reshape((b, (l + pad) // c, c) + a.shape[2:]), 1, 0)


def from_chunks(a, l):
    a = jnp.moveaxis(a, 0, 1)
    return a.reshape((a.shape[0], a.shape[1] * a.shape[2]) + a.shape[3:])[:, :l]


def gqa_attend(q, k, v, mask):
    s = jnp.einsum('...qgrd,...kgd->...qgrk', q, k, preferred_element_type=jnp.float32) * (q.shape[-1] ** -0.5)
    m = mask[..., :, None, None, :]
    p = jnp.where(m, jax.nn.softmax(jnp.where(m, s, NEG), axis=-1), 0.0)
    o = jnp.einsum('...qgrk,...kgd->...qgrd', p.astype(v.dtype), v)
    return o, p


def compress_rows(rows, w1, w2, pe):
    b, t = rows.shape[:2]
    n_sub = t // CMP_STRIDE
    n_per = CMP_BLOCK // CMP_STRIDE
    n_cmp = n_sub - n_per + 1
    sub = rows[:, :n_sub * CMP_STRIDE].reshape(b, n_sub, CMP_STRIDE, NSA_KV_HEADS, NSA_HEAD_DIM)
    sub = jnp.moveaxis(sub, 3, 2).reshape(b, n_sub, NSA_KV_HEADS, CMP_STRIDE * NSA_HEAD_DIM)
    w1r = w1.reshape(n_per, CMP_STRIDE * NSA_HEAD_DIM, CMP_HIDDEN)
    hid = pe.reshape(-1) @ w1
    for m in range(n_per):
        hid = hid + jnp.einsum('bngc,ch->bngh', sub[:, m:m + n_cmp], w1r[m])
    return jax.nn.silu(hid) @ w2


def slc_attend(q, pos, kb, vb, idx, valid):
    b, lq, g, r, d = q.shape
    k = idx.shape[-1]
    ii = jnp.moveaxis(idx, 2, 1).reshape(b, g, lq * k)
    bi = jnp.arange(b)[:, None, None]
    gi = jnp.arange(g)[None, :, None]
    kg = kb[bi, gi, ii].reshape(b, g, lq, k, SLC_BLOCK, d)
    vg = vb[bi, gi, ii].reshape(b, g, lq, k, SLC_BLOCK, d)
    s = jnp.einsum('bqgrd,bgqksd->bqgrks', q, kg, preferred_element_type=jnp.float32) * (d ** -0.5)
    kpos = idx[..., None] * SLC_BLOCK + jnp.arange(SLC_BLOCK)
    mask = ((kpos <= pos[None, :, None, None, None]) & valid[..., None])[:, :, :, None]
    s = jnp.where(mask, s, NEG).reshape(b, lq, g, r, k * SLC_BLOCK)
    p = jax.nn.softmax(s, axis=-1).reshape(b, lq, g, r, k, SLC_BLOCK)
    return jnp.einsum('bqgrks,bgqksd->bqgrd', p.astype(vg.dtype), vg)


def nsa_cmp_slc(q, pos, rows, w1, w2, pe):
    b, l, g, r, d = q.shape
    t = rows.shape[1]
    kc = compress_rows(rows[:, :, 0], w1[0], w2[0], pe[0])
    vc = compress_rows(rows[:, :, 1], w1[1], w2[1], pe[1])
    n_cmp = kc.shape[1]
    cmp_last = jnp.arange(n_cmp) * CMP_STRIDE + (CMP_BLOCK - 1)
    o_cmp, p_cmp = gqa_attend(q, kc, vc, cmp_last[None, :] <= pos[:, None])
    n_slc = -(-t // SLC_BLOCK)
    c0 = np.arange(n_cmp)[:, None] * CMP_STRIDE
    s0 = np.arange(n_slc)[None, :] * SLC_BLOCK
    cover = np.clip(np.minimum(c0 + CMP_BLOCK, s0 + SLC_BLOCK) - np.maximum(c0, s0), 0, None) / CMP_BLOCK
    score = jnp.einsum('blgrn,nj->blgj', p_cmp, jnp.asarray(cover, jnp.float32))
    blk = jnp.arange(n_slc)[None, :]
    cur = (pos // SLC_BLOCK)[:, None]
    causal = blk * SLC_BLOCK <= pos[:, None]
    forced = (blk == 0) | (blk == cur) | (blk == cur - 1)
    score = jnp.where(forced[None, :, None, :], FORCE_SCORE, jnp.where(causal[None, :, None, :], score, -1.0))
    vals, idx = lax.top_k(score, min(N_SELECT, n_slc))
    valid = vals >= 0.0
    kv = jnp.pad(rows[:, :, 2:4], ((0, 0), (0, n_slc * SLC_BLOCK - t), (0, 0), (0, 0), (0, 0)))
    kv = jnp.transpose(kv.reshape(b, n_slc, SLC_BLOCK, 2, g, d), (3, 0, 4, 1, 2, 5))
    kb, vb = kv[0], kv[1]
    qb = math.gcd(l, SLC_QBLOCK)
    nb = l // qb

    def blocks(a):
        return jnp.moveaxis(a.reshape((b, nb, qb) + a.shape[2:]), 1, 0)

    o_slc = lax.map(lambda a: slc_attend(a[0], a[1], kb, vb, a[2], a[3]),
                    (blocks(q), pos.reshape(nb, qb), blocks(idx), blocks(valid)))
    o_slc = jnp.moveaxis(o_slc, 0, 1).reshape(b, l, g, r, d)
    return o_cmp, o_slc


def swa_banded(q, kv):
    b, t, g, r, d = q.shape
    qb = math.gcd(t, SWA_QBLOCK)
    nb = t // qb
    span = qb + WINDOW
    kvp = jnp.pad(kv, ((0, 0), (WINDOW, 0), (0, 0), (0, 0), (0, 0)))
    idx = np.arange(nb)[:, None] * qb + np.arange(span)[None, :]
    kvb = kvp[:, idx]
    qpos = np.arange(t).reshape(nb, qb)
    kpos = idx - WINDOW
    mask = (kpos[:, None, :] >= 0) & (kpos[:, None, :] <= qpos[:, :, None]) & (qpos[:, :, None] - kpos[:, None, :] < WINDOW)
    o, _ = gqa_attend(q.reshape(b, nb, qb, g, r, d), kvb[:, :, :, 0], kvb[:, :, :, 1], jnp.asarray(mask))
    return o.reshape(b, t, g, r, d)


def swa_buffered(q, pos, kv_new, buf):
    wb = buf.shape[1]
    keys = jnp.concatenate([buf.astype(kv_new.dtype), kv_new], axis=1)
    kpos = PAST_LEN - wb + jnp.arange(keys.shape[1])
    mask = (kpos[None, :] <= pos[:, None]) & (pos[:, None] - kpos[None, :] < WINDOW)
    o, _ = gqa_attend(q, keys[:, :, 0], keys[:, :, 1], mask)
    return o, keys[:, keys.shape[1] - wb:]


def gla_chunked(q, k, v, logf, s0):
    b, l = q.shape[:2]
    c = min(HG_CHUNK, l)
    mid = (c - 1) // 2
    tril = jnp.tril(jnp.ones((c, c), bool))

    def step(s, inp):
        qc, kc, vc, gc = inp
        bc = jnp.cumsum(gc, axis=1)
        bm = bc[:, mid:mid + 1]
        a = jnp.einsum('bthd,bshd->bhts', qc * jnp.exp(bc - bm), kc * jnp.exp(bm - bc))
        a = jnp.where(tril, a, 0.0)
        o = jnp.einsum('bhts,bshv->bthv', a, vc) + jnp.einsum('bthd,bhdv->bthv', qc * jnp.exp(bc), s)
        bl = bc[:, -1]
        s = jnp.exp(bl)[..., None] * s + jnp.einsum('bshd,bshv->bhdv', kc * jnp.exp(bl[:, None] - bc), vc)
        return s, o

    s, o = lax.scan(step, s0.astype(jnp.float32), (to_chunks(q, c), to_chunks(k, c), to_chunks(v, c), to_chunks(logf, c)))
    return from_chunks(o, l), s


def hgrn2(hq, hf, hi, hg, lb, s0, g_norm):
    b, l, _ = hq.shape
    f32 = jnp.float32
    q = jax.nn.silu(hq.astype(f32)).reshape(b, l, HG_HEADS, HG_DK)
    f = (lb + (1.0 - lb) * jax.nn.sigmoid(hf.astype(f32))).reshape(b, l, HG_HEADS, HG_DK)
    v = hi.astype(f32).reshape(b, l, HG_HEADS, HG_DV)
    o, s = gla_chunked(q, 1.0 - f, v, jnp.log(f), s0)
    o = o * lax.rsqrt(jnp.mean(o * o, axis=-1, keepdims=True) + EPS)
    o = o.reshape(b, l, HG_V_W) * g_norm.astype(f32) * jax.nn.silu(hg.astype(f32))
    return o, s


def nsa_hgrn_mixer(h, pos, past_rows, swa_buf, hg_state, lb, w_in, w1, w2, pe, hg_g, w_out):
    b, l, _ = h.shape
    q, kv, g, hq, hf, hi, hgate = split_cols(h @ w_in, AB_SIZES)
    q = q.reshape(b, l, NSA_KV_HEADS, NSA_GROUP, NSA_HEAD_DIM)
    kv = kv.reshape(b, l, 6, NSA_KV_HEADS, NSA_HEAD_DIM)
    rows = kv[:, :, :4] if past_rows is None else jnp.concatenate([past_rows.astype(kv.dtype), kv[:, :, :4]], axis=1)
    o_cmp, o_slc = nsa_cmp_slc(q, pos, rows, w1, w2, pe)
    if swa_buf is None:
        o_swa = swa_banded(q, kv[:, :, 4:])
        new_buf = kv[:, l - min(WINDOW, l):, 4:]
    else:
        o_swa, new_buf = swa_buffered(q, pos, kv[:, :, 4:], swa_buf)
    gates = jax.nn.sigmoid(g.astype(jnp.float32)).reshape(b, l, NSA_KV_HEADS, NSA_GROUP, 3)
    o_nsa = gates[..., 0:1] * o_cmp + gates[..., 1:2] * o_slc + gates[..., 2:3] * o_swa
    o_hg, s_new = hgrn2(hq, hf, hi, hgate, lb, hg_state, hg_g)
    mix = jnp.concatenate([o_nsa.reshape(b, l, NSA_Q).astype(h.dtype), o_hg.astype(h.dtype)], axis=-1)
    return mix @ w_out, kv[:, :, :4], new_buf, s_new


def ssd_chunked(x, dt, a, bm, cm, s0):
    b, l, nh, p = x.shape
    g, n = bm.shape[2], bm.shape[3]
    r = nh // g
    c = min(SSM_CHUNK, l)
    tril = jnp.tril(jnp.ones((c, c), bool))

    def step(s, inp):
        xc, dtc, bc, cc = inp
        cum = jnp.cumsum(dtc * a, axis=1)
        seg = cum[:, :, None, :] - cum[:, None, :, :]
        lm = jnp.exp(jnp.where(tril[None, :, :, None], seg, -jnp.inf)).reshape(b, c, c, g, r)
        xdt = (xc * dtc[..., None]).reshape(b, c, g, r, p)
        cb = jnp.einsum('btgn,bsgn->btsg', cc, bc)
        sg = s.reshape(b, g, r, p, n)
        y = jnp.einsum('btsg,btsgr,bsgrp->btgrp', cb, lm, xdt)
        y = y + jnp.einsum('btgn,bgrpn->btgrp', cc, sg) * jnp.exp(cum).reshape(b, c, g, r)[..., None]
        dec = jnp.exp(cum[:, -1:] - cum).reshape(b, c, g, r)
        sg = jnp.exp(cum[:, -1]).reshape(b, g, r)[..., None, None] * sg + jnp.einsum('bsgn,bsgrp->bgrpn', bc, xdt * dec[..., None])
        return sg.reshape(b, nh, p, n), y.reshape(b, c, nh, p)

    s, y = lax.scan(step, s0.astype(jnp.float32), (to_chunks(x, c), to_chunks(dt, c), to_chunks(bm, c), to_chunks(cm, c)))
    return from_chunks(y, l), s


def mamba_mixer(h, conv_state, ssm_state, w_in, conv_w, conv_b, dt_bias, a_log, d_skip, norm_g, w_out):
    b, l, _ = h.shape
    f32 = jnp.float32
    z, xbc, dt = split_cols(h @ w_in, (SSM_D_INNER, SSM_CONV_DIM, SSM_HEADS))
    xpad = jnp.concatenate([conv_state.astype(xbc.dtype), xbc], axis=1)
    acc = conv_b.astype(f32)
    for j in range(SSM_CONV):
        acc = acc + xpad[:, j:j + l].astype(f32) * conv_w[j].astype(f32)
    xbc = jax.nn.silu(acc)
    new_conv = xpad[:, xpad.shape[1] - (SSM_CONV - 1):]
    xs, bm, cm = split_cols(xbc, (SSM_D_INNER, SSM_GROUPS * SSM_STATE, SSM_GROUPS * SSM_STATE))
    xs = xs.reshape(b, l, SSM_HEADS, SSM_HEAD_DIM)
    bm = bm.reshape(b, l, SSM_GROUPS, SSM_STATE)
    cm = cm.reshape(b, l, SSM_GROUPS, SSM_STATE)
    dt = jax.nn.softplus(dt.astype(f32) + dt_bias.astype(f32))
    a = -jnp.exp(a_log.astype(f32))
    y, s = ssd_chunked(xs, dt, a, bm, cm, ssm_state)
    y = y + d_skip.astype(f32)[:, None] * xs
    y = (y.reshape(b, l, SSM_D_INNER) * jax.nn.silu(z.astype(f32))).reshape(b, l, SSM_GROUPS, SSM_D_INNER // SSM_GROUPS)
    y = (y * lax.rsqrt(jnp.mean(y * y, axis=-1, keepdims=True) + EPS)).reshape(b, l, SSM_D_INNER) * norm_g.astype(f32)
    return y.astype(h.dtype) @ w_out, new_conv, s


def swiglu(h, w_in, w_out):
    gt, up = split_cols(h @ w_in, (D_FF, D_FF))
    return (jax.nn.silu(gt) * up) @ w_out


def ple_add(x, p, w_up, w_gate, g):
    return x + ((p.astype(x.dtype) @ w_up) * jax.nn.sigmoid(rms_norm(x, g) @ w_gate)).astype(x.dtype)


def setup_inputs(seed: int = 0) -> dict:
    key = jax.random.key(seed)
    ks = jax.random.split(key, 40)
    f32 = jnp.float32

    def nrm(k, shape, scale):
        return jax.random.normal(k, shape, f32) * scale

    n_pages = PAST_LEN // PAGE_SIZE
    n_used = DEC_BATCH * n_pages
    n_pool = n_used + (n_used + 3) // 4
    page_table = jax.random.permutation(ks[0], n_pool)[:n_used].reshape(DEC_BATCH, n_pages).astype(jnp.int32)
    dt0 = jnp.exp(jax.random.uniform(ks[1], (N_C, SSM_HEADS), f32, math.log(1e-3), math.log(1e-1)))
    return {
        'x_prompt': nrm(ks[2], (BATCH, SEQ, D_MODEL), 1.0),
        'x_sample': nrm(ks[3], (DEC_BATCH, DEC_SEQ, D_MODEL), 1.0),
        'cache_nsa_kv': nrm(ks[4], (N_A, n_pool, PAGE_SIZE, 4, NSA_KV_HEADS, NSA_HEAD_DIM), 1.0),
        'cache_swa_kv': nrm(ks[5], (N_A, DEC_BATCH, min(WINDOW, PAST_LEN), 2, NSA_KV_HEADS, NSA_HEAD_DIM), 1.0),
        'state_hgrn': nrm(ks[6], (N_A, DEC_BATCH, HG_HEADS, HG_DK, HG_DV), 0.5),
        'state_ssm': nrm(ks[7], (N_C, DEC_BATCH, SSM_HEADS, SSM_HEAD_DIM, SSM_STATE), 0.1),
        'cache_conv': nrm(ks[8], (N_C, DEC_BATCH, SSM_CONV - 1, SSM_CONV_DIM), 1.0),
        'page_table': page_table,
        'p_prompt': nrm(ks[9], (DEPTH, BATCH, SEQ, PLE_DIM), 1.0),
        'p_sample': nrm(ks[10], (DEPTH, DEC_BATCH, DEC_SEQ, PLE_DIM), 1.0),
        'norm_mix': 1.0 + nrm(ks[11], (DEPTH, D_MODEL), 0.02),
        'norm_ffn': 1.0 + nrm(ks[12], (DEPTH, D_MODEL), 0.02),
        'w_ab_in': nrm(ks[13], (N_A, D_MODEL, AB_IN), D_MODEL ** -0.5),
        'w_cmp1': nrm(ks[14], (N_A, 2, CMP_BLOCK * NSA_HEAD_DIM, CMP_HIDDEN), (CMP_BLOCK * NSA_HEAD_DIM) ** -0.5),
        'w_cmp2': nrm(ks[15], (N_A, 2, CMP_HIDDEN, NSA_HEAD_DIM), CMP_HIDDEN ** -0.5),
        'cmp_pe': nrm(ks[16], (N_A, 2, CMP_BLOCK, NSA_HEAD_DIM), 0.1),
        'hg_lb_logits': nrm(ks[17], (N_A + 1, HG_QF_W), 0.5),
        'hg_norm': 1.0 + nrm(ks[18], (N_A, HG_V_W), 0.02),
        'w_ab_out': nrm(ks[19], (N_A, AB_MIX, D_MODEL), AB_MIX ** -0.5),
        'w_ssm_in': nrm(ks[20], (N_C, D_MODEL, SSM_IN), D_MODEL ** -0.5),
        'ssm_conv_w': nrm(ks[21], (N_C, SSM_CONV, SSM_CONV_DIM), SSM_CONV ** -0.5),
        'ssm_conv_b': nrm(ks[22], (N_C, SSM_CONV_DIM), 0.01),
        'ssm_dt_bias': dt0 + jnp.log(-jnp.expm1(-dt0)),
        'ssm_a_log': jnp.log(jax.random.uniform(ks[23], (N_C, SSM_HEADS), f32, 1.0, 16.0)),
        'ssm_d': 1.0 + nrm(ks[24], (N_C, SSM_HEADS), 0.01),
        'ssm_norm': 1.0 + nrm(ks[25], (N_C, SSM_D_INNER), 0.02),
        'w_ssm_out': nrm(ks[26], (N_C, SSM_D_INNER, D_MODEL), SSM_D_INNER ** -0.5),
        'w_ffn_in': nrm(ks[27], (DEPTH, D_MODEL, 2 * D_FF), D_MODEL ** -0.5),
        'w_ffn_out': nrm(ks[28], (DEPTH, D_FF, D_MODEL), D_FF ** -0.5),
        'w_ple_up': nrm(ks[29], (DEPTH, PLE_DIM, D_MODEL), PLE_DIM ** -0.5),
        'w_ple_gate': nrm(ks[30], (DEPTH, D_MODEL, D_MODEL), D_MODEL ** -0.5),
        'norm_ple': 1.0 + nrm(ks[31], (DEPTH, D_MODEL), 0.02),
        'norm_final': 1.0 + nrm(ks[32], (D_MODEL,), 0.02),
    }


def reference(x_prompt, x_sample, cache_nsa_kv, cache_swa_kv, state_hgrn, state_ssm, cache_conv, page_table,
              p_prompt, p_sample, norm_mix, norm_ffn, w_ab_in, w_cmp1, w_cmp2, cmp_pe, hg_lb_logits, hg_norm,
              w_ab_out, w_ssm_in, ssm_conv_w, ssm_conv_b, ssm_dt_bias, ssm_a_log, ssm_d, ssm_norm, w_ssm_out,
              w_ffn_in, w_ffn_out, w_ple_up, w_ple_gate, norm_ple, norm_final):
    f32 = jnp.float32
    bp, lp, _ = x_prompt.shape
    bs, ls, _ = x_sample.shape
    n_pages = page_table.shape[1]
    pos_p = jnp.arange(lp)
    pos_s = PAST_LEN + jnp.arange(ls)
    lb_all = jnp.cumsum(jax.nn.softmax(hg_lb_logits.astype(f32), axis=0), axis=0)
    xp, xs = x_prompt, x_sample
    nsa_p, nsa_s, swa_p, swa_s, hg_p, hg_s, ssm_p, ssm_s, cv_p, cv_s = [], [], [], [], [], [], [], [], [], []
    for i in range(DEPTH):
        hp = rms_norm(xp, norm_mix[i])
        hs = rms_norm(xs, norm_mix[i])
        if i % 2 == 0:
            a = i // 2
            wa = (lb_all[a], w_ab_in[a], w_cmp1[a], w_cmp2[a], cmp_pe[a], hg_norm[a], w_ab_out[a])
            yp, r_p, b_p, s_p = nsa_hgrn_mixer(hp, pos_p, None, None, jnp.zeros((bp, HG_HEADS, HG_DK, HG_DV), f32), *wa)
            past = cache_nsa_kv[a][page_table].reshape((bs, n_pages * PAGE_SIZE) + cache_nsa_kv.shape[3:])
            ys, r_s, b_s, s_s = nsa_hgrn_mixer(hs, pos_s, past, cache_swa_kv[a], state_hgrn[a], *wa)
            nsa_p.append(r_p)
            nsa_s.append(r_s)
            swa_p.append(b_p)
            swa_s.append(b_s)
            hg_p.append(s_p)
            hg_s.append(s_s)
        else:
            c = i // 2
            wc = (w_ssm_in[c], ssm_conv_w[c], ssm_conv_b[c], ssm_dt_bias[c], ssm_a_log[c], ssm_d[c], ssm_norm[c], w_ssm_out[c])
            yp, c_p, t_p = mamba_mixer(hp, jnp.zeros((bp, SSM_CONV - 1, SSM_CONV_DIM), hp.dtype),
                                       jnp.zeros((bp, SSM_HEADS, SSM_HEAD_DIM, SSM_STATE), f32), *wc)
            ys, c_s, t_s = mamba_mixer(hs, cache_conv[c], state_ssm[c], *wc)
            cv_p.append(c_p)
            cv_s.append(c_s)
            ssm_p.append(t_p)
            ssm_s.append(t_s)
        xp = xp + yp
        xs = xs + ys
        xp = xp + swiglu(rms_norm(xp, norm_ffn[i]), w_ffn_in[i], w_ffn_out[i])
        xs = xs + swiglu(rms_norm(xs, norm_ffn[i]), w_ffn_in[i], w_ffn_out[i])
        xp = ple_add(xp, p_prompt[i], w_ple_up[i], w_ple_gate[i], norm_ple[i])
        xs = ple_add(xs, p_sample[i], w_ple_up[i], w_ple_gate[i], norm_ple[i])
    y_prompt = rms_norm(xp, norm_final)
    y_sample = rms_norm(xs, norm_final)
    return (y_prompt, y_sample, jnp.stack(nsa_p), jnp.stack(nsa_s), jnp.stack(swa_p), jnp.stack(swa_s),
            jnp.stack(hg_p), jnp.stack(hg_s), jnp.stack(ssm_p), jnp.stack(ssm_s), jnp.stack(cv_p), jnp.stack(cv_s))
```

```python
import functools
import math

import jax
import jax.numpy as jnp
import numpy as np
from jax import lax
from jax.experimental import pallas as pl
from jax.experimental.pallas import tpu as pltpu

D_MODEL = 4096
PAST_LEN = 16384
PAGE_SIZE = 128
PLE_DIM = 256
NSA_HEADS = 16
NSA_KV_HEADS = 4
NSA_GROUP = NSA_HEADS // NSA_KV_HEADS
NSA_HEAD_DIM = 128
NSA_Q = NSA_HEADS * NSA_HEAD_DIM
NSA_KV_W = 6 * NSA_KV_HEADS * NSA_HEAD_DIM
NSA_GATE_W = 3 * NSA_HEADS
CMP_BLOCK = 32
CMP_STRIDE = 16
CMP_HIDDEN = 256
SLC_BLOCK = 64
N_SELECT = 16
WINDOW = 512
SLC_QBLOCK = 32
SWA_QBLOCK = 128
FORCE_SCORE = 1e4
HG_HEADS = 16
HG_DK = 128
HG_DV = (D_MODEL // 2) // HG_HEADS
HG_QF_W = HG_HEADS * HG_DK
HG_V_W = HG_HEADS * HG_DV
HG_CHUNK = 32
AB_SIZES = (NSA_Q, NSA_KV_W, NSA_GATE_W, HG_QF_W, HG_QF_W, HG_V_W, HG_V_W)
SSM_D_INNER = 2 * D_MODEL
SSM_HEAD_DIM = 64
SSM_HEADS = SSM_D_INNER // SSM_HEAD_DIM
SSM_GROUPS = 8
SSM_STATE = 128
SSM_CONV = 4
SSM_CONV_DIM = SSM_D_INNER + 2 * SSM_GROUPS * SSM_STATE
SSM_CHUNK = 64
D_FF = ((8 * D_MODEL + 3 * 256 - 1) // (3 * 256)) * 256
EPS = 1e-6
NEG = -1e30

LANE = 128
VMEM_LIMIT_BYTES = 56 * 1024 * 1024
BF16 = jnp.bfloat16
F32 = jnp.float32


def _mm_kernel(x_ref, w_ref, o_ref, *, nk):
    acc = jnp.dot(x_ref[...], w_ref[...], preferred_element_type=F32)
    if nk == 1:
        o_ref[...] = acc
    else:
        k = pl.program_id(2)

        @pl.when(k == 0)
        def _():
            o_ref[...] = acc

        @pl.when(k > 0)
        def _():
            o_ref[...] += acc


def _mm_tiles(m, k, n):
    tm = min(m, 1024)
    units = k // LANE
    for tn in (1024, 512, 256):
        tn = min(n, tn)
        budget = VMEM_LIMIT_BYTES - 2 * tm * tn * 4 - (4 << 20)
        for parts in range(1, units + 1):
            if units % parts:
                continue
            tk = (units // parts) * LANE
            if 2 * (tm + tn) * tk * 2 <= budget and (tk >= 1024 or parts == 1):
                return tm, tn, tk
    raise ValueError(f"no matmul tiling for {(m, k, n)}")


def _mm(x, w):
    m, k = x.shape
    k2, n = w.shape
    assert k == k2 and k % LANE == 0
    tm, tn, tk = _mm_tiles(m, k, n)
    assert m % tm == 0 and k % tk == 0
    nk = k // tk
    return pl.pallas_call(
        functools.partial(_mm_kernel, nk=nk),
        grid=(m // tm, pl.cdiv(n, tn), nk),
        in_specs=[pl.BlockSpec((tm, tk), lambda i, j, kk: (i, kk)),
                  pl.BlockSpec((tk, tn), lambda i, j, kk: (kk, j))],
        out_specs=pl.BlockSpec((tm, tn), lambda i, j, kk: (i, j)),
        out_shape=jax.ShapeDtypeStruct((m, n), F32),
        compiler_params=pltpu.CompilerParams(
            dimension_semantics=("arbitrary", "arbitrary", "arbitrary"),
            vmem_limit_bytes=VMEM_LIMIT_BYTES),
        name="mm",
    )(x, w)


def _proj(h, wb):
    b, l, k = h.shape
    return _mm(h.reshape(b * l, k).astype(BF16), wb).reshape(b, l, wb.shape[1])


def rms_norm(x, g):
    xf = x.astype(F32)
    y = xf * lax.rsqrt(jnp.mean(xf * xf, axis=-1, keepdims=True) + EPS)
    return (y * g.astype(F32)).astype(x.dtype)


def split_cols(a, sizes):
    out, o = [], 0
    for s in sizes:
        out.append(a[..., o:o + s])
        o += s
    return out


def to_chunks(a, c):
    b, l = a.shape[:2]
    pad = (-l) % c
    a = jnp.pad(a.astype(F32), [(0, 0), (0, pad)] + [(0, 0)] * (a.ndim - 2))
    return jnp.moveaxis(a.reshape((b, (l + pad) // c, c) + a.shape[2:]), 1, 0)


def from_chunks(a, l):
    a = jnp.moveaxis(a, 0, 1)
    return a.reshape((a.shape[0], a.shape[1] * a.shape[2]) + a.shape[3:])[:, :l]


def gqa_attend(q, k, v, mask):
    s = jnp.einsum('...qgrd,...kgd->...qgrk', q, k, preferred_element_type=F32) * (q.shape[-1] ** -0.5)
    m = mask[..., :, None, None, :]
    p = jnp.where(m, jax.nn.softmax(jnp.where(m, s, NEG), axis=-1), 0.0)
    o = jnp.einsum('...qgrk,...kgd->...qgrd', p.astype(v.dtype), v)
    return o, p


def compress_rows(rows, w1, w2, pe):
    b, t = rows.shape[:2]
    n_sub = t // CMP_STRIDE
    n_per = CMP_BLOCK // CMP_STRIDE
    n_cmp = n_sub - n_per + 1
    sub = rows[:, :n_sub * CMP_STRIDE].reshape(b, n_sub, CMP_STRIDE, NSA_KV_HEADS, NSA_HEAD_DIM)
    sub = jnp.moveaxis(sub, 3, 2).reshape(b, n_sub, NSA_KV_HEADS, CMP_STRIDE * NSA_HEAD_DIM)
    w1r = w1.reshape(n_per, CMP_STRIDE * NSA_HEAD_DIM, CMP_HIDDEN)
    hid = pe.reshape(-1) @ w1
    for m in range(n_per):
        hid = hid + jnp.einsum('bngc,ch->bngh', sub[:, m:m + n_cmp], w1r[m])
    return jax.nn.silu(hid) @ w2


def slc_attend(q, pos, kb, vb, idx, valid):
    b, lq, g, r, d = q.shape
    k = idx.shape[-1]
    ii = jnp.moveaxis(idx, 2, 1).reshape(b, g, lq * k)
    bi = jnp.arange(b)[:, None, None]
    gi = jnp.arange(g)[None, :, None]
    kg = kb[bi, gi, ii].reshape(b, g, lq, k, SLC_BLOCK, d)
    vg = vb[bi, gi, ii].reshape(b, g, lq, k, SLC_BLOCK, d)
    s = jnp.einsum('bqgrd,bgqksd->bqgrks', q, kg, preferred_element_type=F32) * (d ** -0.5)
    kpos = idx[..., None] * SLC_BLOCK + jnp.arange(SLC_BLOCK)
    mask = ((kpos <= pos[None, :, None, None, None]) & valid[..., None])[:, :, :, None]
    s = jnp.where(mask, s, NEG).reshape(b, lq, g, r, k * SLC_BLOCK)
    p = jax.nn.softmax(s, axis=-1).reshape(b, lq, g, r, k, SLC_BLOCK)
    return jnp.einsum('bqgrks,bgqksd->bqgrd', p.astype(vg.dtype), vg)


def nsa_cmp_slc(q, pos, rows, w1, w2, pe):
    b, l, g, r, d = q.shape
    t = rows.shape[1]
    kc = compress_rows(rows[:, :, 0], w1[0], w2[0], pe[0])
    vc = compress_rows(rows[:, :, 1], w1[1], w2[1], pe[1])
    n_cmp = kc.shape[1]
    cmp_last = jnp.arange(n_cmp) * CMP_STRIDE + (CMP_BLOCK - 1)
    o_cmp, p_cmp = gqa_attend(q, kc, vc, cmp_last[None, :] <= pos[:, None])
    n_slc = -(-t // SLC_BLOCK)
    c0 = np.arange(n_cmp)[:, None] * CMP_STRIDE
    s0 = np.arange(n_slc)[None, :] * SLC_BLOCK
    cover = np.clip(np.minimum(c0 + CMP_BLOCK, s0 + SLC_BLOCK) - np.maximum(c0, s0), 0, None) / CMP_BLOCK
    score = jnp.einsum('blgrn,nj->blgj', p_cmp, jnp.asarray(cover, F32))
    blk = jnp.arange(n_slc)[None, :]
    cur = (pos // SLC_BLOCK)[:, None]
    causal = blk * SLC_BLOCK <= pos[:, None]
    forced = (blk == 0) | (blk == cur) | (blk == cur - 1)
    score = jnp.where(forced[None, :, None, :], FORCE_SCORE, jnp.where(causal[None, :, None, :], score, -1.0))
    vals, idx = lax.top_k(score, min(N_SELECT, n_slc))
    valid = vals >= 0.0
    kv = jnp.pad(rows[:, :, 2:4], ((0, 0), (0, n_slc * SLC_BLOCK - t), (0, 0), (0, 0), (0, 0)))
    kv = jnp.transpose(kv.reshape(b, n_slc, SLC_BLOCK, 2, g, d), (3, 0, 4, 1, 2, 5))
    kb, vb = kv[0], kv[1]
    qb = math.gcd(l, SLC_QBLOCK)
    nb = l // qb

    def blocks(a):
        return jnp.moveaxis(a.reshape((b, nb, qb) + a.shape[2:]), 1, 0)

    o_slc = lax.map(lambda a: slc_attend(a[0], a[1], kb, vb, a[2], a[3]),
                    (blocks(q), pos.reshape(nb, qb), blocks(idx), blocks(valid)))
    o_slc = jnp.moveaxis(o_slc, 0, 1).reshape(b, l, g, r, d)
    return o_cmp, o_slc


def swa_banded(q, kv):
    b, t, g, r, d = q.shape
    qb = math.gcd(t, SWA_QBLOCK)
    nb = t // qb
    span = qb + WINDOW
    kvp = jnp.pad(kv, ((0, 0), (WINDOW, 0), (0, 0), (0, 0), (0, 0)))
    idx = np.arange(nb)[:, None] * qb + np.arange(span)[None, :]
    kvb = kvp[:, idx]
    qpos = np.arange(t).reshape(nb, qb)
    kpos = idx - WINDOW
    mask = (kpos[:, None, :] >= 0) & (kpos[:, None, :] <= qpos[:, :, None]) & (qpos[:, :, None] - kpos[:, None, :] < WINDOW)
    o, _ = gqa_attend(q.reshape(b, nb, qb, g, r, d), kvb[:, :, :, 0], kvb[:, :, :, 1], jnp.asarray(mask))
    return o.reshape(b, t, g, r, d)


def swa_buffered(q, pos, kv_new, buf):
    wb = buf.shape[1]
    keys = jnp.concatenate([buf.astype(kv_new.dtype), kv_new], axis=1)
    kpos = PAST_LEN - wb + jnp.arange(keys.shape[1])
    mask = (kpos[None, :] <= pos[:, None]) & (pos[:, None] - kpos[None, :] < WINDOW)
    o, _ = gqa_attend(q, keys[:, :, 0], keys[:, :, 1], mask)
    return o, keys[:, keys.shape[1] - wb:]


def gla_chunked(q, k, v, logf, s0):
    b, l = q.shape[:2]
    c = min(HG_CHUNK, l)
    mid = (c - 1) // 2
    tril = jnp.tril(jnp.ones((c, c), bool))

    def step(s, inp):
        qc, kc, vc, gc = inp
        bc = jnp.cumsum(gc, axis=1)
        bm = bc[:, mid:mid + 1]
        a = jnp.einsum('bthd,bshd->bhts', qc * jnp.exp(bc - bm), kc * jnp.exp(bm - bc))
        a = jnp.where(tril, a, 0.0)
        o = jnp.einsum('bhts,bshv->bthv', a, vc) + jnp.einsum('bthd,bhdv->bthv', qc * jnp.exp(bc), s)
        bl = bc[:, -1]
        s = jnp.exp(bl)[..., None] * s + jnp.einsum('bshd,bshv->bhdv', kc * jnp.exp(bl[:, None] - bc), vc)
        return s, o

    s, o = lax.scan(step, s0.astype(F32), (to_chunks(q, c), to_chunks(k, c), to_chunks(v, c), to_chunks(logf, c)))
    return from_chunks(o, l), s


def hgrn2(hq, hf, hi, hg, lb, s0, g_norm):
    b, l, _ = hq.shape
    q = jax.nn.silu(hq.astype(F32)).reshape(b, l, HG_HEADS, HG_DK)
    f = (lb + (1.0 - lb) * jax.nn.sigmoid(hf.astype(F32))).reshape(b, l, HG_HEADS, HG_DK)
    v = hi.astype(F32).reshape(b, l, HG_HEADS, HG_DV)
    o, s = gla_chunked(q, 1.0 - f, v, jnp.log(f), s0)
    o = o * lax.rsqrt(jnp.mean(o * o, axis=-1, keepdims=True) + EPS)
    o = o.reshape(b, l, HG_V_W) * g_norm.astype(F32) * jax.nn.silu(hg.astype(F32))
    return o, s


def nsa_hgrn_mixer(h, pos, past_rows, swa_buf, hg_state, lb, w_in, w1, w2, pe, hg_g, w_out):
    b, l, _ = h.shape
    q, kv, g, hq, hf, hi, hgate = split_cols(_proj(h, w_in), AB_SIZES)
    q = q.reshape(b, l, NSA_KV_HEADS, NSA_GROUP, NSA_HEAD_DIM)
    kv = kv.reshape(b, l, 6, NSA_KV_HEADS, NSA_HEAD_DIM)
    rows = kv[:, :, :4] if past_rows is None else jnp.concatenate([past_rows.astype(kv.dtype), kv[:, :, :4]], axis=1)
    o_cmp, o_slc = nsa_cmp_slc(q, pos, rows, w1, w2, pe)
    if swa_buf is None:
        o_swa = swa_banded(q, kv[:, :, 4:])
        new_buf = kv[:, l - min(WINDOW, l):, 4:]
    else:
        o_swa, new_buf = swa_buffered(q, pos, kv[:, :, 4:], swa_buf)
    gates = jax.nn.sigmoid(g.astype(F32)).reshape(b, l, NSA_KV_HEADS, NSA_GROUP, 3)
    o_nsa = gates[..., 0:1] * o_cmp + gates[..., 1:2] * o_slc + gates[..., 2:3] * o_swa
    o_hg, s_new = hgrn2(hq, hf, hi, hgate, lb, hg_state, hg_g)
    mix = jnp.concatenate([o_nsa.reshape(b, l, NSA_Q).astype(h.dtype), o_hg.astype(h.dtype)], axis=-1)
    return _proj(mix, w_out), kv[:, :, :4], new_buf, s_new


def ssd_chunked(x, dt, a, bm, cm, s0):
    b, l, nh, p = x.shape
    g, n = bm.shape[2], bm.shape[3]
    r = nh // g
    c = min(SSM_CHUNK, l)
    tril = jnp.tril(jnp.ones((c, c), bool))

    def step(s, inp):
        xc, dtc, bc, cc = inp
        cum = jnp.cumsum(dtc * a, axis=1)
        seg = cum[:, :, None, :] - cum[:, None, :, :]
        lm = jnp.exp(jnp.where(tril[None, :, :, None], seg, -jnp.inf)).reshape(b, c, c, g, r)
        xdt = (xc * dtc[..., None]).reshape(b, c, g, r, p)
        cb = jnp.einsum('btgn,bsgn->btsg', cc, bc)
        sg = s.reshape(b, g, r, p, n)
        y = jnp.einsum('btsg,btsgr,bsgrp->btgrp', cb, lm, xdt)
        y = y + jnp.einsum('btgn,bgrpn->btgrp', cc, sg) * jnp.exp(cum).reshape(b, c, g, r)[..., None]
        dec = jnp.exp(cum[:, -1:] - cum).reshape(b, c, g, r)
        sg = jnp.exp(cum[:, -1]).reshape(b, g, r)[..., None, None] * sg + jnp.einsum('bsgn,bsgrp->bgrpn', bc, xdt * dec[..., None])
        return sg.reshape(b, nh, p, n), y.reshape(b, c, nh, p)

    s, y = lax.scan(step, s0.astype(F32), (to_chunks(x, c), to_chunks(dt, c), to_chunks(bm, c), to_chunks(cm, c)))
    return from_chunks(y, l), s


def mamba_mixer(h, conv_state, ssm_state, w_in, conv_w, conv_b, dt_bias, a_log, d_skip, norm_g, w_out):
    b, l, _ = h.shape
    z, xbc, dt = split_cols(_proj(h, w_in), (SSM_D_INNER, SSM_CONV_DIM, SSM_HEADS))
    xpad = jnp.concatenate([conv_state.astype(xbc.dtype), xbc], axis=1)
    acc = conv_b.astype(F32)
    for j in range(SSM_CONV):
        acc = acc + xpad[:, j:j + l].astype(F32) * conv_w[j].astype(F32)
    xbc = jax.nn.silu(acc)
    new_conv = xpad[:, xpad.shape[1] - (SSM_CONV - 1):]
    xs, bm, cm = split_cols(xbc, (SSM_D_INNER, SSM_GROUPS * SSM_STATE, SSM_GROUPS * SSM_STATE))
    xs = xs.reshape(b, l, SSM_HEADS, SSM_HEAD_DIM)
    bm = bm.reshape(b, l, SSM_GROUPS, SSM_STATE)
    cm = cm.reshape(b, l, SSM_GROUPS, SSM_STATE)
    dt = jax.nn.softplus(dt.astype(F32) + dt_bias.astype(F32))
    a = -jnp.exp(a_log.astype(F32))
    y, s = ssd_chunked(xs, dt, a, bm, cm, ssm_state)
    y = y + d_skip.astype(F32)[:, None] * xs
    y = (y.reshape(b, l, SSM_D_INNER) * jax.nn.silu(z.astype(F32))).reshape(b, l, SSM_GROUPS, SSM_D_INNER // SSM_GROUPS)
    y = (y * lax.rsqrt(jnp.mean(y * y, axis=-1, keepdims=True) + EPS)).reshape(b, l, SSM_D_INNER) * norm_g.astype(F32)
    return _proj(y.astype(h.dtype), w_out), new_conv, s


def swiglu(h, w_in, w_out):
    gt, up = split_cols(_proj(h, w_in), (D_FF, D_FF))
    return _proj(jax.nn.silu(gt) * up, w_out)


def ple_add(x, p, w_up, w_gate, g):
    return x + (_proj(p.astype(x.dtype), w_up) * jax.nn.sigmoid(_proj(rms_norm(x, g), w_gate))).astype(x.dtype)


def kernel(x_prompt, x_sample, cache_nsa_kv, cache_swa_kv, state_hgrn, state_ssm, cache_conv, page_table, p_prompt, p_sample, norm_mix, norm_ffn, w_ab_in, w_cmp1, w_cmp2, cmp_pe, hg_lb_logits, hg_norm, w_ab_out, w_ssm_in, ssm_conv_w, ssm_conv_b, ssm_dt_bias, ssm_a_log, ssm_d, ssm_norm, w_ssm_out, w_ffn_in, w_ffn_out, w_ple_up, w_ple_gate, norm_ple, norm_final):
    depth = norm_mix.shape[0]
    bp, lp, _ = x_prompt.shape
    bs, ls, _ = x_sample.shape
    n_pages = page_table.shape[1]
    pos_p = jnp.arange(lp)
    pos_s = PAST_LEN + jnp.arange(ls)
    lb_all = jnp.cumsum(jax.nn.softmax(hg_lb_logits.astype(F32), axis=0), axis=0)
    xp, xs = x_prompt, x_sample
    nsa_p, nsa_s, swa_p, swa_s, hg_p, hg_s, ssm_p, ssm_s, cv_p, cv_s = [], [], [], [], [], [], [], [], [], []
    for i in range(depth):
        hp = rms_norm(xp, norm_mix[i])
        hs = rms_norm(xs, norm_mix[i])
        if i % 2 == 0:
            a = i // 2
            wa = (lb_all[a], w_ab_in[a].astype(BF16), w_cmp1[a], w_cmp2[a], cmp_pe[a], hg_norm[a],
                  w_ab_out[a].astype(BF16))
            yp, r_p, b_p, s_p = nsa_hgrn_mixer(hp, pos_p, None, None, jnp.zeros((bp, HG_HEADS, HG_DK, HG_DV), F32), *wa)
            past = cache_nsa_kv[a][page_table].reshape((bs, n_pages * PAGE_SIZE) + cache_nsa_kv.shape[3:])
            ys, r_s, b_s, s_s = nsa_hgrn_mixer(hs, pos_s, past, cache_swa_kv[a], state_hgrn[a], *wa)
            nsa_p.append(r_p)
            nsa_s.append(r_s)
            swa_p.append(b_p)
            swa_s.append(b_s)
            hg_p.append(s_p)
            hg_s.append(s_s)
        else:
            c = i // 2
            wc = (w_ssm_in[c].astype(BF16), ssm_conv_w[c], ssm_conv_b[c], ssm_dt_bias[c], ssm_a_log[c], ssm_d[c],
                  ssm_norm[c], w_ssm_out[c].astype(BF16))
            yp, c_p, t_p = mamba_mixer(hp, jnp.zeros((bp, SSM_CONV - 1, SSM_CONV_DIM), hp.dtype),
                                       jnp.zeros((bp, SSM_HEADS, SSM_HEAD_DIM, SSM_STATE), F32), *wc)
            ys, c_s, t_s = mamba_mixer(hs, cache_conv[c], state_ssm[c], *wc)
            cv_p.append(c_p)
            cv_s.append(c_s)
            ssm_p.append(t_p)
            ssm_s.append(t_s)
        xp = xp + yp
        xs = xs + ys
        wfi = w_ffn_in[i].astype(BF16)
        wfo = w_ffn_out[i].astype(BF16)
        xp = xp + swiglu(rms_norm(xp, norm_ffn[i]), wfi, wfo)
        xs = xs + swiglu(rms_norm(xs, norm_ffn[i]), wfi, wfo)
        wpu = w_ple_up[i].astype(BF16)
        wpg = w_ple_gate[i].astype(BF16)
        xp = ple_add(xp, p_prompt[i], wpu, wpg, norm_ple[i])
        xs = ple_add(xs, p_sample[i], wpu, wpg, norm_ple[i])
    y_prompt = rms_norm(xp, norm_final)
    y_sample = rms_norm(xs, norm_final)
    return (y_prompt, y_sample, jnp.stack(nsa_p), jnp.stack(nsa_s), jnp.stack(swa_p), jnp.stack(swa_s),
            jnp.stack(hg_p), jnp.stack(hg_s), jnp.stack(ssm_p), jnp.stack(ssm_s), jnp.stack(cv_p), jnp.stack(cv_s))
```

```python
import functools
import math

import jax
import jax.numpy as jnp
import numpy as np
from jax import lax
from jax.experimental import pallas as pl
from jax.experimental.pallas import tpu as pltpu

D_MODEL = 4096
PAST_LEN = 16384
PAGE_SIZE = 128
PLE_DIM = 256
NSA_HEADS = 16
NSA_KV_HEADS = 4
NSA_GROUP = NSA_HEADS // NSA_KV_HEADS
NSA_HEAD_DIM = 128
NSA_Q = NSA_HEADS * NSA_HEAD_DIM
NSA_KV_W = 6 * NSA_KV_HEADS * NSA_HEAD_DIM
NSA_GATE_W = 3 * NSA_HEADS
CMP_BLOCK = 32
CMP_STRIDE = 16
CMP_HIDDEN = 256
SLC_BLOCK = 64
N_SELECT = 16
WINDOW = 512
SLC_QBLOCK = 32
SWA_QBLOCK = 128
FORCE_SCORE = 1e4
HG_HEADS = 16
HG_DK = 128
HG_DV = (D_MODEL // 2) // HG_HEADS
HG_QF_W = HG_HEADS * HG_DK
HG_V_W = HG_HEADS * HG_DV
HG_CHUNK = 32
AB_SIZES = (NSA_Q, NSA_KV_W, NSA_GATE_W, HG_QF_W, HG_QF_W, HG_V_W, HG_V_W)
SSM_D_INNER = 2 * D_MODEL
SSM_HEAD_DIM = 64
SSM_HEADS = SSM_D_INNER // SSM_HEAD_DIM
SSM_GROUPS = 8
SSM_STATE = 128
SSM_CONV = 4
SSM_CONV_DIM = SSM_D_INNER + 2 * SSM_GROUPS * SSM_STATE
SSM_CHUNK = 64
D_FF = ((8 * D_MODEL + 3 * 256 - 1) // (3 * 256)) * 256
EPS = 1e-6
NEG = -1e30

LANE = 128
VMEM_LIMIT_BYTES = 56 * 1024 * 1024
BF16 = jnp.bfloat16
F32 = jnp.float32


def _mm_kernel(x_ref, w_ref, o_ref, *, nk):
    acc = jnp.dot(x_ref[...], w_ref[...], preferred_element_type=F32)
    if nk == 1:
        o_ref[...] = acc
    else:
        k = pl.program_id(2)

        @pl.when(k == 0)
        def _():
            o_ref[...] = acc

        @pl.when(k > 0)
        def _():
            o_ref[...] += acc


def _mm_tiles(m, k, n):
    tm = min(m, 1024)
    units = k // LANE
    for tn in (1024, 512, 256):
        tn = min(n, tn)
        budget = VMEM_LIMIT_BYTES - 2 * tm * tn * 4 - (4 << 20)
        for parts in range(1, units + 1):
            if units % parts:
                continue
            tk = (units // parts) * LANE
            if 2 * (tm + tn) * tk * 2 <= budget and (tk >= 1024 or parts == 1):
                return tm, tn, tk
    raise ValueError(f"no matmul tiling for {(m, k, n)}")


def _mm(x, w):
    m, k = x.shape
    k2, n = w.shape
    assert k == k2 and k % LANE == 0
    tm, tn, tk = _mm_tiles(m, k, n)
    assert m % tm == 0 and k % tk == 0
    nk = k // tk
    return pl.pallas_call(
        functools.partial(_mm_kernel, nk=nk),
        grid=(m // tm, pl.cdiv(n, tn), nk),
        in_specs=[pl.BlockSpec((tm, tk), lambda i, j, kk: (i, kk)),
                  pl.BlockSpec((tk, tn), lambda i, j, kk: (kk, j))],
        out_specs=pl.BlockSpec((tm, tn), lambda i, j, kk: (i, j)),
        out_shape=jax.ShapeDtypeStruct((m, n), F32),
        compiler_params=pltpu.CompilerParams(
            dimension_semantics=("arbitrary", "arbitrary", "arbitrary"),
            vmem_limit_bytes=VMEM_LIMIT_BYTES),
        name="mm",
    )(x, w)


def _proj(h, wb):
    b, l, k = h.shape
    return _mm(h.reshape(b * l, k).astype(BF16), wb).reshape(b, l, wb.shape[1])


NSA_TQ = 128


def _dot_nt(a, b):
    return lax.dot_general(a, b, (((1,), (1,)), ((), ())), preferred_element_type=F32)


def _masked_attend(q, k, v, mask, scale):
    s = jnp.where(mask, _dot_nt(q, k) * scale, NEG)
    e = jnp.exp(s - jnp.max(s, axis=-1, keepdims=True))
    den = jnp.sum(e, axis=-1, keepdims=True)
    return jnp.dot(e.astype(BF16), v, preferred_element_type=F32) / den


def _nsa_prompt_kernel(q_ref, kc_ref, vc_ref, ks_ref, vs_ref, kw_ref, vw_ref, gate_ref, cover_ref, expand_ref,
                       o_ref, *, seq):
    tq = NSA_TQ
    n_cmp = seq // CMP_STRIDE - CMP_BLOCK // CMP_STRIDE + 1
    n_slc = seq // SLC_BLOCK
    span = WINDOW + tq
    scale = NSA_HEAD_DIM ** -0.5
    q0 = pl.program_id(2) * tq
    pos = q0 + lax.broadcasted_iota(jnp.int32, (tq, 1), 0)
    lane = lax.broadcasted_iota(jnp.int32, (tq, LANE), 1)

    cmp_ok = (lane * CMP_STRIDE + (CMP_BLOCK - 1) <= pos) & (lane < n_cmp)
    kc = kc_ref[...]
    vc = vc_ref[...]
    psum = jnp.zeros((tq, LANE), F32)
    o_cmp = []
    for r in range(NSA_GROUP):
        qr = q_ref[:, r * NSA_HEAD_DIM:(r + 1) * NSA_HEAD_DIM]
        s = jnp.where(cmp_ok, _dot_nt(qr, kc) * scale, NEG)
        e = jnp.where(cmp_ok, jnp.exp(s - jnp.max(s, axis=-1, keepdims=True)), 0.0)
        den = jnp.sum(e, axis=-1, keepdims=True)
        p = e / jnp.where(den > 0.0, den, 1.0)
        psum = psum + p
        o_cmp.append(jnp.dot(p.astype(BF16), vc, preferred_element_type=F32))

    score = jnp.dot(psum, cover_ref[...], preferred_element_type=F32, precision=lax.Precision.HIGHEST)
    cur = pos // SLC_BLOCK
    forced = (lane == 0) | (lane == cur) | (lane == cur - 1)
    causal = lane * SLC_BLOCK <= pos
    score = jnp.where(forced, FORCE_SCORE, jnp.where(causal, score, -1.0))
    score = jnp.where(lane < n_slc, score, -2.0)
    rank = jnp.zeros((tq, LANE), jnp.int32)
    for i in range(n_slc):
        ci = score[:, i:i + 1]
        beats = (ci > score) | ((ci == score) & (lane > i))
        rank = rank + beats.astype(jnp.int32)
    sel = ((rank < min(N_SELECT, n_slc)) & (lane < n_slc)).astype(BF16)
    sel_keys = jnp.dot(sel, expand_ref[...], preferred_element_type=F32)
    kpos = lax.broadcasted_iota(jnp.int32, (tq, seq), 1)
    m_slc = (sel_keys > 0.5) & (kpos <= pos)

    w0 = pl.multiple_of(jnp.maximum(q0 - WINDOW, 0), tq)
    kw = kw_ref[pl.ds(w0, span), :]
    vw = vw_ref[pl.ds(w0, span), :]
    wpos = w0 + lax.broadcasted_iota(jnp.int32, (tq, span), 1)
    m_swa = (wpos <= pos) & (pos - wpos < WINDOW)

    ks = ks_ref[...]
    vs = vs_ref[...]
    for r in range(NSA_GROUP):
        qr = q_ref[:, r * NSA_HEAD_DIM:(r + 1) * NSA_HEAD_DIM]
        o_slc = _masked_attend(qr, ks, vs, m_slc, scale)
        o_swa = _masked_attend(qr, kw, vw, m_swa, scale)
        g = gate_ref[:, 3 * r:3 * r + 3]
        o = g[:, 0:1] * o_cmp[r] + g[:, 1:2] * o_slc + g[:, 2:3] * o_swa
        o_ref[:, r * NSA_HEAD_DIM:(r + 1) * NSA_HEAD_DIM] = o.astype(o_ref.dtype)


def _nsa_prompt(qb, kvb, kc, vc, gates, batch, seq):
    assert seq % NSA_TQ == 0 and seq // SLC_BLOCK <= LANE and seq // CMP_STRIDE <= LANE + 1
    nt = seq // NSA_TQ
    n_cmp = seq // CMP_STRIDE - CMP_BLOCK // CMP_STRIDE + 1
    n_slc = seq // SLC_BLOCK
    c0 = np.arange(LANE)[:, None] * CMP_STRIDE
    s0 = np.arange(LANE)[None, :] * SLC_BLOCK
    cover = np.clip(np.minimum(c0 + CMP_BLOCK, s0 + SLC_BLOCK) - np.maximum(c0, s0), 0, None) / CMP_BLOCK
    cover = cover * (np.arange(LANE)[:, None] < n_cmp) * (np.arange(LANE)[None, :] < n_slc)
    expand = (np.arange(seq)[None, :] // SLC_BLOCK == np.arange(LANE)[:, None])
    hd = NSA_HEAD_DIM
    gw = NSA_GROUP * hd

    def kv_spec(slot):
        return pl.BlockSpec((seq, hd), lambda b, g, t: (b, slot * NSA_KV_HEADS + g))

    cmp_spec = pl.BlockSpec((None, None, LANE, hd), lambda b, g, t: (b, g, 0, 0))
    return pl.pallas_call(
        functools.partial(_nsa_prompt_kernel, seq=seq),
        grid=(batch, NSA_KV_HEADS, nt),
        in_specs=[pl.BlockSpec((NSA_TQ, gw), lambda b, g, t: (b * nt + t, g)),
                  cmp_spec, cmp_spec, kv_spec(2), kv_spec(3), kv_spec(4), kv_spec(5),
                  pl.BlockSpec((NSA_TQ, LANE), lambda b, g, t: (b * nt + t, g)),
                  pl.BlockSpec((LANE, LANE), lambda b, g, t: (0, 0)),
                  pl.BlockSpec((LANE, seq), lambda b, g, t: (0, 0))],
        out_specs=pl.BlockSpec((NSA_TQ, gw), lambda b, g, t: (b * nt + t, g)),
        out_shape=jax.ShapeDtypeStruct((batch * seq, NSA_Q), BF16),
        compiler_params=pltpu.CompilerParams(
            dimension_semantics=("arbitrary", "arbitrary", "arbitrary"),
            vmem_limit_bytes=VMEM_LIMIT_BYTES),
        name="nsa_prompt",
    )(qb, kc, vc, kvb, kvb, kvb, kvb, gates, jnp.asarray(cover, F32), jnp.asarray(expand, BF16))


HG_SLAB = 256


def _hgrn_kernel(hq_ref, hf_ref, hi_ref, hg_ref, lb_ref, gn_ref, s0_ref, tri_ref, o_ref, s_ref,
                 qg_s, kg_s, qs_s, kd_s, v_s, el_s, o_s, *, seq):
    c = HG_CHUNK
    hi_prec = lax.Precision.HIGHEST
    lb = lb_ref[...]

    def prep(i, carry):
        rows = pl.ds(pl.multiple_of(i * HG_SLAB, HG_SLAB), HG_SLAB)
        hq = hq_ref[rows, :]
        q = hq * jax.nn.sigmoid(hq)
        f = lb + (1.0 - lb) * jax.nn.sigmoid(hf_ref[rows, :])
        k = 1.0 - f
        bc = jnp.dot(tri_ref[0], jnp.log(f), preferred_element_type=F32, precision=hi_prec)
        bm = jnp.dot(tri_ref[1], bc, preferred_element_type=F32, precision=hi_prec)
        bl = jnp.dot(tri_ref[2], bc, preferred_element_type=F32, precision=hi_prec)
        qg_s[rows, :] = (q * jnp.exp(bc - bm)).astype(qg_s.dtype)
        kg_s[rows, :] = (k * jnp.exp(bm - bc)).astype(kg_s.dtype)
        qs_s[rows, :] = (q * jnp.exp(bc)).astype(qs_s.dtype)
        kd_s[rows, :] = (k * jnp.exp(bl - bc)).astype(kd_s.dtype)
        v_s[rows, :] = hi_ref[rows, :].astype(v_s.dtype)
        el_s[rows, :] = jnp.exp(bl)
        return carry

    lax.fori_loop(0, seq // HG_SLAB, prep, 0)

    tril = lax.broadcasted_iota(jnp.int32, (c, c), 0) >= lax.broadcasted_iota(jnp.int32, (c, c), 1)

    def chunk(ci, st):
        r0 = pl.multiple_of(ci * c, c)
        rows = pl.ds(r0, c)
        kd = kd_s[rows, :]
        v = v_s[rows, :]
        a = jnp.where(tril, _dot_nt(qg_s[rows, :], kg_s[rows, :]), 0.0)
        o = jnp.dot(a.astype(v.dtype), v, preferred_element_type=F32) + _dot_nt(qs_s[rows, :], st.astype(v.dtype))
        o_s[rows, :] = o
        upd = lax.dot_general(v, kd, (((0,), (0,)), ((), ())), preferred_element_type=F32)
        return st * el_s[pl.ds(r0, 1), :] + upd

    st = lax.fori_loop(0, seq // c, chunk, s0_ref[...].T, unroll=2)
    s_ref[...] = st.T

    gn = gn_ref[...]

    def finish(i, carry):
        rows = pl.ds(pl.multiple_of(i * HG_SLAB, HG_SLAB), HG_SLAB)
        o = o_s[rows, :]
        hg = hg_ref[rows, :]
        o = o * lax.rsqrt(jnp.mean(o * o, axis=-1, keepdims=True) + EPS)
        o_ref[rows, :] = (o * gn * (hg * jax.nn.sigmoid(hg))).astype(o_ref.dtype)
        return carry

    lax.fori_loop(0, seq // HG_SLAB, finish, 0)


def _hgrn_tri():
    i = np.arange(HG_SLAB)[:, None]
    j = np.arange(HG_SLAB)[None, :]
    same = (i // HG_CHUNK) == (j // HG_CHUNK)
    cum = same & (j <= i)
    mid = j == (i // HG_CHUNK) * HG_CHUNK + (HG_CHUNK - 1) // 2
    last = j == (i // HG_CHUNK) * HG_CHUNK + HG_CHUNK - 1
    return jnp.asarray(np.stack([cum, mid, last]), F32)


def _hgrn_prompt(proj, col0, lb, g_norm, s0, batch, seq):
    assert seq % HG_SLAB == 0 and col0 % LANE == 0 and HG_DK == LANE and HG_DV == LANE
    c0 = col0 // LANE

    def col_spec(group):
        return pl.BlockSpec((seq, LANE), lambda b, h: (b, c0 + group * HG_HEADS + h))

    vec_spec = pl.BlockSpec((1, LANE), lambda b, h: (0, h))
    st_spec = pl.BlockSpec((None, None, HG_DK, HG_DV), lambda b, h: (b, h, 0, 0))
    return pl.pallas_call(
        functools.partial(_hgrn_kernel, seq=seq),
        grid=(batch, HG_HEADS),
        in_specs=[col_spec(0), col_spec(1), col_spec(2), col_spec(3), vec_spec, vec_spec, st_spec,
                  pl.BlockSpec((3, HG_SLAB, HG_SLAB), lambda b, h: (0, 0, 0))],
        out_specs=[pl.BlockSpec((seq, LANE), lambda b, h: (b, h)), st_spec],
        out_shape=[jax.ShapeDtypeStruct((batch * seq, HG_V_W), BF16),
                   jax.ShapeDtypeStruct((batch, HG_HEADS, HG_DK, HG_DV), F32)],
        scratch_shapes=[pltpu.VMEM((seq, LANE), BF16)] * 5 + [pltpu.VMEM((seq, LANE), F32)] * 2,
        compiler_params=pltpu.CompilerParams(
            dimension_semantics=("arbitrary", "arbitrary"),
            vmem_limit_bytes=VMEM_LIMIT_BYTES),
        name="hgrn_prompt",
    )(proj, proj, proj, proj, lb, g_norm, s0, _hgrn_tri())


def rms_norm(x, g):
    xf = x.astype(F32)
    y = xf * lax.rsqrt(jnp.mean(xf * xf, axis=-1, keepdims=True) + EPS)
    return (y * g.astype(F32)).astype(x.dtype)


def split_cols(a, sizes):
    out, o = [], 0
    for s in sizes:
        out.append(a[..., o:o + s])
        o += s
    return out


def to_chunks(a, c):
    b, l = a.shape[:2]
    pad = (-l) % c
    a = jnp.pad(a.astype(F32), [(0, 0), (0, pad)] + [(0, 0)] * (a.ndim - 2))
    return jnp.moveaxis(a.reshape((b, (l + pad) // c, c) + a.shape[2:]), 1, 0)


def from_chunks(a, l):
    a = jnp.moveaxis(a, 0, 1)
    return a.reshape((a.shape[0], a.shape[1] * a.shape[2]) + a.shape[3:])[:, :l]


def gqa_attend(q, k, v, mask):
    s = jnp.einsum('...qgrd,...kgd->...qgrk', q, k, preferred_element_type=F32) * (q.shape[-1] ** -0.5)
    m = mask[..., :, None, None, :]
    p = jnp.where(m, jax.nn.softmax(jnp.where(m, s, NEG), axis=-1), 0.0)
    o = jnp.einsum('...qgrk,...kgd->...qgrd', p.astype(v.dtype), v)
    return o, p


def compress_rows(rows, w1, w2, pe):
    b, t = rows.shape[:2]
    n_sub = t // CMP_STRIDE
    n_per = CMP_BLOCK // CMP_STRIDE
    n_cmp = n_sub - n_per + 1
    sub = rows[:, :n_sub * CMP_STRIDE].reshape(b, n_sub, CMP_STRIDE, NSA_KV_HEADS, NSA_HEAD_DIM)
    sub = jnp.moveaxis(sub, 3, 2).reshape(b, n_sub, NSA_KV_HEADS, CMP_STRIDE * NSA_HEAD_DIM)
    w1r = w1.reshape(n_per, CMP_STRIDE * NSA_HEAD_DIM, CMP_HIDDEN)
    hid = pe.reshape(-1) @ w1
    for m in range(n_per):
        hid = hid + jnp.einsum('bngc,ch->bngh', sub[:, m:m + n_cmp], w1r[m])
    return jax.nn.silu(hid) @ w2


def slc_attend(q, pos, kb, vb, idx, valid):
    b, lq, g, r, d = q.shape
    k = idx.shape[-1]
    ii = jnp.moveaxis(idx, 2, 1).reshape(b, g, lq * k)
    bi = jnp.arange(b)[:, None, None]
    gi = jnp.arange(g)[None, :, None]
    kg = kb[bi, gi, ii].reshape(b, g, lq, k, SLC_BLOCK, d)
    vg = vb[bi, gi, ii].reshape(b, g, lq, k, SLC_BLOCK, d)
    s = jnp.einsum('bqgrd,bgqksd->bqgrks', q, kg, preferred_element_type=F32) * (d ** -0.5)
    kpos = idx[..., None] * SLC_BLOCK + jnp.arange(SLC_BLOCK)
    mask = ((kpos <= pos[None, :, None, None, None]) & valid[..., None])[:, :, :, None]
    s = jnp.where(mask, s, NEG).reshape(b, lq, g, r, k * SLC_BLOCK)
    p = jax.nn.softmax(s, axis=-1).reshape(b, lq, g, r, k, SLC_BLOCK)
    return jnp.einsum('bqgrks,bgqksd->bqgrd', p.astype(vg.dtype), vg)


def nsa_cmp_slc(q, pos, rows, w1, w2, pe):
    b, l, g, r, d = q.shape
    t = rows.shape[1]
    kc = compress_rows(rows[:, :, 0], w1[0], w2[0], pe[0])
    vc = compress_rows(rows[:, :, 1], w1[1], w2[1], pe[1])
    n_cmp = kc.shape[1]
    cmp_last = jnp.arange(n_cmp) * CMP_STRIDE + (CMP_BLOCK - 1)
    o_cmp, p_cmp = gqa_attend(q, kc, vc, cmp_last[None, :] <= pos[:, None])
    n_slc = -(-t // SLC_BLOCK)
    c0 = np.arange(n_cmp)[:, None] * CMP_STRIDE
    s0 = np.arange(n_slc)[None, :] * SLC_BLOCK
    cover = np.clip(np.minimum(c0 + CMP_BLOCK, s0 + SLC_BLOCK) - np.maximum(c0, s0), 0, None) / CMP_BLOCK
    score = jnp.einsum('blgrn,nj->blgj', p_cmp, jnp.asarray(cover, F32))
    blk = jnp.arange(n_slc)[None, :]
    cur = (pos // SLC_BLOCK)[:, None]
    causal = blk * SLC_BLOCK <= pos[:, None]
    forced = (blk == 0) | (blk == cur) | (blk == cur - 1)
    score = jnp.where(forced[None, :, None, :], FORCE_SCORE, jnp.where(causal[None, :, None, :], score, -1.0))
    vals, idx = lax.top_k(score, min(N_SELECT, n_slc))
    valid = vals >= 0.0
    kv = jnp.pad(rows[:, :, 2:4], ((0, 0), (0, n_slc * SLC_BLOCK - t), (0, 0), (0, 0), (0, 0)))
    kv = jnp.transpose(kv.reshape(b, n_slc, SLC_BLOCK, 2, g, d), (3, 0, 4, 1, 2, 5))
    kb, vb = kv[0], kv[1]
    qb = math.gcd(l, SLC_QBLOCK)
    nb = l // qb

    def blocks(a):
        return jnp.moveaxis(a.reshape((b, nb, qb) + a.shape[2:]), 1, 0)

    o_slc = lax.map(lambda a: slc_attend(a[0], a[1], kb, vb, a[2], a[3]),
                    (blocks(q), pos.reshape(nb, qb), blocks(idx), blocks(valid)))
    o_slc = jnp.moveaxis(o_slc, 0, 1).reshape(b, l, g, r, d)
    return o_cmp, o_slc


def swa_banded(q, kv):
    b, t, g, r, d = q.shape
    qb = math.gcd(t, SWA_QBLOCK)
    nb = t // qb
    span = qb + WINDOW
    kvp = jnp.pad(kv, ((0, 0), (WINDOW, 0), (0, 0), (0, 0), (0, 0)))
    idx = np.arange(nb)[:, None] * qb + np.arange(span)[None, :]
    kvb = kvp[:, idx]
    qpos = np.arange(t).reshape(nb, qb)
    kpos = idx - WINDOW
    mask = (kpos[:, None, :] >= 0) & (kpos[:, None, :] <= qpos[:, :, None]) & (qpos[:, :, None] - kpos[:, None, :] < WINDOW)
    o, _ = gqa_attend(q.reshape(b, nb, qb, g, r, d), kvb[:, :, :, 0], kvb[:, :, :, 1], jnp.asarray(mask))
    return o.reshape(b, t, g, r, d)


def swa_buffered(q, pos, kv_new, buf):
    wb = buf.shape[1]
    keys = jnp.concatenate([buf.astype(kv_new.dtype), kv_new], axis=1)
    kpos = PAST_LEN - wb + jnp.arange(keys.shape[1])
    mask = (kpos[None, :] <= pos[:, None]) & (pos[:, None] - kpos[None, :] < WINDOW)
    o, _ = gqa_attend(q, keys[:, :, 0], keys[:, :, 1], mask)
    return o, keys[:, keys.shape[1] - wb:]


def gla_chunked(q, k, v, logf, s0):
    b, l = q.shape[:2]
    c = min(HG_CHUNK, l)
    mid = (c - 1) // 2
    tril = jnp.tril(jnp.ones((c, c), bool))

    def step(s, inp):
        qc, kc, vc, gc = inp
        bc = jnp.cumsum(gc, axis=1)
        bm = bc[:, mid:mid + 1]
        a = jnp.einsum('bthd,bshd->bhts', qc * jnp.exp(bc - bm), kc * jnp.exp(bm - bc))
        a = jnp.where(tril, a, 0.0)
        o = jnp.einsum('bhts,bshv->bthv', a, vc) + jnp.einsum('bthd,bhdv->bthv', qc * jnp.exp(bc), s)
        bl = bc[:, -1]
        s = jnp.exp(bl)[..., None] * s + jnp.einsum('bshd,bshv->bhdv', kc * jnp.exp(bl[:, None] - bc), vc)
        return s, o

    s, o = lax.scan(step, s0.astype(F32), (to_chunks(q, c), to_chunks(k, c), to_chunks(v, c), to_chunks(logf, c)))
    return from_chunks(o, l), s


def hgrn2(hq, hf, hi, hg, lb, s0, g_norm):
    b, l, _ = hq.shape
    q = jax.nn.silu(hq.astype(F32)).reshape(b, l, HG_HEADS, HG_DK)
    f = (lb + (1.0 - lb) * jax.nn.sigmoid(hf.astype(F32))).reshape(b, l, HG_HEADS, HG_DK)
    v = hi.astype(F32).reshape(b, l, HG_HEADS, HG_DV)
    o, s = gla_chunked(q, 1.0 - f, v, jnp.log(f), s0)
    o = o * lax.rsqrt(jnp.mean(o * o, axis=-1, keepdims=True) + EPS)
    o = o.reshape(b, l, HG_V_W) * g_norm.astype(F32) * jax.nn.silu(hg.astype(F32))
    return o, s


def nsa_prompt_branch(q, kv, g, w1, w2, pe):
    b, l = q.shape[:2]

    def cmp_pad(slot):
        c = compress_rows(kv[:, :, slot], w1[slot], w2[slot], pe[slot])
        c = jnp.pad(c, ((0, 0), (0, LANE - c.shape[1]), (0, 0), (0, 0)))
        return jnp.moveaxis(c, 2, 1).astype(BF16)

    gates = jax.nn.sigmoid(g.astype(F32)).reshape(b * l, NSA_KV_HEADS, 3 * NSA_GROUP)
    gates = jnp.pad(gates, ((0, 0), (0, 0), (0, LANE - 3 * NSA_GROUP))).reshape(b * l, NSA_KV_HEADS * LANE)
    o = _nsa_prompt(q.reshape(b * l, NSA_Q).astype(BF16), kv.reshape(b * l, NSA_KV_W).astype(BF16),
                    cmp_pad(0), cmp_pad(1), gates, b, l)
    return o.reshape(b, l, NSA_Q)


AB_HG_COL = NSA_Q + NSA_KV_W
AB_GATE_COL = AB_HG_COL + 2 * HG_QF_W + 2 * HG_V_W
AB_PACKED = -(-(AB_GATE_COL + NSA_GATE_W) // LANE) * LANE


def pack_ab_in(w):
    g0 = NSA_Q + NSA_KV_W
    pad = jnp.zeros((w.shape[0], AB_PACKED - w.shape[1]), BF16)
    return jnp.concatenate([w[:, :g0].astype(BF16), w[:, g0 + NSA_GATE_W:].astype(BF16),
                            w[:, g0:g0 + NSA_GATE_W].astype(BF16), pad], axis=1)


def nsa_hgrn_mixer(h, pos, past_rows, swa_buf, hg_state, lb, w_in, w1, w2, pe, hg_g, w_out):
    b, l, _ = h.shape
    proj = _proj(h, w_in)
    q, kv, hq, hf, hi, hgate, g = split_cols(
        proj, (NSA_Q, NSA_KV_W, HG_QF_W, HG_QF_W, HG_V_W, HG_V_W, NSA_GATE_W))
    kv = kv.reshape(b, l, 6, NSA_KV_HEADS, NSA_HEAD_DIM)
    if past_rows is None:
        o_nsa = nsa_prompt_branch(q, kv, g, w1, w2, pe)
        new_buf = kv[:, l - min(WINDOW, l):, 4:]
        o_hg, s_new = _hgrn_prompt(proj.reshape(b * l, AB_PACKED), AB_HG_COL, lb[None], hg_g[None], hg_state, b, l)
        mix = jnp.concatenate([o_nsa, o_hg.reshape(b, l, HG_V_W)], axis=-1)
        return _proj(mix, w_out), kv[:, :, :4], new_buf, s_new
    else:
        q = q.reshape(b, l, NSA_KV_HEADS, NSA_GROUP, NSA_HEAD_DIM)
        rows = jnp.concatenate([past_rows.astype(kv.dtype), kv[:, :, :4]], axis=1)
        o_cmp, o_slc = nsa_cmp_slc(q, pos, rows, w1, w2, pe)
        o_swa, new_buf = swa_buffered(q, pos, kv[:, :, 4:], swa_buf)
        gates = jax.nn.sigmoid(g.astype(F32)).reshape(b, l, NSA_KV_HEADS, NSA_GROUP, 3)
        o_nsa = gates[..., 0:1] * o_cmp + gates[..., 1:2] * o_slc + gates[..., 2:3] * o_swa
        o_nsa = o_nsa.reshape(b, l, NSA_Q)
    o_hg, s_new = hgrn2(hq, hf, hi, hgate, lb, hg_state, hg_g)
    mix = jnp.concatenate([o_nsa.astype(h.dtype), o_hg.astype(h.dtype)], axis=-1)
    return _proj(mix, w_out), kv[:, :, :4], new_buf, s_new


def ssd_chunked(x, dt, a, bm, cm, s0):
    b, l, nh, p = x.shape
    g, n = bm.shape[2], bm.shape[3]
    r = nh // g
    c = min(SSM_CHUNK, l)
    tril = jnp.tril(jnp.ones((c, c), bool))

    def step(s, inp):
        xc, dtc, bc, cc = inp
        cum = jnp.cumsum(dtc * a, axis=1)
        seg = cum[:, :, None, :] - cum[:, None, :, :]
        lm = jnp.exp(jnp.where(tril[None, :, :, None], seg, -jnp.inf)).reshape(b, c, c, g, r)
        xdt = (xc * dtc[..., None]).reshape(b, c, g, r, p)
        cb = jnp.einsum('btgn,bsgn->btsg', cc, bc)
        sg = s.reshape(b, g, r, p, n)
        y = jnp.einsum('btsg,btsgr,bsgrp->btgrp', cb, lm, xdt)
        y = y + jnp.einsum('btgn,bgrpn->btgrp', cc, sg) * jnp.exp(cum).reshape(b, c, g, r)[..., None]
        dec = jnp.exp(cum[:, -1:] - cum).reshape(b, c, g, r)
        sg = jnp.exp(cum[:, -1]).reshape(b, g, r)[..., None, None] * sg + jnp.einsum('bsgn,bsgrp->bgrpn', bc, xdt * dec[..., None])
        return sg.reshape(b, nh, p, n), y.reshape(b, c, nh, p)

    s, y = lax.scan(step, s0.astype(F32), (to_chunks(x, c), to_chunks(dt, c), to_chunks(bm, c), to_chunks(cm, c)))
    return from_chunks(y, l), s


def mamba_mixer(h, conv_state, ssm_state, w_in, conv_w, conv_b, dt_bias, a_log, d_skip, norm_g, w_out):
    b, l, _ = h.shape
    z, xbc, dt = split_cols(_proj(h, w_in), (SSM_D_INNER, SSM_CONV_DIM, SSM_HEADS))
    xpad = jnp.concatenate([conv_state.astype(xbc.dtype), xbc], axis=1)
    acc = conv_b.astype(F32)
    for j in range(SSM_CONV):
        acc = acc + xpad[:, j:j + l].astype(F32) * conv_w[j].astype(F32)
    xbc = jax.nn.silu(acc)
    new_conv = xpad[:, xpad.shape[1] - (SSM_CONV - 1):]
    xs, bm, cm = split_cols(xbc, (SSM_D_INNER, SSM_GROUPS * SSM_STATE, SSM_GROUPS * SSM_STATE))
    xs = xs.reshape(b, l, SSM_HEADS, SSM_HEAD_DIM)
    bm = bm.reshape(b, l, SSM_GROUPS, SSM_STATE)
    cm = cm.reshape(b, l, SSM_GROUPS, SSM_STATE)
    dt = jax.nn.softplus(dt.astype(F32) + dt_bias.astype(F32))
    a = -jnp.exp(a_log.astype(F32))
    y, s = ssd_chunked(xs, dt, a, bm, cm, ssm_state)
    y = y + d_skip.astype(F32)[:, None] * xs
    y = (y.reshape(b, l, SSM_D_INNER) * jax.nn.silu(z.astype(F32))).reshape(b, l, SSM_GROUPS, SSM_D_INNER // SSM_GROUPS)
    y = (y * lax.rsqrt(jnp.mean(y * y, axis=-1, keepdims=True) + EPS)).reshape(b, l, SSM_D_INNER) * norm_g.astype(F32)
    return _proj(y.astype(h.dtype), w_out), new_conv, s


def swiglu(h, w_in, w_out):
    gt, up = split_cols(_proj(h, w_in), (D_FF, D_FF))
    return _proj(jax.nn.silu(gt) * up, w_out)


def ple_add(x, p, w_up, w_gate, g):
    return x + (_proj(p.astype(x.dtype), w_up) * jax.nn.sigmoid(_proj(rms_norm(x, g), w_gate))).astype(x.dtype)


def kernel(x_prompt, x_sample, cache_nsa_kv, cache_swa_kv, state_hgrn, state_ssm, cache_conv, page_table, p_prompt, p_sample, norm_mix, norm_ffn, w_ab_in, w_cmp1, w_cmp2, cmp_pe, hg_lb_logits, hg_norm, w_ab_out, w_ssm_in, ssm_conv_w, ssm_conv_b, ssm_dt_bias, ssm_a_log, ssm_d, ssm_norm, w_ssm_out, w_ffn_in, w_ffn_out, w_ple_up, w_ple_gate, norm_ple, norm_final):
    depth = norm_mix.shape[0]
    bp, lp, _ = x_prompt.shape
    bs, ls, _ = x_sample.shape
    n_pages = page_table.shape[1]
    pos_p = jnp.arange(lp)
    pos_s = PAST_LEN + jnp.arange(ls)
    lb_all = jnp.cumsum(jax.nn.softmax(hg_lb_logits.astype(F32), axis=0), axis=0)
    xp, xs = x_prompt, x_sample
    nsa_p, nsa_s, swa_p, swa_s, hg_p, hg_s, ssm_p, ssm_s, cv_p, cv_s = [], [], [], [], [], [], [], [], [], []
    for i in range(depth):
        hp = rms_norm(xp, norm_mix[i])
        hs = rms_norm(xs, norm_mix[i])
        if i % 2 == 0:
            a = i // 2
            wa = (lb_all[a], pack_ab_in(w_ab_in[a]), w_cmp1[a], w_cmp2[a], cmp_pe[a], hg_norm[a],
                  w_ab_out[a].astype(BF16))
            yp, r_p, b_p, s_p = nsa_hgrn_mixer(hp, pos_p, None, None, jnp.zeros((bp, HG_HEADS, HG_DK, HG_DV), F32), *wa)
            past = cache_nsa_kv[a][page_table].reshape((bs, n_pages * PAGE_SIZE) + cache_nsa_kv.shape[3:])
            ys, r_s, b_s, s_s = nsa_hgrn_mixer(hs, pos_s, past, cache_swa_kv[a], state_hgrn[a], *wa)
            nsa_p.append(r_p)
            nsa_s.append(r_s)
            swa_p.append(b_p)
            swa_s.append(b_s)
            hg_p.append(s_p)
            hg_s.append(s_s)
        else:
            c = i // 2
            wc = (w_ssm_in[c].astype(BF16), ssm_conv_w[c], ssm_conv_b[c], ssm_dt_bias[c], ssm_a_log[c], ssm_d[c],
                  ssm_norm[c], w_ssm_out[c].astype(BF16))
            yp, c_p, t_p = mamba_mixer(hp, jnp.zeros((bp, SSM_CONV - 1, SSM_CONV_DIM), hp.dtype),
                                       jnp.zeros((bp, SSM_HEADS, SSM_HEAD_DIM, SSM_STATE), F32), *wc)
            ys, c_s, t_s = mamba_mixer(hs, cache_conv[c], state_ssm[c], *wc)
            cv_p.append(c_p)
            cv_s.append(c_s)
            ssm_p.append(t_p)
            ssm_s.append(t_s)
        xp = xp + yp
        xs = xs + ys
        wfi = w_ffn_in[i].astype(BF16)
        wfo = w_ffn_out[i].astype(BF16)
        xp = xp + swiglu(rms_norm(xp, norm_ffn[i]), wfi, wfo)
        xs = xs + swiglu(rms_norm(xs, norm_ffn[i]), wfi, wfo)
        wpu = w_ple_up[i].astype(BF16)
        wpg = w_ple_gate[i].astype(BF16)
        xp = ple_add(xp, p_prompt[i], wpu, wpg, norm_ple[i])
        xs = ple_add(xs, p_sample[i], wpu, wpg, norm_ple[i])
    y_prompt = rms_norm(xp, norm_final)
    y_sample = rms_norm(xs, norm_final)
    return (y_prompt, y_sample, jnp.stack(nsa_p), jnp.stack(nsa_s), jnp.stack(swa_p), jnp.stack(swa_s),
            jnp.stack(hg_p), jnp.stack(hg_s), jnp.stack(ssm_p), jnp.stack(ssm_s), jnp.stack(cv_p), jnp.stack(cv_s))
```

```python
import functools
import math

import jax
import jax.numpy as jnp
import numpy as np
from jax import lax
from jax.experimental import pallas as pl
from jax.experimental.pallas import tpu as pltpu

D_MODEL = 4096
PAST_LEN = 16384
PAGE_SIZE = 128
PLE_DIM = 256
NSA_HEADS = 16
NSA_KV_HEADS = 4
NSA_GROUP = NSA_HEADS // NSA_KV_HEADS
NSA_HEAD_DIM = 128
NSA_Q = NSA_HEADS * NSA_HEAD_DIM
NSA_KV_W = 6 * NSA_KV_HEADS * NSA_HEAD_DIM
NSA_GATE_W = 3 * NSA_HEADS
CMP_BLOCK = 32
CMP_STRIDE = 16
CMP_HIDDEN = 256
SLC_BLOCK = 64
N_SELECT = 16
WINDOW = 512
SLC_QBLOCK = 32
SWA_QBLOCK = 128
FORCE_SCORE = 1e4
HG_HEADS = 16
HG_DK = 128
HG_DV = (D_MODEL // 2) // HG_HEADS
HG_QF_W = HG_HEADS * HG_DK
HG_V_W = HG_HEADS * HG_DV
HG_CHUNK = 32
AB_SIZES = (NSA_Q, NSA_KV_W, NSA_GATE_W, HG_QF_W, HG_QF_W, HG_V_W, HG_V_W)
SSM_D_INNER = 2 * D_MODEL
SSM_HEAD_DIM = 64
SSM_HEADS = SSM_D_INNER // SSM_HEAD_DIM
SSM_GROUPS = 8
SSM_STATE = 128
SSM_CONV = 4
SSM_CONV_DIM = SSM_D_INNER + 2 * SSM_GROUPS * SSM_STATE
SSM_CHUNK = 64
D_FF = ((8 * D_MODEL + 3 * 256 - 1) // (3 * 256)) * 256
EPS = 1e-6
NEG = -1e30

LANE = 128
VMEM_LIMIT_BYTES = 56 * 1024 * 1024
BF16 = jnp.bfloat16
F32 = jnp.float32


MAX_ROW_TILE = 1024
NORM_ROW_TILE = 512


def _rms_kernel(x_ref, g_ref, o_ref):
    x = x_ref[...]
    y = x * lax.rsqrt(jnp.mean(x * x, axis=-1, keepdims=True) + EPS)
    o_ref[...] = (y * g_ref[...]).astype(o_ref.dtype)


def _rms(x, g, out_dtype):
    m, d = x.shape
    tm = min(m, NORM_ROW_TILE)
    assert m % tm == 0
    return pl.pallas_call(
        _rms_kernel,
        grid=(m // tm,),
        in_specs=[pl.BlockSpec((tm, d), lambda i: (i, 0)), pl.BlockSpec((1, d), lambda i: (0, 0))],
        out_specs=pl.BlockSpec((tm, d), lambda i: (i, 0)),
        out_shape=jax.ShapeDtypeStruct((m, d), out_dtype),
        compiler_params=pltpu.CompilerParams(dimension_semantics=("arbitrary",),
                                             vmem_limit_bytes=VMEM_LIMIT_BYTES),
        name="rms",
    )(x, g.astype(F32)[None])


def _mm_kernel(*refs, n_lhs, residual):
    x_refs = refs[:n_lhs]
    w_ref = refs[n_lhs]
    o_ref = refs[-1]
    k = pl.program_id(2)

    def first(x_ref):
        acc = jnp.dot(x_ref[...], w_ref[...], preferred_element_type=F32)
        o_ref[...] = acc + refs[n_lhs + 1][...] if residual else acc

    def later(x_ref):
        o_ref[...] += jnp.dot(x_ref[...], w_ref[...], preferred_element_type=F32)

    pl.when(k == 0)(functools.partial(first, x_refs[0]))
    if n_lhs == 1:
        pl.when(k > 0)(functools.partial(later, x_refs[0]))
    else:
        for p in range(1, n_lhs):
            pl.when(k == p)(functools.partial(later, x_refs[p]))


def _k_tile(k, tm, tn, budget):
    units = k // LANE
    for parts in range(1, units + 1):
        if units % parts == 0 and 2 * (tm + tn) * (units // parts) * LANE * 2 <= budget:
            return (units // parts) * LANE
    raise ValueError(f"no K tile for {(k, tm, tn)}")


def _mm(xs, w, res=None):
    xs = list(xs) if isinstance(xs, (list, tuple)) else [xs]
    m = xs[0].shape[0]
    k, n = w.shape
    tm = min(m, MAX_ROW_TILE)
    tn = min(n, 1024)
    budget = VMEM_LIMIT_BYTES - (4 if res is not None else 2) * tm * tn * 4 - (4 << 20)
    if len(xs) == 1:
        tk = _k_tile(k, tm, tn, budget)
        x_specs = [pl.BlockSpec((tm, tk), lambda i, j, kk: (i, kk))]
    else:
        tk = k // len(xs)
        assert all(x.shape[1] == tk for x in xs) and 2 * (len(xs) * tm + tn) * tk * 2 <= budget
        x_specs = [pl.BlockSpec((tm, tk), lambda i, j, kk: (i, 0)) for _ in xs]
    assert m % tm == 0 and k % tk == 0 and tk % LANE == 0
    tile = pl.BlockSpec((tm, tn), lambda i, j, kk: (i, j))
    return pl.pallas_call(
        functools.partial(_mm_kernel, n_lhs=len(xs), residual=res is not None),
        grid=(m // tm, pl.cdiv(n, tn), k // tk),
        in_specs=x_specs + [pl.BlockSpec((tk, tn), lambda i, j, kk: (kk, j))] + ([tile] if res is not None else []),
        out_specs=tile,
        out_shape=jax.ShapeDtypeStruct((m, n), F32),
        compiler_params=pltpu.CompilerParams(
            dimension_semantics=("arbitrary", "arbitrary", "arbitrary"),
            vmem_limit_bytes=VMEM_LIMIT_BYTES),
        name="mm",
    )(*xs, w, *([res] if res is not None else []))


FFN_TILE = 512
FFN_PAD = -(-D_FF // FFN_TILE) * FFN_TILE


def pack_ffn_in(w):
    d = w.shape[0]
    pad = ((0, 0), (0, FFN_PAD - D_FF))
    gate = jnp.pad(w[:, :D_FF].astype(BF16), pad).reshape(d, FFN_PAD // FFN_TILE, 1, FFN_TILE)
    up = jnp.pad(w[:, D_FF:].astype(BF16), pad).reshape(d, FFN_PAD // FFN_TILE, 1, FFN_TILE)
    return jnp.concatenate([gate, up], axis=2).reshape(d, 2 * FFN_PAD)


def pack_ffn_out(w):
    return jnp.pad(w.astype(BF16), ((0, FFN_PAD - D_FF), (0, 0)))


def _swiglu_kernel(x_ref, w_ref, o_ref):
    acc = jnp.dot(x_ref[...], w_ref[...], preferred_element_type=F32)
    gate = acc[:, :FFN_TILE]
    o_ref[...] = (gate * jax.nn.sigmoid(gate) * acc[:, FFN_TILE:]).astype(o_ref.dtype)


def _mm_swiglu(x, w_packed):
    m, k = x.shape
    tm = min(m, MAX_ROW_TILE)
    assert m % tm == 0 and w_packed.shape == (k, 2 * FFN_PAD)
    return pl.pallas_call(
        _swiglu_kernel,
        grid=(m // tm, FFN_PAD // FFN_TILE),
        in_specs=[pl.BlockSpec((tm, k), lambda i, j: (i, 0)), pl.BlockSpec((k, 2 * FFN_TILE), lambda i, j: (0, j))],
        out_specs=pl.BlockSpec((tm, FFN_TILE), lambda i, j: (i, j)),
        out_shape=jax.ShapeDtypeStruct((m, FFN_PAD), BF16),
        compiler_params=pltpu.CompilerParams(dimension_semantics=("arbitrary", "arbitrary"),
                                             vmem_limit_bytes=VMEM_LIMIT_BYTES),
        name="mm_swiglu",
    )(x, w_packed)


PLE_COL_TILE = 512


def _ple_kernel(t_ref, wg_ref, p_ref, wu_ref, res_ref, o_ref):
    gate = jnp.dot(t_ref[...], wg_ref[...], preferred_element_type=F32)
    up = jnp.dot(p_ref[...], wu_ref[...], preferred_element_type=F32)
    o_ref[...] = res_ref[...] + up * jax.nn.sigmoid(gate)


def _mm_ple(t, w_gate, p, w_up, res):
    m, k = t.shape
    n = w_gate.shape[1]
    kp = p.shape[1]
    tm = min(m, MAX_ROW_TILE)
    tn = PLE_COL_TILE
    assert m % tm == 0 and n % tn == 0
    tile = pl.BlockSpec((tm, tn), lambda i, j: (i, j))
    return pl.pallas_call(
        _ple_kernel,
        grid=(m // tm, n // tn),
        in_specs=[pl.BlockSpec((tm, k), lambda i, j: (i, 0)), pl.BlockSpec((k, tn), lambda i, j: (0, j)),
                  pl.BlockSpec((tm, kp), lambda i, j: (i, 0)), pl.BlockSpec((kp, tn), lambda i, j: (0, j)), tile],
        out_specs=tile,
        out_shape=jax.ShapeDtypeStruct((m, n), F32),
        compiler_params=pltpu.CompilerParams(dimension_semantics=("arbitrary", "arbitrary"),
                                             vmem_limit_bytes=VMEM_LIMIT_BYTES),
        name="mm_ple",
    )(t, w_gate, p, w_up, res)


NSA_TQ = 128


def _dot_nt(a, b):
    return lax.dot_general(a, b, (((1,), (1,)), ((), ())), preferred_element_type=F32)


def _masked_attend(q, k, v, mask, scale):
    s = jnp.where(mask, _dot_nt(q, k) * scale, NEG)
    e = jnp.exp(s - jnp.max(s, axis=-1, keepdims=True))
    den = jnp.sum(e, axis=-1, keepdims=True)
    return jnp.dot(e.astype(BF16), v, preferred_element_type=F32) / den


def _nsa_prompt_kernel(q_ref, kc_ref, vc_ref, ks_ref, vs_ref, kw_ref, vw_ref, gate_ref, cover_ref, expand_ref,
                       o_ref, *, seq):
    tq = NSA_TQ
    n_cmp = seq // CMP_STRIDE - CMP_BLOCK // CMP_STRIDE + 1
    n_slc = seq // SLC_BLOCK
    span = WINDOW + tq
    scale = NSA_HEAD_DIM ** -0.5
    q0 = pl.program_id(2) * tq
    pos = q0 + lax.broadcasted_iota(jnp.int32, (tq, 1), 0)
    lane = lax.broadcasted_iota(jnp.int32, (tq, LANE), 1)

    cmp_ok = (lane * CMP_STRIDE + (CMP_BLOCK - 1) <= pos) & (lane < n_cmp)
    kc = kc_ref[...]
    vc = vc_ref[...]
    psum = jnp.zeros((tq, LANE), F32)
    o_cmp = []
    for r in range(NSA_GROUP):
        qr = q_ref[:, r * NSA_HEAD_DIM:(r + 1) * NSA_HEAD_DIM].astype(BF16)
        s = jnp.where(cmp_ok, _dot_nt(qr, kc) * scale, NEG)
        e = jnp.where(cmp_ok, jnp.exp(s - jnp.max(s, axis=-1, keepdims=True)), 0.0)
        den = jnp.sum(e, axis=-1, keepdims=True)
        p = e / jnp.where(den > 0.0, den, 1.0)
        psum = psum + p
        o_cmp.append(jnp.dot(p.astype(BF16), vc, preferred_element_type=F32))

    score = jnp.dot(psum, cover_ref[...], preferred_element_type=F32, precision=lax.Precision.HIGHEST)
    cur = pos // SLC_BLOCK
    forced = (lane == 0) | (lane == cur) | (lane == cur - 1)
    causal = lane * SLC_BLOCK <= pos
    score = jnp.where(forced, FORCE_SCORE, jnp.where(causal, score, -1.0))
    score = jnp.where(lane < n_slc, score, -2.0)
    rank = jnp.zeros((tq, LANE), jnp.int32)
    for i in range(n_slc):
        ci = score[:, i:i + 1]
        beats = (ci > score) | ((ci == score) & (lane > i))
        rank = rank + beats.astype(jnp.int32)
    sel = ((rank < min(N_SELECT, n_slc)) & (lane < n_slc)).astype(BF16)
    sel_keys = jnp.dot(sel, expand_ref[...], preferred_element_type=F32)
    kpos = lax.broadcasted_iota(jnp.int32, (tq, seq), 1)
    m_slc = (sel_keys > 0.5) & (kpos <= pos)

    w0 = pl.multiple_of(jnp.maximum(q0 - WINDOW, 0), tq)
    kw = kw_ref[pl.ds(w0, span), :].astype(BF16)
    vw = vw_ref[pl.ds(w0, span), :].astype(BF16)
    wpos = w0 + lax.broadcasted_iota(jnp.int32, (tq, span), 1)
    m_swa = (wpos <= pos) & (pos - wpos < WINDOW)

    ks = ks_ref[...].astype(BF16)
    vs = vs_ref[...].astype(BF16)
    for r in range(NSA_GROUP):
        qr = q_ref[:, r * NSA_HEAD_DIM:(r + 1) * NSA_HEAD_DIM].astype(BF16)
        o_slc = _masked_attend(qr, ks, vs, m_slc, scale)
        o_swa = _masked_attend(qr, kw, vw, m_swa, scale)
        g = gate_ref[:, 3 * r:3 * r + 3]
        o = g[:, 0:1] * o_cmp[r] + g[:, 1:2] * o_slc + g[:, 2:3] * o_swa
        o_ref[:, r * NSA_HEAD_DIM:(r + 1) * NSA_HEAD_DIM] = o.astype(o_ref.dtype)


def _nsa_prompt(proj, kc, vc, gates, batch, seq):
    assert seq % NSA_TQ == 0 and seq // SLC_BLOCK <= LANE and seq // CMP_STRIDE <= LANE + 1
    nt = seq // NSA_TQ
    n_cmp = seq // CMP_STRIDE - CMP_BLOCK // CMP_STRIDE + 1
    n_slc = seq // SLC_BLOCK
    c0 = np.arange(LANE)[:, None] * CMP_STRIDE
    s0 = np.arange(LANE)[None, :] * SLC_BLOCK
    cover = np.clip(np.minimum(c0 + CMP_BLOCK, s0 + SLC_BLOCK) - np.maximum(c0, s0), 0, None) / CMP_BLOCK
    cover = cover * (np.arange(LANE)[:, None] < n_cmp) * (np.arange(LANE)[None, :] < n_slc)
    expand = (np.arange(seq)[None, :] // SLC_BLOCK == np.arange(LANE)[:, None])
    hd = NSA_HEAD_DIM
    gw = NSA_GROUP * hd

    def kv_spec(slot):
        return pl.BlockSpec((seq, hd), lambda b, g, t: (b, NSA_Q // hd + slot * NSA_KV_HEADS + g))

    cmp_spec = pl.BlockSpec((None, None, LANE, hd), lambda b, g, t: (b, g, 0, 0))
    return pl.pallas_call(
        functools.partial(_nsa_prompt_kernel, seq=seq),
        grid=(batch, NSA_KV_HEADS, nt),
        in_specs=[pl.BlockSpec((NSA_TQ, gw), lambda b, g, t: (b * nt + t, g)),
                  cmp_spec, cmp_spec, kv_spec(2), kv_spec(3), kv_spec(4), kv_spec(5),
                  pl.BlockSpec((NSA_TQ, LANE), lambda b, g, t: (b * nt + t, g)),
                  pl.BlockSpec((LANE, LANE), lambda b, g, t: (0, 0)),
                  pl.BlockSpec((LANE, seq), lambda b, g, t: (0, 0))],
        out_specs=pl.BlockSpec((NSA_TQ, gw), lambda b, g, t: (b * nt + t, g)),
        out_shape=jax.ShapeDtypeStruct((batch * seq, NSA_Q), BF16),
        compiler_params=pltpu.CompilerParams(
            dimension_semantics=("arbitrary", "arbitrary", "arbitrary"),
            vmem_limit_bytes=VMEM_LIMIT_BYTES),
        name="nsa_prompt",
    )(proj, kc, vc, proj, proj, proj, proj, gates, jnp.asarray(cover, F32), jnp.asarray(expand, BF16))


HG_SLAB = 256


def _hgrn_kernel(hq_ref, hf_ref, hi_ref, hg_ref, lb_ref, gn_ref, s0_ref, tri_ref, o_ref, s_ref,
                 qg_s, kg_s, qs_s, kd_s, v_s, el_s, o_s, *, seq):
    c = HG_CHUNK
    hi_prec = lax.Precision.HIGHEST
    lb = lb_ref[...]

    def prep(i, carry):
        rows = pl.ds(pl.multiple_of(i * HG_SLAB, HG_SLAB), HG_SLAB)
        hq = hq_ref[rows, :]
        q = hq * jax.nn.sigmoid(hq)
        f = lb + (1.0 - lb) * jax.nn.sigmoid(hf_ref[rows, :])
        k = 1.0 - f
        bc = jnp.dot(tri_ref[0], jnp.log(f), preferred_element_type=F32, precision=hi_prec)
        bm = jnp.dot(tri_ref[1], bc, preferred_element_type=F32, precision=hi_prec)
        bl = jnp.dot(tri_ref[2], bc, preferred_element_type=F32, precision=hi_prec)
        qg_s[rows, :] = (q * jnp.exp(bc - bm)).astype(qg_s.dtype)
        kg_s[rows, :] = (k * jnp.exp(bm - bc)).astype(kg_s.dtype)
        qs_s[rows, :] = (q * jnp.exp(bc)).astype(qs_s.dtype)
        kd_s[rows, :] = (k * jnp.exp(bl - bc)).astype(kd_s.dtype)
        v_s[rows, :] = hi_ref[rows, :].astype(v_s.dtype)
        el_s[rows, :] = jnp.exp(bl)
        return carry

    lax.fori_loop(0, seq // HG_SLAB, prep, 0)

    tril = lax.broadcasted_iota(jnp.int32, (c, c), 0) >= lax.broadcasted_iota(jnp.int32, (c, c), 1)

    def chunk(ci, st):
        r0 = pl.multiple_of(ci * c, c)
        rows = pl.ds(r0, c)
        kd = kd_s[rows, :]
        v = v_s[rows, :]
        a = jnp.where(tril, _dot_nt(qg_s[rows, :], kg_s[rows, :]), 0.0)
        o = jnp.dot(a.astype(v.dtype), v, preferred_element_type=F32) + _dot_nt(qs_s[rows, :], st.astype(v.dtype))
        o_s[rows, :] = o
        upd = lax.dot_general(v, kd, (((0,), (0,)), ((), ())), preferred_element_type=F32)
        return st * el_s[pl.ds(r0, 1), :] + upd

    st = lax.fori_loop(0, seq // c, chunk, s0_ref[...].T, unroll=2)
    s_ref[...] = st.T

    gn = gn_ref[...]

    def finish(i, carry):
        rows = pl.ds(pl.multiple_of(i * HG_SLAB, HG_SLAB), HG_SLAB)
        o = o_s[rows, :]
        hg = hg_ref[rows, :]
        o = o * lax.rsqrt(jnp.mean(o * o, axis=-1, keepdims=True) + EPS)
        o_ref[rows, :] = (o * gn * (hg * jax.nn.sigmoid(hg))).astype(o_ref.dtype)
        return carry

    lax.fori_loop(0, seq // HG_SLAB, finish, 0)


def _hgrn_tri():
    i = np.arange(HG_SLAB)[:, None]
    j = np.arange(HG_SLAB)[None, :]
    same = (i // HG_CHUNK) == (j // HG_CHUNK)
    cum = same & (j <= i)
    mid = j == (i // HG_CHUNK) * HG_CHUNK + (HG_CHUNK - 1) // 2
    last = j == (i // HG_CHUNK) * HG_CHUNK + HG_CHUNK - 1
    return jnp.asarray(np.stack([cum, mid, last]), F32)


def _hgrn_prompt(proj, col0, lb, g_norm, s0, batch, seq):
    assert seq % HG_SLAB == 0 and col0 % LANE == 0 and HG_DK == LANE and HG_DV == LANE
    c0 = col0 // LANE

    def col_spec(group):
        return pl.BlockSpec((seq, LANE), lambda b, h: (b, c0 + group * HG_HEADS + h))

    vec_spec = pl.BlockSpec((1, LANE), lambda b, h: (0, h))
    st_spec = pl.BlockSpec((None, None, HG_DK, HG_DV), lambda b, h: (b, h, 0, 0))
    return pl.pallas_call(
        functools.partial(_hgrn_kernel, seq=seq),
        grid=(batch, HG_HEADS),
        in_specs=[col_spec(0), col_spec(1), col_spec(2), col_spec(3), vec_spec, vec_spec, st_spec,
                  pl.BlockSpec((3, HG_SLAB, HG_SLAB), lambda b, h: (0, 0, 0))],
        out_specs=[pl.BlockSpec((seq, LANE), lambda b, h: (b, h)), st_spec],
        out_shape=[jax.ShapeDtypeStruct((batch * seq, HG_V_W), BF16),
                   jax.ShapeDtypeStruct((batch, HG_HEADS, HG_DK, HG_DV), F32)],
        scratch_shapes=[pltpu.VMEM((seq, LANE), BF16)] * 5 + [pltpu.VMEM((seq, LANE), F32)] * 2,
        compiler_params=pltpu.CompilerParams(
            dimension_semantics=("arbitrary", "arbitrary"),
            vmem_limit_bytes=VMEM_LIMIT_BYTES),
        name="hgrn_prompt",
    )(proj, proj, proj, proj, lb, g_norm, s0, _hgrn_tri())


SSD_TL = 256
SSD_GW = SSM_D_INNER // SSM_GROUPS
SSD_HPG = SSM_HEADS // SSM_GROUPS


def _split3(x):
    hi = x.astype(BF16)
    r1 = x - hi.astype(F32)
    mid = r1.astype(BF16)
    lo = (r1 - mid.astype(F32)).astype(BF16)
    return hi, mid, lo


def _sel_dot(sel, x):
    hi, mid, lo = _split3(x)
    d = functools.partial(jnp.dot, preferred_element_type=F32)
    return d(sel, hi) + d(sel, mid) + d(sel, lo)


def _dot_sel(x, sel):
    hi, mid, lo = _split3(x)
    d = functools.partial(jnp.dot, preferred_element_type=F32)
    return d(hi, sel) + d(mid, sel) + d(lo, sel)


def _causal_conv_silu(x, prev8, w, bias):
    row8 = lax.broadcasted_iota(jnp.int32, prev8.shape, 0)
    acc = bias
    for k in range(SSM_CONV - 1, 0, -1):
        r = pltpu.roll(x, k, 0)
        top = jnp.where(row8 < k, pltpu.roll(prev8, k, 0), r[:8])
        acc = acc + jnp.concatenate([top, r[8:]], axis=0) * w[SSM_CONV - 1 - k:SSM_CONV - k]
    acc = acc + x * w[SSM_CONV - 1:SSM_CONV]
    return acc * jax.nn.sigmoid(acc)


def _ssd_kernel(z_ref, x_ref, b_ref, c_ref, dt_ref, cx0_ref, cb0_ref, cc0_ref, s0_ref,
                wx_ref, wb_ref, wc_ref, bx_ref, bb_ref, bc_ref, dtb_ref, alog_ref, d_ref, gn_ref,
                tri_ref, hsel_ref, o_ref, s_ref,
                st_s, tx_s, tb_s, tc_s, xdt_s, xdd_s, ce_s, le_s, bm_s, cm_s, y_s):
    lt = pl.program_id(2)
    c = SSM_CHUNK
    tl = SSD_TL
    hp = SSM_HEAD_DIM

    @pl.when(lt == 0)
    def _():
        st_s[...] = s0_ref[...].T
        for tail, c0 in ((tx_s, cx0_ref), (tb_s, cb0_ref), (tc_s, cc0_ref)):
            tail[...] = jnp.zeros(tail.shape, F32)
            tail[8 - (SSM_CONV - 1):8, :] = c0[...]

    x_raw = x_ref[...]
    b_raw = b_ref[...]
    c_raw = c_ref[...]
    xs = _causal_conv_silu(x_raw, tx_s[...], wx_ref[...], bx_ref[...])
    bm_s[...] = _causal_conv_silu(b_raw, tb_s[...], wb_ref[...], bb_ref[...]).astype(bm_s.dtype)
    cm_s[...] = _causal_conv_silu(c_raw, tc_s[...], wc_ref[...], bc_ref[...]).astype(cm_s.dtype)
    tx_s[...] = x_raw[tl - 8:]
    tb_s[...] = b_raw[tl - 8:]
    tc_s[...] = c_raw[tl - 8:]

    dt = jax.nn.softplus(dt_ref[...] + dtb_ref[...])
    cum = _sel_dot(tri_ref[0], dt * -jnp.exp(alog_ref[...]))
    hsel = hsel_ref[...]
    dt_e = _dot_sel(dt, hsel)
    cum_e = _dot_sel(cum, hsel)
    last_e = _sel_dot(tri_ref[1], cum_e)
    xdt = xs * dt_e
    xdt_s[...] = xdt.astype(xdt_s.dtype)
    xdd_s[...] = (xdt * jnp.exp(last_e - cum_e)).astype(xdd_s.dtype)
    ce_s[...] = cum_e
    le_s[...] = jnp.exp(last_e)

    trow = lax.broadcasted_iota(jnp.int32, (c, SSD_GW), 0)
    lane_g = lax.broadcasted_iota(jnp.int32, (c, SSD_GW), 1)
    diag = (lane_g % hp) == trow
    t2 = lax.broadcasted_iota(jnp.int32, (c, LANE), 0)
    l2 = lax.broadcasted_iota(jnp.int32, (c, LANE), 1)
    tril2 = (l2 % hp) <= t2
    r3 = lax.broadcasted_iota(jnp.int32, (2 * c, LANE), 0)
    l3 = lax.broadcasted_iota(jnp.int32, (2 * c, LANE), 1)
    blockdiag = (r3 // c) == (l3 // hp)

    for ci in range(tl // c):
        rows = pl.ds(ci * c, c)
        cm = cm_s[rows, :]
        bm = bm_s[rows, :]
        cum_c = ce_s[rows, :]
        cum_row = jnp.sum(jnp.where(diag, cum_c, 0.0), axis=0, keepdims=True)
        cb2 = _dot_nt(cm, jnp.concatenate([bm, bm], axis=0))
        xdt_c = xdt_s[rows, :]
        st = st_s[...]
        y_inter = jnp.dot(cm, st.astype(cm.dtype), preferred_element_type=F32) * jnp.exp(cum_c)
        for j in range(SSD_GW // LANE):
            cols = slice(j * LANE, (j + 1) * LANE)
            seg = cum_c[:, cols] - cum_row[:, cols]
            w = (cb2 * jnp.where(tril2, jnp.exp(seg), 0.0)).astype(xdt_c.dtype)
            xj = xdt_c[:, cols]
            xbd = jnp.where(blockdiag, jnp.concatenate([xj, xj], axis=0), jnp.zeros_like(xj[:1, :1]))
            y_s[rows, cols] = y_inter[:, cols] + jnp.dot(w, xbd, preferred_element_type=F32)
        upd = lax.dot_general(bm, xdd_s[rows, :], (((0,), (0,)), ((), ())), preferred_element_type=F32)
        st_s[...] = st * le_s[pl.ds(ci * c, 1), :] + upd

    z = z_ref[...]
    y = (y_s[...] + d_ref[...] * xs) * (z * jax.nn.sigmoid(z))
    y = y * lax.rsqrt(jnp.mean(y * y, axis=-1, keepdims=True) + EPS)
    o_ref[...] = (y * gn_ref[...]).astype(o_ref.dtype)

    @pl.when(lt == pl.num_programs(2) - 1)
    def _():
        s_ref[...] = st_s[...].T


def _ssd_consts():
    i = np.arange(SSD_TL)[:, None]
    j = np.arange(SSD_TL)[None, :]
    cum = ((i // SSM_CHUNK) == (j // SSM_CHUNK)) & (j <= i)
    last = j == (i // SSM_CHUNK) * SSM_CHUNK + SSM_CHUNK - 1
    tri = jnp.asarray(np.stack([cum, last]), BF16)
    h = np.arange(SSM_HEADS)[None, :, None]
    g = np.arange(SSM_GROUPS)[:, None, None]
    lane = np.arange(SSD_GW)[None, None, :]
    hsel = jnp.asarray(h == g * SSD_HPG + lane // SSM_HEAD_DIM, BF16)
    return tri, hsel


def _ssd_prompt(proj, conv0, s0, conv_w, conv_b, dt_bias, a_log, d_skip, norm_g, batch, seq):
    assert seq % SSD_TL == 0 and SSM_STATE == LANE and SSM_HEADS == LANE and SSM_CHUNK == SSM_HEAD_DIM
    nl = seq // SSD_TL
    gpl = SSD_GW // LANE
    xb = SSM_D_INNER // SSD_GW
    bb = 2 * SSM_D_INNER // LANE
    cb = bb + SSM_GROUPS
    db = cb + SSM_GROUPS
    tri, hsel = _ssd_consts()
    d_e = jnp.repeat(d_skip.astype(F32), SSM_HEAD_DIM)[None]
    row = lambda a: a.astype(F32)[None]

    def rows_spec(width, col_fn):
        return pl.BlockSpec((SSD_TL, width), lambda b, g, t: (b * nl + t, col_fn(g)))

    def conv0_spec(width, col_fn):
        return pl.BlockSpec((None, SSM_CONV - 1, width), lambda b, g, t: (b, 0, col_fn(g)))

    def vec_spec(rows_, width, col_fn):
        return pl.BlockSpec((rows_, width), lambda b, g, t: (0, col_fn(g)))

    xcol = lambda g: g
    bcol = lambda g: SSM_D_INNER // LANE + g
    ccol = lambda g: SSM_D_INNER // LANE + SSM_GROUPS + g
    zero = lambda g: 0
    st_spec = pl.BlockSpec((None, SSD_GW, SSM_STATE), lambda b, g, t: (b, g, 0))
    return pl.pallas_call(
        _ssd_kernel,
        grid=(batch, SSM_GROUPS, nl),
        in_specs=[rows_spec(SSD_GW, xcol), rows_spec(SSD_GW, lambda g: xb + g),
                  rows_spec(LANE, lambda g: bb + g), rows_spec(LANE, lambda g: cb + g),
                  rows_spec(LANE, lambda g: db),
                  conv0_spec(SSD_GW, xcol), conv0_spec(LANE, bcol), conv0_spec(LANE, ccol), st_spec,
                  vec_spec(SSM_CONV, SSD_GW, xcol), vec_spec(SSM_CONV, LANE, bcol), vec_spec(SSM_CONV, LANE, ccol),
                  vec_spec(1, SSD_GW, xcol), vec_spec(1, LANE, bcol), vec_spec(1, LANE, ccol),
                  vec_spec(1, LANE, zero), vec_spec(1, LANE, zero),
                  vec_spec(1, SSD_GW, xcol), vec_spec(1, SSD_GW, xcol),
                  pl.BlockSpec((2, SSD_TL, SSD_TL), lambda b, g, t: (0, 0, 0)),
                  pl.BlockSpec((None, SSM_HEADS, SSD_GW), lambda b, g, t: (g, 0, 0))],
        out_specs=[rows_spec(SSD_GW, xcol), st_spec],
        out_shape=[jax.ShapeDtypeStruct((batch * seq, SSM_D_INNER), BF16),
                   jax.ShapeDtypeStruct((batch, SSM_HEADS * SSM_HEAD_DIM, SSM_STATE), F32)],
        scratch_shapes=[pltpu.VMEM((SSM_STATE, SSD_GW), F32),
                        pltpu.VMEM((8, SSD_GW), F32), pltpu.VMEM((8, LANE), F32), pltpu.VMEM((8, LANE), F32),
                        pltpu.VMEM((SSD_TL, SSD_GW), BF16), pltpu.VMEM((SSD_TL, SSD_GW), BF16),
                        pltpu.VMEM((SSD_TL, SSD_GW), F32), pltpu.VMEM((SSD_TL, SSD_GW), F32),
                        pltpu.VMEM((SSD_TL, LANE), BF16), pltpu.VMEM((SSD_TL, LANE), BF16),
                        pltpu.VMEM((SSD_TL, SSD_GW), F32)],
        compiler_params=pltpu.CompilerParams(
            dimension_semantics=("arbitrary", "arbitrary", "arbitrary"),
            vmem_limit_bytes=VMEM_LIMIT_BYTES),
        name="ssd_prompt",
    )(proj, proj, proj, proj, proj, conv0, conv0, conv0, s0,
      conv_w, conv_w, conv_w, row(conv_b), row(conv_b), row(conv_b), row(dt_bias), row(a_log),
      d_e, row(norm_g), tri, hsel)


def split_cols(a, sizes):
    out, o = [], 0
    for s in sizes:
        out.append(a[..., o:o + s])
        o += s
    return out


def to_chunks(a, c):
    b, l = a.shape[:2]
    pad = (-l) % c
    a = jnp.pad(a.astype(F32), [(0, 0), (0, pad)] + [(0, 0)] * (a.ndim - 2))
    return jnp.moveaxis(a.reshape((b, (l + pad) // c, c) + a.shape[2:]), 1, 0)


def from_chunks(a, l):
    a = jnp.moveaxis(a, 0, 1)
    return a.reshape((a.shape[0], a.shape[1] * a.shape[2]) + a.shape[3:])[:, :l]


def gqa_attend(q, k, v, mask):
    s = jnp.einsum('...qgrd,...kgd->...qgrk', q, k, preferred_element_type=F32) * (q.shape[-1] ** -0.5)
    m = mask[..., :, None, None, :]
    p = jnp.where(m, jax.nn.softmax(jnp.where(m, s, NEG), axis=-1), 0.0)
    o = jnp.einsum('...qgrk,...kgd->...qgrd', p.astype(v.dtype), v)
    return o, p


def compress_rows(rows, w1, w2, pe):
    b, t = rows.shape[:2]
    n_sub = t // CMP_STRIDE
    n_per = CMP_BLOCK // CMP_STRIDE
    n_cmp = n_sub - n_per + 1
    sub = rows[:, :n_sub * CMP_STRIDE].reshape(b, n_sub, CMP_STRIDE, NSA_KV_HEADS, NSA_HEAD_DIM)
    sub = jnp.moveaxis(sub, 3, 2).reshape(b, n_sub, NSA_KV_HEADS, CMP_STRIDE * NSA_HEAD_DIM)
    w1r = w1.reshape(n_per, CMP_STRIDE * NSA_HEAD_DIM, CMP_HIDDEN)
    hid = pe.reshape(-1) @ w1
    for m in range(n_per):
        hid = hid + jnp.einsum('bngc,ch->bngh', sub[:, m:m + n_cmp], w1r[m])
    return jax.nn.silu(hid) @ w2


def slc_attend(q, pos, kb, vb, idx, valid):
    b, lq, g, r, d = q.shape
    k = idx.shape[-1]
    ii = jnp.moveaxis(idx, 2, 1).reshape(b, g, lq * k)
    bi = jnp.arange(b)[:, None, None]
    gi = jnp.arange(g)[None, :, None]
    kg = kb[bi, gi, ii].reshape(b, g, lq, k, SLC_BLOCK, d)
    vg = vb[bi, gi, ii].reshape(b, g, lq, k, SLC_BLOCK, d)
    s = jnp.einsum('bqgrd,bgqksd->bqgrks', q, kg, preferred_element_type=F32) * (d ** -0.5)
    kpos = idx[..., None] * SLC_BLOCK + jnp.arange(SLC_BLOCK)
    mask = ((kpos <= pos[None, :, None, None, None]) & valid[..., None])[:, :, :, None]
    s = jnp.where(mask, s, NEG).reshape(b, lq, g, r, k * SLC_BLOCK)
    p = jax.nn.softmax(s, axis=-1).reshape(b, lq, g, r, k, SLC_BLOCK)
    return jnp.einsum('bqgrks,bgqksd->bqgrd', p.astype(vg.dtype), vg)


def nsa_cmp_slc(q, pos, rows, w1, w2, pe):
    b, l, g, r, d = q.shape
    t = rows.shape[1]
    kc = compress_rows(rows[:, :, 0], w1[0], w2[0], pe[0])
    vc = compress_rows(rows[:, :, 1], w1[1], w2[1], pe[1])
    n_cmp = kc.shape[1]
    cmp_last = jnp.arange(n_cmp) * CMP_STRIDE + (CMP_BLOCK - 1)
    o_cmp, p_cmp = gqa_attend(q, kc, vc, cmp_last[None, :] <= pos[:, None])
    n_slc = -(-t // SLC_BLOCK)
    c0 = np.arange(n_cmp)[:, None] * CMP_STRIDE
    s0 = np.arange(n_slc)[None, :] * SLC_BLOCK
    cover = np.clip(np.minimum(c0 + CMP_BLOCK, s0 + SLC_BLOCK) - np.maximum(c0, s0), 0, None) / CMP_BLOCK
    score = jnp.einsum('blgrn,nj->blgj', p_cmp, jnp.asarray(cover, F32))
    blk = jnp.arange(n_slc)[None, :]
    cur = (pos // SLC_BLOCK)[:, None]
    causal = blk * SLC_BLOCK <= pos[:, None]
    forced = (blk == 0) | (blk == cur) | (blk == cur - 1)
    score = jnp.where(forced[None, :, None, :], FORCE_SCORE, jnp.where(causal[None, :, None, :], score, -1.0))
    vals, idx = lax.top_k(score, min(N_SELECT, n_slc))
    valid = vals >= 0.0
    kv = jnp.pad(rows[:, :, 2:4], ((0, 0), (0, n_slc * SLC_BLOCK - t), (0, 0), (0, 0), (0, 0)))
    kv = jnp.transpose(kv.reshape(b, n_slc, SLC_BLOCK, 2, g, d), (3, 0, 4, 1, 2, 5))
    kb, vb = kv[0], kv[1]
    qb = math.gcd(l, SLC_QBLOCK)
    nb = l // qb

    def blocks(a):
        return jnp.moveaxis(a.reshape((b, nb, qb) + a.shape[2:]), 1, 0)

    o_slc = lax.map(lambda a: slc_attend(a[0], a[1], kb, vb, a[2], a[3]),
                    (blocks(q), pos.reshape(nb, qb), blocks(idx), blocks(valid)))
    o_slc = jnp.moveaxis(o_slc, 0, 1).reshape(b, l, g, r, d)
    return o_cmp, o_slc


def swa_buffered(q, pos, kv_new, buf):
    wb = buf.shape[1]
    keys = jnp.concatenate([buf.astype(kv_new.dtype), kv_new], axis=1)
    kpos = PAST_LEN - wb + jnp.arange(keys.shape[1])
    mask = (kpos[None, :] <= pos[:, None]) & (pos[:, None] - kpos[None, :] < WINDOW)
    o, _ = gqa_attend(q, keys[:, :, 0], keys[:, :, 1], mask)
    return o, keys[:, keys.shape[1] - wb:]


def gla_chunked(q, k, v, logf, s0):
    b, l = q.shape[:2]
    c = min(HG_CHUNK, l)
    mid = (c - 1) // 2
    tril = jnp.tril(jnp.ones((c, c), bool))

    def step(s, inp):
        qc, kc, vc, gc = inp
        bc = jnp.cumsum(gc, axis=1)
        bm = bc[:, mid:mid + 1]
        a = jnp.einsum('bthd,bshd->bhts', qc * jnp.exp(bc - bm), kc * jnp.exp(bm - bc))
        a = jnp.where(tril, a, 0.0)
        o = jnp.einsum('bhts,bshv->bthv', a, vc) + jnp.einsum('bthd,bhdv->bthv', qc * jnp.exp(bc), s)
        bl = bc[:, -1]
        s = jnp.exp(bl)[..., None] * s + jnp.einsum('bshd,bshv->bhdv', kc * jnp.exp(bl[:, None] - bc), vc)
        return s, o

    s, o = lax.scan(step, s0.astype(F32), (to_chunks(q, c), to_chunks(k, c), to_chunks(v, c), to_chunks(logf, c)))
    return from_chunks(o, l), s


def hgrn2(hq, hf, hi, hg, lb, s0, g_norm):
    b, l, _ = hq.shape
    q = jax.nn.silu(hq.astype(F32)).reshape(b, l, HG_HEADS, HG_DK)
    f = (lb + (1.0 - lb) * jax.nn.sigmoid(hf.astype(F32))).reshape(b, l, HG_HEADS, HG_DK)
    v = hi.astype(F32).reshape(b, l, HG_HEADS, HG_DV)
    o, s = gla_chunked(q, 1.0 - f, v, jnp.log(f), s0)
    o = o * lax.rsqrt(jnp.mean(o * o, axis=-1, keepdims=True) + EPS)
    o = o.reshape(b, l, HG_V_W) * g_norm.astype(F32) * jax.nn.silu(hg.astype(F32))
    return o, s


def nsa_prompt_branch(proj, b, l, w1, w2, pe):
    slot_w = NSA_KV_HEADS * NSA_HEAD_DIM

    def cmp_pad(slot):
        rows = proj[:, NSA_Q + slot * slot_w:NSA_Q + (slot + 1) * slot_w].reshape(b, l, NSA_KV_HEADS, NSA_HEAD_DIM)
        c = compress_rows(rows, w1[slot], w2[slot], pe[slot])
        c = jnp.pad(c, ((0, 0), (0, LANE - c.shape[1]), (0, 0), (0, 0)))
        return jnp.moveaxis(c, 2, 1).astype(BF16)

    g = proj[:, AB_GATE_COL:AB_GATE_COL + NSA_GATE_W]
    gates = jax.nn.sigmoid(g).reshape(b * l, NSA_KV_HEADS, 3 * NSA_GROUP)
    gates = jnp.pad(gates, ((0, 0), (0, 0), (0, LANE - 3 * NSA_GROUP))).reshape(b * l, NSA_KV_HEADS * LANE)
    return _nsa_prompt(proj, cmp_pad(0), cmp_pad(1), gates, b, l)


AB_HG_COL = NSA_Q + NSA_KV_W
AB_GATE_COL = AB_HG_COL + 2 * HG_QF_W + 2 * HG_V_W
AB_PACKED = -(-(AB_GATE_COL + NSA_GATE_W) // LANE) * LANE


def pack_ab_in(w):
    g0 = NSA_Q + NSA_KV_W
    pad = jnp.zeros((w.shape[0], AB_PACKED - w.shape[1]), BF16)
    return jnp.concatenate([w[:, :g0].astype(BF16), w[:, g0 + NSA_GATE_W:].astype(BF16),
                            w[:, g0:g0 + NSA_GATE_W].astype(BF16), pad], axis=1)


def ab_mixer_prompt(x, h, b, l, hg_state, lb, w_in, w1, w2, pe, hg_g, w_out):
    proj = _mm(h, w_in)
    o_nsa = nsa_prompt_branch(proj, b, l, w1, w2, pe)
    o_hg, s_new = _hgrn_prompt(proj, AB_HG_COL, lb[None], hg_g[None], hg_state, b, l)
    x = _mm([o_nsa, o_hg], w_out, res=x)
    kv = proj[:, NSA_Q:NSA_Q + NSA_KV_W].reshape(b, l, 6, NSA_KV_HEADS, NSA_HEAD_DIM)
    return x, kv[:, :, :4], kv[:, l - min(WINDOW, l):, 4:], s_new


def ab_mixer_sample(x, h, b, l, pos, past_rows, swa_buf, hg_state, lb, w_in, w1, w2, pe, hg_g, w_out):
    proj = _mm(h, w_in).reshape(b, l, AB_PACKED)
    q, kv, hq, hf, hi, hgate, g = split_cols(
        proj, (NSA_Q, NSA_KV_W, HG_QF_W, HG_QF_W, HG_V_W, HG_V_W, NSA_GATE_W))
    kv = kv.reshape(b, l, 6, NSA_KV_HEADS, NSA_HEAD_DIM)
    q = q.reshape(b, l, NSA_KV_HEADS, NSA_GROUP, NSA_HEAD_DIM)
    rows = jnp.concatenate([past_rows.astype(kv.dtype), kv[:, :, :4]], axis=1)
    o_cmp, o_slc = nsa_cmp_slc(q, pos, rows, w1, w2, pe)
    o_swa, new_buf = swa_buffered(q, pos, kv[:, :, 4:], swa_buf)
    gates = jax.nn.sigmoid(g).reshape(b, l, NSA_KV_HEADS, NSA_GROUP, 3)
    o_nsa = gates[..., 0:1] * o_cmp + gates[..., 1:2] * o_slc + gates[..., 2:3] * o_swa
    o_hg, s_new = hgrn2(hq, hf, hi, hgate, lb, hg_state, hg_g)
    mix = jnp.concatenate([o_nsa.reshape(b * l, NSA_Q), o_hg.reshape(b * l, HG_V_W)], axis=-1).astype(BF16)
    return _mm(mix, w_out, res=x), kv[:, :, :4], new_buf, s_new


def ssd_chunked(x, dt, a, bm, cm, s0):
    b, l, nh, p = x.shape
    g, n = bm.shape[2], bm.shape[3]
    r = nh // g
    c = min(SSM_CHUNK, l)
    tril = jnp.tril(jnp.ones((c, c), bool))

    def step(s, inp):
        xc, dtc, bc, cc = inp
        cum = jnp.cumsum(dtc * a, axis=1)
        seg = cum[:, :, None, :] - cum[:, None, :, :]
        lm = jnp.exp(jnp.where(tril[None, :, :, None], seg, -jnp.inf)).reshape(b, c, c, g, r)
        xdt = (xc * dtc[..., None]).reshape(b, c, g, r, p)
        cb = jnp.einsum('btgn,bsgn->btsg', cc, bc)
        sg = s.reshape(b, g, r, p, n)
        y = jnp.einsum('btsg,btsgr,bsgrp->btgrp', cb, lm, xdt)
        y = y + jnp.einsum('btgn,bgrpn->btgrp', cc, sg) * jnp.exp(cum).reshape(b, c, g, r)[..., None]
        dec = jnp.exp(cum[:, -1:] - cum).reshape(b, c, g, r)
        sg = jnp.exp(cum[:, -1]).reshape(b, g, r)[..., None, None] * sg + jnp.einsum('bsgn,bsgrp->bgrpn', bc, xdt * dec[..., None])
        return sg.reshape(b, nh, p, n), y.reshape(b, c, nh, p)

    s, y = lax.scan(step, s0.astype(F32), (to_chunks(x, c), to_chunks(dt, c), to_chunks(bm, c), to_chunks(cm, c)))
    return from_chunks(y, l), s


def mamba_prompt(x, h, b, l, w_in, conv_w, conv_b, dt_bias, a_log, d_skip, norm_g, w_out):
    proj = _mm(h, w_in)
    y, s = _ssd_prompt(proj, jnp.zeros((b, SSM_CONV - 1, SSM_CONV_DIM), F32),
                       jnp.zeros((b, SSM_HEADS * SSM_HEAD_DIM, SSM_STATE), F32),
                       conv_w, conv_b, dt_bias, a_log, d_skip, norm_g, b, l)
    assert l >= SSM_CONV - 1
    new_conv = proj.reshape(b, l, -1)[:, l - (SSM_CONV - 1):, SSM_D_INNER:SSM_D_INNER + SSM_CONV_DIM]
    return _mm(y, w_out, res=x), new_conv, s.reshape(b, SSM_HEADS, SSM_HEAD_DIM, SSM_STATE)


def mamba_sample(x, h, b, l, conv_state, ssm_state, w_in, conv_w, conv_b, dt_bias, a_log, d_skip, norm_g, w_out):
    z, xbc, dt = split_cols(_mm(h, w_in).reshape(b, l, -1), (SSM_D_INNER, SSM_CONV_DIM, SSM_HEADS))
    xpad = jnp.concatenate([conv_state.astype(xbc.dtype), xbc], axis=1)
    acc = conv_b.astype(F32)
    for j in range(SSM_CONV):
        acc = acc + xpad[:, j:j + l].astype(F32) * conv_w[j].astype(F32)
    xbc = jax.nn.silu(acc)
    new_conv = xpad[:, xpad.shape[1] - (SSM_CONV - 1):]
    xs, bm, cm = split_cols(xbc, (SSM_D_INNER, SSM_GROUPS * SSM_STATE, SSM_GROUPS * SSM_STATE))
    xs = xs.reshape(b, l, SSM_HEADS, SSM_HEAD_DIM)
    bm = bm.reshape(b, l, SSM_GROUPS, SSM_STATE)
    cm = cm.reshape(b, l, SSM_GROUPS, SSM_STATE)
    dt = jax.nn.softplus(dt.astype(F32) + dt_bias.astype(F32))
    a = -jnp.exp(a_log.astype(F32))
    y, s = ssd_chunked(xs, dt, a, bm, cm, ssm_state)
    y = y + d_skip.astype(F32)[:, None] * xs
    y = (y.reshape(b, l, SSM_D_INNER) * jax.nn.silu(z.astype(F32))).reshape(b, l, SSM_GROUPS, SSM_D_INNER // SSM_GROUPS)
    y = (y * lax.rsqrt(jnp.mean(y * y, axis=-1, keepdims=True) + EPS)).reshape(b, l, SSM_D_INNER) * norm_g.astype(F32)
    return _mm(y.reshape(b * l, SSM_D_INNER).astype(BF16), w_out, res=x), new_conv, s


def ffn_ple(x, p, g_ffn, w_ffn_in, w_ffn_out, g_ple, w_ple_gate, w_ple_up):
    x = _mm(_mm_swiglu(_rms(x, g_ffn, BF16), w_ffn_in), w_ffn_out, res=x)
    return _mm_ple(_rms(x, g_ple, BF16), w_ple_gate, p, w_ple_up, x)


def kernel(x_prompt, x_sample, cache_nsa_kv, cache_swa_kv, state_hgrn, state_ssm, cache_conv, page_table, p_prompt, p_sample, norm_mix, norm_ffn, w_ab_in, w_cmp1, w_cmp2, cmp_pe, hg_lb_logits, hg_norm, w_ab_out, w_ssm_in, ssm_conv_w, ssm_conv_b, ssm_dt_bias, ssm_a_log, ssm_d, ssm_norm, w_ssm_out, w_ffn_in, w_ffn_out, w_ple_up, w_ple_gate, norm_ple, norm_final):
    depth = norm_mix.shape[0]
    bp, lp, d = x_prompt.shape
    bs, ls, _ = x_sample.shape
    n_pages = page_table.shape[1]
    pos_s = PAST_LEN + jnp.arange(ls)
    lb_all = jnp.cumsum(jax.nn.softmax(hg_lb_logits.astype(F32), axis=0), axis=0)
    xp = x_prompt.reshape(bp * lp, d)
    xs = x_sample.reshape(bs * ls, d)
    nsa_p, nsa_s, swa_p, swa_s, hg_p, hg_s, ssm_p, ssm_s, cv_p, cv_s = [], [], [], [], [], [], [], [], [], []
    for i in range(depth):
        hp = _rms(xp, norm_mix[i], BF16)
        hs = _rms(xs, norm_mix[i], BF16)
        if i % 2 == 0:
            a = i // 2
            wa = (lb_all[a], pack_ab_in(w_ab_in[a]), w_cmp1[a], w_cmp2[a], cmp_pe[a], hg_norm[a],
                  w_ab_out[a].astype(BF16))
            xp, r_p, b_p, s_p = ab_mixer_prompt(xp, hp, bp, lp, jnp.zeros((bp, HG_HEADS, HG_DK, HG_DV), F32), *wa)
            past = cache_nsa_kv[a][page_table].reshape((bs, n_pages * PAGE_SIZE) + cache_nsa_kv.shape[3:])
            xs, r_s, b_s, s_s = ab_mixer_sample(xs, hs, bs, ls, pos_s, past, cache_swa_kv[a], state_hgrn[a], *wa)
            nsa_p.append(r_p)
            nsa_s.append(r_s)
            swa_p.append(b_p)
            swa_s.append(b_s)
            hg_p.append(s_p)
            hg_s.append(s_s)
        else:
            c = i // 2
            wc = (w_ssm_in[c].astype(BF16), ssm_conv_w[c], ssm_conv_b[c], ssm_dt_bias[c], ssm_a_log[c], ssm_d[c],
                  ssm_norm[c], w_ssm_out[c].astype(BF16))
            xp, c_p, t_p = mamba_prompt(xp, hp, bp, lp, *wc)
            xs, c_s, t_s = mamba_sample(xs, hs, bs, ls, cache_conv[c], state_ssm[c], *wc)
            cv_p.append(c_p)
            cv_s.append(c_s)
            ssm_p.append(t_p)
            ssm_s.append(t_s)
        wf = (norm_ffn[i], pack_ffn_in(w_ffn_in[i]), pack_ffn_out(w_ffn_out[i]), norm_ple[i],
              w_ple_gate[i].astype(BF16), w_ple_up[i].astype(BF16))
        xp = ffn_ple(xp, p_prompt[i].reshape(bp * lp, PLE_DIM).astype(BF16), *wf)
        xs = ffn_ple(xs, p_sample[i].reshape(bs * ls, PLE_DIM).astype(BF16), *wf)
    y_prompt = _rms(xp, norm_final, F32).reshape(bp, lp, d)
    y_sample = _rms(xs, norm_final, F32).reshape(bs, ls, d)
    return (y_prompt, y_sample, jnp.stack(nsa_p), jnp.stack(nsa_s), jnp.stack(swa_p), jnp.stack(swa_s),
            jnp.stack(hg_p), jnp.stack(hg_s), jnp.stack(ssm_p), jnp.stack(ssm_s), jnp.stack(cv_p), jnp.stack(cv_s))
```

```python
import functools
import math

import jax
import jax.numpy as jnp
import numpy as np
from jax import lax
from jax.experimental import pallas as pl
from jax.experimental.pallas import tpu as pltpu

D_MODEL = 4096
PAST_LEN = 16384
PAGE_SIZE = 128
PLE_DIM = 256
NSA_HEADS = 16
NSA_KV_HEADS = 4
NSA_GROUP = NSA_HEADS // NSA_KV_HEADS
NSA_HEAD_DIM = 128
NSA_Q = NSA_HEADS * NSA_HEAD_DIM
NSA_KV_W = 6 * NSA_KV_HEADS * NSA_HEAD_DIM
NSA_GATE_W = 3 * NSA_HEADS
CMP_BLOCK = 32
CMP_STRIDE = 16
CMP_HIDDEN = 256
SLC_BLOCK = 64
N_SELECT = 16
WINDOW = 512
SLC_QBLOCK = 32
SWA_QBLOCK = 128
FORCE_SCORE = 1e4
HG_HEADS = 16
HG_DK = 128
HG_DV = (D_MODEL // 2) // HG_HEADS
HG_QF_W = HG_HEADS * HG_DK
HG_V_W = HG_HEADS * HG_DV
HG_CHUNK = 32
AB_SIZES = (NSA_Q, NSA_KV_W, NSA_GATE_W, HG_QF_W, HG_QF_W, HG_V_W, HG_V_W)
SSM_D_INNER = 2 * D_MODEL
SSM_HEAD_DIM = 64
SSM_HEADS = SSM_D_INNER // SSM_HEAD_DIM
SSM_GROUPS = 8
SSM_STATE = 128
SSM_CONV = 4
SSM_CONV_DIM = SSM_D_INNER + 2 * SSM_GROUPS * SSM_STATE
SSM_CHUNK = 64
D_FF = ((8 * D_MODEL + 3 * 256 - 1) // (3 * 256)) * 256
EPS = 1e-6
NEG = -1e30

LANE = 128
VMEM_LIMIT_BYTES = 56 * 1024 * 1024
BF16 = jnp.bfloat16
F32 = jnp.float32


MAX_ROW_TILE = 1024
NORM_ROW_TILE = 512


def _rms_kernel(x_ref, g_ref, o_ref):
    x = x_ref[...]
    y = x * lax.rsqrt(jnp.mean(x * x, axis=-1, keepdims=True) + EPS)
    o_ref[...] = (y * g_ref[...]).astype(o_ref.dtype)


def _rms(x, g, out_dtype):
    m, d = x.shape
    tm = min(m, NORM_ROW_TILE)
    assert m % tm == 0
    return pl.pallas_call(
        _rms_kernel,
        grid=(m // tm,),
        in_specs=[pl.BlockSpec((tm, d), lambda i: (i, 0)), pl.BlockSpec((1, d), lambda i: (0, 0))],
        out_specs=pl.BlockSpec((tm, d), lambda i: (i, 0)),
        out_shape=jax.ShapeDtypeStruct((m, d), out_dtype),
        compiler_params=pltpu.CompilerParams(dimension_semantics=("arbitrary",),
                                             vmem_limit_bytes=VMEM_LIMIT_BYTES),
        name="rms",
    )(x, g.astype(F32)[None])


def _mm_kernel(*refs, n_lhs, residual):
    x_refs = refs[:n_lhs]
    w_ref = refs[n_lhs]
    o_ref = refs[-1]
    k = pl.program_id(2)

    def first(x_ref):
        acc = jnp.dot(x_ref[...], w_ref[...].astype(BF16), preferred_element_type=F32)
        o_ref[...] = acc + refs[n_lhs + 1][...] if residual else acc

    def later(x_ref):
        o_ref[...] += jnp.dot(x_ref[...], w_ref[...].astype(BF16), preferred_element_type=F32)

    pl.when(k == 0)(functools.partial(first, x_refs[0]))
    if n_lhs == 1:
        pl.when(k > 0)(functools.partial(later, x_refs[0]))
    else:
        for p in range(1, n_lhs):
            pl.when(k == p)(functools.partial(later, x_refs[p]))


def _w_tile_bytes(w):
    return 2 * w.dtype.itemsize + (2 if w.dtype != BF16 else 0)


def _k_tile(k, x_bytes_per_k, w_bytes_per_k, budget):
    units = k // LANE
    for parts in range(1, units + 1):
        if units % parts == 0 and (x_bytes_per_k + w_bytes_per_k) * (units // parts) * LANE <= budget:
            return (units // parts) * LANE
    raise ValueError(f"no K tile for {k}")


def _mm(xs, w, res=None):
    xs = list(xs) if isinstance(xs, (list, tuple)) else [xs]
    m = xs[0].shape[0]
    k, n = w.shape
    tm = min(m, MAX_ROW_TILE)
    n_out_bufs = 4 if res is not None else 2
    widest = 1024 if w.dtype == BF16 else 512
    for tn in (min(n, widest), min(n, widest // 2)):
        budget = VMEM_LIMIT_BYTES - n_out_bufs * tm * tn * 4 - (4 << 20)
        x_per_k = 2 * 2 * tm * len(xs)
        w_per_k = _w_tile_bytes(w) * tn
        tk = _k_tile(k, x_per_k, w_per_k, budget) if len(xs) == 1 else k // len(xs)
        if tk == k or tk >= 2048:
            break
    if len(xs) == 1:
        x_specs = [pl.BlockSpec((tm, tk), lambda i, j, kk: (i, kk))]
    else:
        assert all(x.shape[1] == tk for x in xs) and (x_per_k + w_per_k) * tk <= budget
        x_specs = [pl.BlockSpec((tm, tk), lambda i, j, kk: (i, 0)) for _ in xs]
    assert m % tm == 0 and k % tk == 0 and tk % LANE == 0
    tile = pl.BlockSpec((tm, tn), lambda i, j, kk: (i, j))
    return pl.pallas_call(
        functools.partial(_mm_kernel, n_lhs=len(xs), residual=res is not None),
        grid=(m // tm, pl.cdiv(n, tn), k // tk),
        in_specs=x_specs + [pl.BlockSpec((tk, tn), lambda i, j, kk: (kk, j))] + ([tile] if res is not None else []),
        out_specs=tile,
        out_shape=jax.ShapeDtypeStruct((m, n), F32),
        compiler_params=pltpu.CompilerParams(
            dimension_semantics=("arbitrary", "arbitrary", "arbitrary"),
            vmem_limit_bytes=VMEM_LIMIT_BYTES),
        name="mm",
    )(*xs, w, *([res] if res is not None else []))


FFN_TILE = 256


def _swiglu_kernel(x_ref, wg_ref, wu_ref, o_ref):
    x = x_ref[...]
    gate = jnp.dot(x, wg_ref[...].astype(BF16), preferred_element_type=F32)
    up = jnp.dot(x, wu_ref[...].astype(BF16), preferred_element_type=F32)
    o_ref[...] = (gate * jax.nn.sigmoid(gate) * up).astype(o_ref.dtype)


def _mm_swiglu(x, w):
    m, k = x.shape
    hidden = w.shape[1] // 2
    tm = min(m, MAX_ROW_TILE)
    nj = hidden // FFN_TILE
    assert m % tm == 0 and hidden % FFN_TILE == 0 and w.shape[0] == k
    return pl.pallas_call(
        _swiglu_kernel,
        grid=(m // tm, nj),
        in_specs=[pl.BlockSpec((tm, k), lambda i, j: (i, 0)),
                  pl.BlockSpec((k, FFN_TILE), lambda i, j: (0, j)),
                  pl.BlockSpec((k, FFN_TILE), lambda i, j: (0, nj + j))],
        out_specs=pl.BlockSpec((tm, FFN_TILE), lambda i, j: (i, j)),
        out_shape=jax.ShapeDtypeStruct((m, hidden), BF16),
        compiler_params=pltpu.CompilerParams(dimension_semantics=("arbitrary", "arbitrary"),
                                             vmem_limit_bytes=VMEM_LIMIT_BYTES),
        name="mm_swiglu",
    )(x, w, w)


PLE_COL_TILE = 512


def _ple_kernel(t_ref, wg_ref, p_ref, wu_ref, res_ref, o_ref):
    gate = jnp.dot(t_ref[...], wg_ref[...].astype(BF16), preferred_element_type=F32)
    up = jnp.dot(p_ref[...], wu_ref[...].astype(BF16), preferred_element_type=F32)
    o_ref[...] = res_ref[...] + up * jax.nn.sigmoid(gate)


def _mm_ple(t, w_gate, p, w_up, res):
    m, k = t.shape
    n = w_gate.shape[1]
    kp = p.shape[1]
    tm = min(m, MAX_ROW_TILE)
    tn = PLE_COL_TILE
    assert m % tm == 0 and n % tn == 0
    tile = pl.BlockSpec((tm, tn), lambda i, j: (i, j))
    return pl.pallas_call(
        _ple_kernel,
        grid=(m // tm, n // tn),
        in_specs=[pl.BlockSpec((tm, k), lambda i, j: (i, 0)), pl.BlockSpec((k, tn), lambda i, j: (0, j)),
                  pl.BlockSpec((tm, kp), lambda i, j: (i, 0)), pl.BlockSpec((kp, tn), lambda i, j: (0, j)), tile],
        out_specs=tile,
        out_shape=jax.ShapeDtypeStruct((m, n), F32),
        compiler_params=pltpu.CompilerParams(dimension_semantics=("arbitrary", "arbitrary"),
                                             vmem_limit_bytes=VMEM_LIMIT_BYTES),
        name="mm_ple",
    )(t, w_gate, p, w_up, res)


NSA_TQ = 256
NSA_KEY_BUCKET = 512


def _dot_nt(a, b):
    return lax.dot_general(a, b, (((1,), (1,)), ((), ())), preferred_element_type=F32)


def _group_attend(q4, k, v, mask, scale, tq):
    s4 = _dot_nt(q4, k) * scale
    es, dens = [], []
    for r in range(NSA_GROUP):
        s = jnp.where(mask, s4[r * tq:(r + 1) * tq], NEG)
        e = jnp.exp(s - jnp.max(s, axis=-1, keepdims=True))
        dens.append(jnp.sum(e, axis=-1, keepdims=True))
        es.append(e.astype(BF16))
    o4 = jnp.dot(jnp.concatenate(es, axis=0), v, preferred_element_type=F32)
    return [o4[r * tq:(r + 1) * tq] / dens[r] for r in range(NSA_GROUP)]


def _nsa_prompt_kernel(q_ref, kc_ref, vc_ref, ks_ref, vs_ref, kw_ref, vw_ref, gate_ref, cover_ref, expand_ref,
                       o_ref, slc_s, *, seq):
    tq = NSA_TQ
    hd = NSA_HEAD_DIM
    n_cmp = seq // CMP_STRIDE - CMP_BLOCK // CMP_STRIDE + 1
    n_slc = seq // SLC_BLOCK
    span = WINDOW + tq
    scale = hd ** -0.5
    q0 = pl.program_id(2) * tq
    pos = q0 + lax.broadcasted_iota(jnp.int32, (tq, 1), 0)
    lane = lax.broadcasted_iota(jnp.int32, (tq, LANE), 1)
    q4 = jnp.concatenate([q_ref[:, r * hd:(r + 1) * hd].astype(BF16) for r in range(NSA_GROUP)], axis=0)

    cmp_ok = (lane * CMP_STRIDE + (CMP_BLOCK - 1) <= pos) & (lane < n_cmp)
    s4 = _dot_nt(q4, kc_ref[...]) * scale
    psum = jnp.zeros((tq, LANE), F32)
    ps = []
    for r in range(NSA_GROUP):
        s = jnp.where(cmp_ok, s4[r * tq:(r + 1) * tq], NEG)
        e = jnp.where(cmp_ok, jnp.exp(s - jnp.max(s, axis=-1, keepdims=True)), 0.0)
        den = jnp.sum(e, axis=-1, keepdims=True)
        p = e / jnp.where(den > 0.0, den, 1.0)
        psum = psum + p
        ps.append(p.astype(BF16))
    o_cmp4 = jnp.dot(jnp.concatenate(ps, axis=0), vc_ref[...], preferred_element_type=F32)

    score = jnp.dot(psum, cover_ref[...], preferred_element_type=F32, precision=lax.Precision.HIGHEST)
    cur = pos // SLC_BLOCK
    forced = (lane == 0) | (lane == cur) | (lane == cur - 1)
    causal = lane * SLC_BLOCK <= pos
    score = jnp.where(forced, FORCE_SCORE, jnp.where(causal, score, -1.0))
    score = jnp.where(lane < n_slc, score, -2.0)
    rank = jnp.zeros((tq, LANE), jnp.int32)
    for i in range(n_slc):
        ci = score[:, i:i + 1]
        beats = (ci > score) | ((ci == score) & (lane > i))
        rank = rank + beats.astype(jnp.int32)
    sel = ((rank < min(N_SELECT, n_slc)) & (lane < n_slc)).astype(BF16)

    def selected(nk):
        sel_keys = jnp.dot(sel, expand_ref[:, :nk], preferred_element_type=F32)
        kpos = lax.broadcasted_iota(jnp.int32, (tq, nk), 1)
        outs = _group_attend(q4, ks_ref[:nk, :].astype(BF16), vs_ref[:nk, :].astype(BF16),
                             (sel_keys > 0.5) & (kpos <= pos), scale, tq)
        for r in range(NSA_GROUP):
            slc_s[r] = outs[r]

    n_bucket = (q0 + tq + NSA_KEY_BUCKET - 1) // NSA_KEY_BUCKET
    for nb in range(1, seq // NSA_KEY_BUCKET + 1):
        pl.when(n_bucket == nb)(functools.partial(selected, nb * NSA_KEY_BUCKET))

    w0 = pl.multiple_of(jnp.maximum(q0 - WINDOW, 0), tq)
    wpos = w0 + lax.broadcasted_iota(jnp.int32, (tq, span), 1)
    o_swa = _group_attend(q4, kw_ref[pl.ds(w0, span), :].astype(BF16), vw_ref[pl.ds(w0, span), :].astype(BF16),
                          (wpos <= pos) & (pos - wpos < WINDOW), scale, tq)

    for r in range(NSA_GROUP):
        g = gate_ref[:, 3 * r:3 * r + 3]
        o = g[:, 0:1] * o_cmp4[r * tq:(r + 1) * tq] + g[:, 1:2] * slc_s[r] + g[:, 2:3] * o_swa[r]
        o_ref[:, r * hd:(r + 1) * hd] = o.astype(o_ref.dtype)


def _nsa_prompt(proj, kc, vc, gates, batch, seq):
    assert seq % NSA_TQ == 0 and seq // SLC_BLOCK <= LANE and seq // CMP_STRIDE <= LANE + 1
    assert seq % NSA_KEY_BUCKET == 0 and seq >= WINDOW + NSA_TQ and WINDOW % NSA_TQ == 0
    nt = seq // NSA_TQ
    n_cmp = seq // CMP_STRIDE - CMP_BLOCK // CMP_STRIDE + 1
    n_slc = seq // SLC_BLOCK
    c0 = np.arange(LANE)[:, None] * CMP_STRIDE
    s0 = np.arange(LANE)[None, :] * SLC_BLOCK
    cover = np.clip(np.minimum(c0 + CMP_BLOCK, s0 + SLC_BLOCK) - np.maximum(c0, s0), 0, None) / CMP_BLOCK
    cover = cover * (np.arange(LANE)[:, None] < n_cmp) * (np.arange(LANE)[None, :] < n_slc)
    expand = (np.arange(seq)[None, :] // SLC_BLOCK == np.arange(LANE)[:, None])
    hd = NSA_HEAD_DIM
    gw = NSA_GROUP * hd

    def kv_spec(slot):
        return pl.BlockSpec((seq, hd), lambda b, g, t: (b, NSA_Q // hd + slot * NSA_KV_HEADS + g))

    cmp_spec = pl.BlockSpec((None, None, LANE, hd), lambda b, g, t: (b, g, 0, 0))
    return pl.pallas_call(
        functools.partial(_nsa_prompt_kernel, seq=seq),
        grid=(batch, NSA_KV_HEADS, nt),
        in_specs=[pl.BlockSpec((NSA_TQ, gw), lambda b, g, t: (b * nt + t, g)),
                  cmp_spec, cmp_spec, kv_spec(2), kv_spec(3), kv_spec(4), kv_spec(5),
                  pl.BlockSpec((NSA_TQ, LANE), lambda b, g, t: (b * nt + t, g)),
                  pl.BlockSpec((LANE, LANE), lambda b, g, t: (0, 0)),
                  pl.BlockSpec((LANE, seq), lambda b, g, t: (0, 0))],
        out_specs=pl.BlockSpec((NSA_TQ, gw), lambda b, g, t: (b * nt + t, g)),
        out_shape=jax.ShapeDtypeStruct((batch * seq, NSA_Q), BF16),
        scratch_shapes=[pltpu.VMEM((NSA_GROUP, NSA_TQ, hd), F32)],
        compiler_params=pltpu.CompilerParams(
            dimension_semantics=("arbitrary", "arbitrary", "arbitrary"),
            vmem_limit_bytes=VMEM_LIMIT_BYTES),
        name="nsa_prompt",
    )(proj, kc, vc, proj, proj, proj, proj, gates, jnp.asarray(cover, F32), jnp.asarray(expand, BF16))


def _compress_kernel(x_ref, w1_ref, w2_ref, pe_ref, o_ref, *, seq):
    n_sub = seq // CMP_STRIDE
    n_cmp = n_sub - CMP_BLOCK // CMP_STRIDE + 1
    acc = jnp.zeros((n_sub, 2 * CMP_HIDDEN), F32)
    for r in range(CMP_STRIDE):
        xr = x_ref[pl.ds(r, n_sub, stride=CMP_STRIDE), :].astype(BF16)
        acc = acc + jnp.dot(xr, w1_ref[r], preferred_element_type=F32)
    hid = acc[:, :CMP_HIDDEN] + pltpu.roll(acc[:, CMP_HIDDEN:], n_sub - 1, 0) + pe_ref[...]
    c = jnp.dot((hid * jax.nn.sigmoid(hid)).astype(BF16), w2_ref[...], preferred_element_type=F32)
    row = lax.broadcasted_iota(jnp.int32, c.shape, 0)
    o_ref[...] = jnp.where(row < n_cmp, c, 0.0).astype(o_ref.dtype)


def _compress_prompt(proj, w1, w2, pe, batch, seq):
    assert CMP_BLOCK == 2 * CMP_STRIDE and seq // CMP_STRIDE == LANE and NSA_HEAD_DIM == LANE
    hd = NSA_HEAD_DIM
    w1p = w1.reshape(2, 2, CMP_STRIDE, hd, CMP_HIDDEN).transpose(0, 2, 3, 1, 4)
    w1p = w1p.reshape(2, CMP_STRIDE, hd, 2 * CMP_HIDDEN).astype(BF16)
    pe_h = jnp.einsum('sc,sch->sh', pe.reshape(2, -1), w1)[:, None, :]
    return pl.pallas_call(
        functools.partial(_compress_kernel, seq=seq),
        grid=(batch, 2, NSA_KV_HEADS),
        in_specs=[pl.BlockSpec((seq, hd), lambda b, s, g: (b, NSA_Q // hd + s * NSA_KV_HEADS + g)),
                  pl.BlockSpec((None, CMP_STRIDE, hd, 2 * CMP_HIDDEN), lambda b, s, g: (s, 0, 0, 0)),
                  pl.BlockSpec((None, CMP_HIDDEN, hd), lambda b, s, g: (s, 0, 0)),
                  pl.BlockSpec((None, 1, CMP_HIDDEN), lambda b, s, g: (s, 0, 0))],
        out_specs=pl.BlockSpec((None, None, None, LANE, hd), lambda b, s, g: (s, b, g, 0, 0)),
        out_shape=jax.ShapeDtypeStruct((2, batch, NSA_KV_HEADS, LANE, hd), BF16),
        compiler_params=pltpu.CompilerParams(dimension_semantics=("arbitrary", "arbitrary", "arbitrary"),
                                             vmem_limit_bytes=VMEM_LIMIT_BYTES),
        name="compress_prompt",
    )(proj, w1p, w2.astype(BF16), pe_h)


HG_SLAB = 256
HG_UNROLL = 8


def _hgrn_kernel(hq_ref, hf_ref, hi_ref, hg_ref, lb_ref, gn_ref, s0_ref, tri_ref, o_ref, s_ref,
                 qs_s, el_s, o_s, u_s, st_s, *, seq):
    c = HG_CHUNK
    per_slab = HG_SLAB // c
    lb = lb_ref[...]

    def prep(i, carry):
        rows = pl.ds(pl.multiple_of(i * HG_SLAB, HG_SLAB), HG_SLAB)
        hq = hq_ref[rows, :]
        q = hq * jax.nn.sigmoid(hq)
        f = lb + (1.0 - lb) * jax.nn.sigmoid(hf_ref[rows, :])
        k = 1.0 - f
        v = hi_ref[rows, :].astype(BF16)
        bc = _sel_dot(tri_ref[0], jnp.log(f))
        bc3 = bc.reshape(per_slab, c, LANE)
        mid = (c - 1) // 2
        bm = jnp.broadcast_to(bc3[:, mid:mid + 1, :], bc3.shape).reshape(HG_SLAB, LANE)
        bl = jnp.broadcast_to(bc3[:, c - 1:c, :], bc3.shape).reshape(HG_SLAB, LANE)
        a = _dot_nt((q * jnp.exp(bc - bm)).astype(BF16), (k * jnp.exp(bm - bc)).astype(BF16))
        a = jnp.where(tri_ref[0] > 0, a, 0.0)
        o_s[rows, :] = jnp.dot(a.astype(BF16), v, preferred_element_type=F32)
        qs_s[rows, :] = (q * jnp.exp(bc)).astype(qs_s.dtype)
        kd = (k * jnp.exp(bl - bc)).astype(BF16)
        el = jnp.exp(bl)
        for j in range(per_slab):
            ch = slice(j * c, (j + 1) * c)
            u_s[i * per_slab + j] = lax.dot_general(v[ch], kd[ch], (((0,), (0,)), ((), ())),
                                                    preferred_element_type=F32)
            el_s[pl.ds(i * per_slab + j, 1), :] = el[j * c:j * c + 1]
        return carry

    lax.fori_loop(0, seq // HG_SLAB, prep, 0)

    def scan(ci, st):
        st_s[ci] = st.astype(st_s.dtype)
        return st * el_s[pl.ds(ci, 1), :] + u_s[ci]

    st = lax.fori_loop(0, seq // c, scan, s0_ref[...].T, unroll=HG_UNROLL)
    s_ref[...] = st.T

    gn = gn_ref[...]

    def finish(i, carry):
        rows = pl.ds(pl.multiple_of(i * HG_SLAB, HG_SLAB), HG_SLAB)
        qs = qs_s[rows, :]
        inter = [_dot_nt(qs[j * c:(j + 1) * c], st_s[i * per_slab + j]) for j in range(per_slab)]
        o = o_s[rows, :] + jnp.concatenate(inter, axis=0)
        hg = hg_ref[rows, :]
        o = o * lax.rsqrt(jnp.mean(o * o, axis=-1, keepdims=True) + EPS)
        o_ref[rows, :] = (o * gn * (hg * jax.nn.sigmoid(hg))).astype(o_ref.dtype)
        return carry

    lax.fori_loop(0, seq // HG_SLAB, finish, 0)


def _hgrn_tri():
    i = np.arange(HG_SLAB)[:, None]
    j = np.arange(HG_SLAB)[None, :]
    cum = ((i // HG_CHUNK) == (j // HG_CHUNK)) & (j <= i)
    return jnp.asarray(cum[None], BF16)


def _hgrn_prompt(proj, col0, lb, g_norm, s0, batch, seq):
    assert seq % HG_SLAB == 0 and col0 % LANE == 0 and HG_DK == LANE and HG_DV == LANE
    c0 = col0 // LANE

    def col_spec(group):
        return pl.BlockSpec((seq, LANE), lambda b, h: (b, c0 + group * HG_HEADS + h))

    vec_spec = pl.BlockSpec((1, LANE), lambda b, h: (0, h))
    st_spec = pl.BlockSpec((None, None, HG_DK, HG_DV), lambda b, h: (b, h, 0, 0))
    return pl.pallas_call(
        functools.partial(_hgrn_kernel, seq=seq),
        grid=(batch, HG_HEADS),
        in_specs=[col_spec(0), col_spec(1), col_spec(2), col_spec(3), vec_spec, vec_spec, st_spec,
                  pl.BlockSpec((1, HG_SLAB, HG_SLAB), lambda b, h: (0, 0, 0))],
        out_specs=[pl.BlockSpec((seq, LANE), lambda b, h: (b, h)), st_spec],
        out_shape=[jax.ShapeDtypeStruct((batch * seq, HG_V_W), BF16),
                   jax.ShapeDtypeStruct((batch, HG_HEADS, HG_DK, HG_DV), F32)],
        scratch_shapes=[pltpu.VMEM((seq, LANE), BF16), pltpu.VMEM((seq // HG_CHUNK, LANE), F32),
                        pltpu.VMEM((seq, LANE), F32), pltpu.VMEM((seq // HG_CHUNK, HG_DV, HG_DK), F32),
                        pltpu.VMEM((seq // HG_CHUNK, HG_DV, HG_DK), BF16)],
        compiler_params=pltpu.CompilerParams(
            dimension_semantics=("arbitrary", "arbitrary"),
            vmem_limit_bytes=VMEM_LIMIT_BYTES),
        name="hgrn_prompt",
    )(proj, proj, proj, proj, lb, g_norm, s0, _hgrn_tri())


SSD_TL = 256
SSD_GW = SSM_D_INNER // SSM_GROUPS
SSD_HPG = SSM_HEADS // SSM_GROUPS


def _split3(x):
    hi = x.astype(BF16)
    r1 = x - hi.astype(F32)
    mid = r1.astype(BF16)
    lo = (r1 - mid.astype(F32)).astype(BF16)
    return hi, mid, lo


def _sel_dot(sel, x):
    hi, mid, lo = _split3(x)
    d = functools.partial(jnp.dot, preferred_element_type=F32)
    return d(sel, hi) + d(sel, mid) + d(sel, lo)


def _dot_sel(x, sel):
    hi, mid, lo = _split3(x)
    d = functools.partial(jnp.dot, preferred_element_type=F32)
    return d(hi, sel) + d(mid, sel) + d(lo, sel)


def _causal_conv_silu(x, prev8, w, bias):
    row8 = lax.broadcasted_iota(jnp.int32, prev8.shape, 0)
    acc = bias
    for k in range(SSM_CONV - 1, 0, -1):
        r = pltpu.roll(x, k, 0)
        top = jnp.where(row8 < k, pltpu.roll(prev8, k, 0), r[:8])
        acc = acc + jnp.concatenate([top, r[8:]], axis=0) * w[SSM_CONV - 1 - k:SSM_CONV - k]
    acc = acc + x * w[SSM_CONV - 1:SSM_CONV]
    return acc * jax.nn.sigmoid(acc)


def _ssd_kernel(z_ref, x_ref, b_ref, c_ref, dt_ref, cx0_ref, cb0_ref, cc0_ref, s0_ref,
                wx_ref, wb_ref, wc_ref, bx_ref, bb_ref, bc_ref, dtb_ref, alog_ref, d_ref, gn_ref,
                tri_ref, hsel_ref, o_ref, s_ref,
                st_s, tx_s, tb_s, tc_s, xdt_s, xdd_s, ce_s, le_s, bm_s, cm_s, y_s):
    lt = pl.program_id(2)
    c = SSM_CHUNK
    tl = SSD_TL
    hp = SSM_HEAD_DIM

    @pl.when(lt == 0)
    def _():
        st_s[...] = s0_ref[...].T
        for tail, c0 in ((tx_s, cx0_ref), (tb_s, cb0_ref), (tc_s, cc0_ref)):
            tail[...] = jnp.zeros(tail.shape, F32)
            tail[8 - (SSM_CONV - 1):8, :] = c0[...]

    x_raw = x_ref[...]
    b_raw = b_ref[...]
    c_raw = c_ref[...]
    xs = _causal_conv_silu(x_raw, tx_s[...], wx_ref[...], bx_ref[...])
    bm_s[...] = _causal_conv_silu(b_raw, tb_s[...], wb_ref[...], bb_ref[...]).astype(bm_s.dtype)
    cm_s[...] = _causal_conv_silu(c_raw, tc_s[...], wc_ref[...], bc_ref[...]).astype(cm_s.dtype)
    tx_s[...] = x_raw[tl - 8:]
    tb_s[...] = b_raw[tl - 8:]
    tc_s[...] = c_raw[tl - 8:]

    dt = jax.nn.softplus(dt_ref[...] + dtb_ref[...])
    cum = _sel_dot(tri_ref[0], dt * -jnp.exp(alog_ref[...]))
    hsel = hsel_ref[...]
    dt_e = _dot_sel(dt, hsel)
    cum_e = _dot_sel(cum, hsel)
    last_e = _sel_dot(tri_ref[1], cum_e)
    xdt = xs * dt_e
    xdt_s[...] = xdt.astype(xdt_s.dtype)
    xdd_s[...] = (xdt * jnp.exp(last_e - cum_e)).astype(xdd_s.dtype)
    ce_s[...] = cum_e
    le_s[...] = jnp.exp(last_e)

    trow = lax.broadcasted_iota(jnp.int32, (c, SSD_GW), 0)
    lane_g = lax.broadcasted_iota(jnp.int32, (c, SSD_GW), 1)
    diag = (lane_g % hp) == trow
    t2 = lax.broadcasted_iota(jnp.int32, (c, LANE), 0)
    l2 = lax.broadcasted_iota(jnp.int32, (c, LANE), 1)
    tril2 = (l2 % hp) <= t2
    r3 = lax.broadcasted_iota(jnp.int32, (2 * c, LANE), 0)
    l3 = lax.broadcasted_iota(jnp.int32, (2 * c, LANE), 1)
    blockdiag = (r3 // c) == (l3 // hp)

    for ci in range(tl // c):
        rows = pl.ds(ci * c, c)
        cm = cm_s[rows, :]
        bm = bm_s[rows, :]
        cum_c = ce_s[rows, :]
        cum_row = jnp.sum(jnp.where(diag, cum_c, 0.0), axis=0, keepdims=True)
        cb2 = _dot_nt(cm, jnp.concatenate([bm, bm], axis=0))
        xdt_c = xdt_s[rows, :]
        st = st_s[...]
        y_inter = jnp.dot(cm, st.astype(cm.dtype), preferred_element_type=F32) * jnp.exp(cum_c)
        for j in range(SSD_GW // LANE):
            cols = slice(j * LANE, (j + 1) * LANE)
            seg = cum_c[:, cols] - cum_row[:, cols]
            w = (cb2 * jnp.where(tril2, jnp.exp(seg), 0.0)).astype(xdt_c.dtype)
            xj = xdt_c[:, cols]
            xbd = jnp.where(blockdiag, jnp.concatenate([xj, xj], axis=0), jnp.zeros_like(xj[:1, :1]))
            y_s[rows, cols] = y_inter[:, cols] + jnp.dot(w, xbd, preferred_element_type=F32)
        upd = lax.dot_general(bm, xdd_s[rows, :], (((0,), (0,)), ((), ())), preferred_element_type=F32)
        st_s[...] = st * le_s[pl.ds(ci * c, 1), :] + upd

    z = z_ref[...]
    y = (y_s[...] + d_ref[...] * xs) * (z * jax.nn.sigmoid(z))
    y = y * lax.rsqrt(jnp.mean(y * y, axis=-1, keepdims=True) + EPS)
    o_ref[...] = (y * gn_ref[...]).astype(o_ref.dtype)

    @pl.when(lt == pl.num_programs(2) - 1)
    def _():
        s_ref[...] = st_s[...].T


def _ssd_consts():
    i = np.arange(SSD_TL)[:, None]
    j = np.arange(SSD_TL)[None, :]
    cum = ((i // SSM_CHUNK) == (j // SSM_CHUNK)) & (j <= i)
    last = j == (i // SSM_CHUNK) * SSM_CHUNK + SSM_CHUNK - 1
    tri = jnp.asarray(np.stack([cum, last]), BF16)
    h = np.arange(SSM_HEADS)[None, :, None]
    g = np.arange(SSM_GROUPS)[:, None, None]
    lane = np.arange(SSD_GW)[None, None, :]
    hsel = jnp.asarray(h == g * SSD_HPG + lane // SSM_HEAD_DIM, BF16)
    return tri, hsel


def _ssd_prompt(proj, conv0, s0, conv_w, conv_b, dt_bias, a_log, d_skip, norm_g, batch, seq):
    assert seq % SSD_TL == 0 and SSM_STATE == LANE and SSM_HEADS == LANE and SSM_CHUNK == SSM_HEAD_DIM
    nl = seq // SSD_TL
    gpl = SSD_GW // LANE
    xb = SSM_D_INNER // SSD_GW
    bb = 2 * SSM_D_INNER // LANE
    cb = bb + SSM_GROUPS
    db = cb + SSM_GROUPS
    tri, hsel = _ssd_consts()
    d_e = jnp.repeat(d_skip.astype(F32), SSM_HEAD_DIM)[None]
    row = lambda a: a.astype(F32)[None]

    def rows_spec(width, col_fn):
        return pl.BlockSpec((SSD_TL, width), lambda b, g, t: (b * nl + t, col_fn(g)))

    def conv0_spec(width, col_fn):
        return pl.BlockSpec((None, SSM_CONV - 1, width), lambda b, g, t: (b, 0, col_fn(g)))

    def vec_spec(rows_, width, col_fn):
        return pl.BlockSpec((rows_, width), lambda b, g, t: (0, col_fn(g)))

    xcol = lambda g: g
    bcol = lambda g: SSM_D_INNER // LANE + g
    ccol = lambda g: SSM_D_INNER // LANE + SSM_GROUPS + g
    zero = lambda g: 0
    st_spec = pl.BlockSpec((None, SSD_GW, SSM_STATE), lambda b, g, t: (b, g, 0))
    return pl.pallas_call(
        _ssd_kernel,
        grid=(batch, SSM_GROUPS, nl),
        in_specs=[rows_spec(SSD_GW, xcol), rows_spec(SSD_GW, lambda g: xb + g),
                  rows_spec(LANE, lambda g: bb + g), rows_spec(LANE, lambda g: cb + g),
                  rows_spec(LANE, lambda g: db),
                  conv0_spec(SSD_GW, xcol), conv0_spec(LANE, bcol), conv0_spec(LANE, ccol), st_spec,
                  vec_spec(SSM_CONV, SSD_GW, xcol), vec_spec(SSM_CONV, LANE, bcol), vec_spec(SSM_CONV, LANE, ccol),
                  vec_spec(1, SSD_GW, xcol), vec_spec(1, LANE, bcol), vec_spec(1, LANE, ccol),
                  vec_spec(1, LANE, zero), vec_spec(1, LANE, zero),
                  vec_spec(1, SSD_GW, xcol), vec_spec(1, SSD_GW, xcol),
                  pl.BlockSpec((2, SSD_TL, SSD_TL), lambda b, g, t: (0, 0, 0)),
                  pl.BlockSpec((None, SSM_HEADS, SSD_GW), lambda b, g, t: (g, 0, 0))],
        out_specs=[rows_spec(SSD_GW, xcol), st_spec],
        out_shape=[jax.ShapeDtypeStruct((batch * seq, SSM_D_INNER), BF16),
                   jax.ShapeDtypeStruct((batch, SSM_HEADS * SSM_HEAD_DIM, SSM_STATE), F32)],
        scratch_shapes=[pltpu.VMEM((SSM_STATE, SSD_GW), F32),
                        pltpu.VMEM((8, SSD_GW), F32), pltpu.VMEM((8, LANE), F32), pltpu.VMEM((8, LANE), F32),
                        pltpu.VMEM((SSD_TL, SSD_GW), BF16), pltpu.VMEM((SSD_TL, SSD_GW), BF16),
                        pltpu.VMEM((SSD_TL, SSD_GW), F32), pltpu.VMEM((SSD_TL, SSD_GW), F32),
                        pltpu.VMEM((SSD_TL, LANE), BF16), pltpu.VMEM((SSD_TL, LANE), BF16),
                        pltpu.VMEM((SSD_TL, SSD_GW), F32)],
        compiler_params=pltpu.CompilerParams(
            dimension_semantics=("arbitrary", "arbitrary", "arbitrary"),
            vmem_limit_bytes=VMEM_LIMIT_BYTES),
        name="ssd_prompt",
    )(proj, proj, proj, proj, proj, conv0, conv0, conv0, s0,
      conv_w, conv_w, conv_w, row(conv_b), row(conv_b), row(conv_b), row(dt_bias), row(a_log),
      d_e, row(norm_g), tri, hsel)


def split_cols(a, sizes):
    out, o = [], 0
    for s in sizes:
        out.append(a[..., o:o + s])
        o += s
    return out


def to_chunks(a, c):
    b, l = a.shape[:2]
    pad = (-l) % c
    a = jnp.pad(a.astype(F32), [(0, 0), (0, pad)] + [(0, 0)] * (a.ndim - 2))
    return jnp.moveaxis(a.reshape((b, (l + pad) // c, c) + a.shape[2:]), 1, 0)


def from_chunks(a, l):
    a = jnp.moveaxis(a, 0, 1)
    return a.reshape((a.shape[0], a.shape[1] * a.shape[2]) + a.shape[3:])[:, :l]


def gqa_attend(q, k, v, mask):
    s = jnp.einsum('...qgrd,...kgd->...qgrk', q, k, preferred_element_type=F32) * (q.shape[-1] ** -0.5)
    m = mask[..., :, None, None, :]
    p = jnp.where(m, jax.nn.softmax(jnp.where(m, s, NEG), axis=-1), 0.0)
    o = jnp.einsum('...qgrk,...kgd->...qgrd', p.astype(v.dtype), v)
    return o, p


def compress_rows(rows, w1, w2, pe):
    b, t = rows.shape[:2]
    n_sub = t // CMP_STRIDE
    n_per = CMP_BLOCK // CMP_STRIDE
    n_cmp = n_sub - n_per + 1
    sub = rows[:, :n_sub * CMP_STRIDE].reshape(b, n_sub, CMP_STRIDE, NSA_KV_HEADS, NSA_HEAD_DIM)
    sub = jnp.moveaxis(sub, 3, 2).reshape(b, n_sub, NSA_KV_HEADS, CMP_STRIDE * NSA_HEAD_DIM)
    w1r = w1.reshape(n_per, CMP_STRIDE * NSA_HEAD_DIM, CMP_HIDDEN)
    hid = pe.reshape(-1) @ w1
    for m in range(n_per):
        hid = hid + jnp.einsum('bngc,ch->bngh', sub[:, m:m + n_cmp], w1r[m])
    return jax.nn.silu(hid) @ w2


def slc_attend(q, pos, kb, vb, idx, valid):
    b, lq, g, r, d = q.shape
    k = idx.shape[-1]
    ii = jnp.moveaxis(idx, 2, 1).reshape(b, g, lq * k)
    bi = jnp.arange(b)[:, None, None]
    gi = jnp.arange(g)[None, :, None]
    kg = kb[bi, gi, ii].reshape(b, g, lq, k, SLC_BLOCK, d)
    vg = vb[bi, gi, ii].reshape(b, g, lq, k, SLC_BLOCK, d)
    s = jnp.einsum('bqgrd,bgqksd->bqgrks', q, kg, preferred_element_type=F32) * (d ** -0.5)
    kpos = idx[..., None] * SLC_BLOCK + jnp.arange(SLC_BLOCK)
    mask = ((kpos <= pos[None, :, None, None, None]) & valid[..., None])[:, :, :, None]
    s = jnp.where(mask, s, NEG).reshape(b, lq, g, r, k * SLC_BLOCK)
    p = jax.nn.softmax(s, axis=-1).reshape(b, lq, g, r, k, SLC_BLOCK)
    return jnp.einsum('bqgrks,bgqksd->bqgrd', p.astype(vg.dtype), vg)


def nsa_cmp_slc(q, pos, rows, w1, w2, pe):
    b, l, g, r, d = q.shape
    t = rows.shape[1]
    kc = compress_rows(rows[:, :, 0], w1[0], w2[0], pe[0])
    vc = compress_rows(rows[:, :, 1], w1[1], w2[1], pe[1])
    n_cmp = kc.shape[1]
    cmp_last = jnp.arange(n_cmp) * CMP_STRIDE + (CMP_BLOCK - 1)
    o_cmp, p_cmp = gqa_attend(q, kc, vc, cmp_last[None, :] <= pos[:, None])
    n_slc = -(-t // SLC_BLOCK)
    c0 = np.arange(n_cmp)[:, None] * CMP_STRIDE
    s0 = np.arange(n_slc)[None, :] * SLC_BLOCK
    cover = np.clip(np.minimum(c0 + CMP_BLOCK, s0 + SLC_BLOCK) - np.maximum(c0, s0), 0, None) / CMP_BLOCK
    score = jnp.einsum('blgrn,nj->blgj', p_cmp, jnp.asarray(cover, F32))
    blk = jnp.arange(n_slc)[None, :]
    cur = (pos // SLC_BLOCK)[:, None]
    causal = blk * SLC_BLOCK <= pos[:, None]
    forced = (blk == 0) | (blk == cur) | (blk == cur - 1)
    score = jnp.where(forced[None, :, None, :], FORCE_SCORE, jnp.where(causal[None, :, None, :], score, -1.0))
    vals, idx = lax.top_k(score, min(N_SELECT, n_slc))
    valid = vals >= 0.0
    kv = jnp.pad(rows[:, :, 2:4], ((0, 0), (0, n_slc * SLC_BLOCK - t), (0, 0), (0, 0), (0, 0)))
    kv = jnp.transpose(kv.reshape(b, n_slc, SLC_BLOCK, 2, g, d), (3, 0, 4, 1, 2, 5))
    kb, vb = kv[0], kv[1]
    qb = math.gcd(l, SLC_QBLOCK)
    nb = l // qb

    def blocks(a):
        return jnp.moveaxis(a.reshape((b, nb, qb) + a.shape[2:]), 1, 0)

    o_slc = lax.map(lambda a: slc_attend(a[0], a[1], kb, vb, a[2], a[3]),
                    (blocks(q), pos.reshape(nb, qb), blocks(idx), blocks(valid)))
    o_slc = jnp.moveaxis(o_slc, 0, 1).reshape(b, l, g, r, d)
    return o_cmp, o_slc


def swa_buffered(q, pos, kv_new, buf):
    wb = buf.shape[1]
    keys = jnp.concatenate([buf.astype(kv_new.dtype), kv_new], axis=1)
    kpos = PAST_LEN - wb + jnp.arange(keys.shape[1])
    mask = (kpos[None, :] <= pos[:, None]) & (pos[:, None] - kpos[None, :] < WINDOW)
    o, _ = gqa_attend(q, keys[:, :, 0], keys[:, :, 1], mask)
    return o, keys[:, keys.shape[1] - wb:]


def gla_chunked(q, k, v, logf, s0):
    b, l = q.shape[:2]
    c = min(HG_CHUNK, l)
    mid = (c - 1) // 2
    tril = jnp.tril(jnp.ones((c, c), bool))

    def step(s, inp):
        qc, kc, vc, gc = inp
        bc = jnp.cumsum(gc, axis=1)
        bm = bc[:, mid:mid + 1]
        a = jnp.einsum('bthd,bshd->bhts', qc * jnp.exp(bc - bm), kc * jnp.exp(bm - bc))
        a = jnp.where(tril, a, 0.0)
        o = jnp.einsum('bhts,bshv->bthv', a, vc) + jnp.einsum('bthd,bhdv->bthv', qc * jnp.exp(bc), s)
        bl = bc[:, -1]
        s = jnp.exp(bl)[..., None] * s + jnp.einsum('bshd,bshv->bhdv', kc * jnp.exp(bl[:, None] - bc), vc)
        return s, o

    s, o = lax.scan(step, s0.astype(F32), (to_chunks(q, c), to_chunks(k, c), to_chunks(v, c), to_chunks(logf, c)))
    return from_chunks(o, l), s


def hgrn2(hq, hf, hi, hg, lb, s0, g_norm):
    b, l, _ = hq.shape
    q = jax.nn.silu(hq.astype(F32)).reshape(b, l, HG_HEADS, HG_DK)
    f = (lb + (1.0 - lb) * jax.nn.sigmoid(hf.astype(F32))).reshape(b, l, HG_HEADS, HG_DK)
    v = hi.astype(F32).reshape(b, l, HG_HEADS, HG_DV)
    o, s = gla_chunked(q, 1.0 - f, v, jnp.log(f), s0)
    o = o * lax.rsqrt(jnp.mean(o * o, axis=-1, keepdims=True) + EPS)
    o = o.reshape(b, l, HG_V_W) * g_norm.astype(F32) * jax.nn.silu(hg.astype(F32))
    return o, s


def nsa_prompt_branch(proj, b, l, w1, w2, pe):
    cmp = _compress_prompt(proj, w1, w2, pe, b, l)
    g = proj[:, AB_GATE_COL:AB_GATE_COL + NSA_GATE_W]
    gates = jax.nn.sigmoid(g).reshape(b * l, NSA_KV_HEADS, 3 * NSA_GROUP)
    gates = jnp.pad(gates, ((0, 0), (0, 0), (0, LANE - 3 * NSA_GROUP))).reshape(b * l, NSA_KV_HEADS * LANE)
    return _nsa_prompt(proj, cmp[0], cmp[1], gates, b, l)


AB_HG_COL = NSA_Q + NSA_KV_W
AB_GATE_COL = AB_HG_COL + 2 * HG_QF_W + 2 * HG_V_W
AB_PACKED = -(-(AB_GATE_COL + NSA_GATE_W) // LANE) * LANE


def pack_ab_in(w):
    g0 = NSA_Q + NSA_KV_W
    pad = jnp.zeros((w.shape[0], AB_PACKED - w.shape[1]), BF16)
    return jnp.concatenate([w[:, :g0].astype(BF16), w[:, g0 + NSA_GATE_W:].astype(BF16),
                            w[:, g0:g0 + NSA_GATE_W].astype(BF16), pad], axis=1)


def ab_mixer_prompt(x, h, b, l, hg_state, lb, w_in, w1, w2, pe, hg_g, w_out):
    proj = _mm(h, w_in)
    o_nsa = nsa_prompt_branch(proj, b, l, w1, w2, pe)
    o_hg, s_new = _hgrn_prompt(proj, AB_HG_COL, lb[None], hg_g[None], hg_state, b, l)
    x = _mm([o_nsa, o_hg], w_out, res=x)
    kv = proj[:, NSA_Q:NSA_Q + NSA_KV_W].reshape(b, l, 6, NSA_KV_HEADS, NSA_HEAD_DIM)
    return x, kv[:, :, :4], kv[:, l - min(WINDOW, l):, 4:], s_new


def ab_mixer_sample(x, h, b, l, pos, past_rows, swa_buf, hg_state, lb, w_in, w1, w2, pe, hg_g, w_out):
    proj = _mm(h, w_in).reshape(b, l, AB_PACKED)
    q, kv, hq, hf, hi, hgate, g = split_cols(
        proj, (NSA_Q, NSA_KV_W, HG_QF_W, HG_QF_W, HG_V_W, HG_V_W, NSA_GATE_W))
    kv = kv.reshape(b, l, 6, NSA_KV_HEADS, NSA_HEAD_DIM)
    q = q.reshape(b, l, NSA_KV_HEADS, NSA_GROUP, NSA_HEAD_DIM)
    rows = jnp.concatenate([past_rows.astype(kv.dtype), kv[:, :, :4]], axis=1)
    o_cmp, o_slc = nsa_cmp_slc(q, pos, rows, w1, w2, pe)
    o_swa, new_buf = swa_buffered(q, pos, kv[:, :, 4:], swa_buf)
    gates = jax.nn.sigmoid(g).reshape(b, l, NSA_KV_HEADS, NSA_GROUP, 3)
    o_nsa = gates[..., 0:1] * o_cmp + gates[..., 1:2] * o_slc + gates[..., 2:3] * o_swa
    o_hg, s_new = hgrn2(hq, hf, hi, hgate, lb, hg_state, hg_g)
    mix = jnp.concatenate([o_nsa.reshape(b * l, NSA_Q), o_hg.reshape(b * l, HG_V_W)], axis=-1).astype(BF16)
    return _mm(mix, w_out, res=x), kv[:, :, :4], new_buf, s_new


def ssd_chunked(x, dt, a, bm, cm, s0):
    b, l, nh, p = x.shape
    g, n = bm.shape[2], bm.shape[3]
    r = nh // g
    c = min(SSM_CHUNK, l)
    tril = jnp.tril(jnp.ones((c, c), bool))

    def step(s, inp):
        xc, dtc, bc, cc = inp
        cum = jnp.cumsum(dtc * a, axis=1)
        seg = cum[:, :, None, :] - cum[:, None, :, :]
        lm = jnp.exp(jnp.where(tril[None, :, :, None], seg, -jnp.inf)).reshape(b, c, c, g, r)
        xdt = (xc * dtc[..., None]).reshape(b, c, g, r, p)
        cb = jnp.einsum('btgn,bsgn->btsg', cc, bc)
        sg = s.reshape(b, g, r, p, n)
        y = jnp.einsum('btsg,btsgr,bsgrp->btgrp', cb, lm, xdt)
        y = y + jnp.einsum('btgn,bgrpn->btgrp', cc, sg) * jnp.exp(cum).reshape(b, c, g, r)[..., None]
        dec = jnp.exp(cum[:, -1:] - cum).reshape(b, c, g, r)
        sg = jnp.exp(cum[:, -1]).reshape(b, g, r)[..., None, None] * sg + jnp.einsum('bsgn,bsgrp->bgrpn', bc, xdt * dec[..., None])
        return sg.reshape(b, nh, p, n), y.reshape(b, c, nh, p)

    s, y = lax.scan(step, s0.astype(F32), (to_chunks(x, c), to_chunks(dt, c), to_chunks(bm, c), to_chunks(cm, c)))
    return from_chunks(y, l), s


def mamba_prompt(x, h, b, l, w_in, conv_w, conv_b, dt_bias, a_log, d_skip, norm_g, w_out):
    proj = _mm(h, w_in)
    y, s = _ssd_prompt(proj, jnp.zeros((b, SSM_CONV - 1, SSM_CONV_DIM), F32),
                       jnp.zeros((b, SSM_HEADS * SSM_HEAD_DIM, SSM_STATE), F32),
                       conv_w, conv_b, dt_bias, a_log, d_skip, norm_g, b, l)
    assert l >= SSM_CONV - 1
    new_conv = proj.reshape(b, l, -1)[:, l - (SSM_CONV - 1):, SSM_D_INNER:SSM_D_INNER + SSM_CONV_DIM]
    return _mm(y, w_out, res=x), new_conv, s.reshape(b, SSM_HEADS, SSM_HEAD_DIM, SSM_STATE)


def mamba_sample(x, h, b, l, conv_state, ssm_state, w_in, conv_w, conv_b, dt_bias, a_log, d_skip, norm_g, w_out):
    z, xbc, dt = split_cols(_mm(h, w_in).reshape(b, l, -1), (SSM_D_INNER, SSM_CONV_DIM, SSM_HEADS))
    xpad = jnp.concatenate([conv_state.astype(xbc.dtype), xbc], axis=1)
    acc = conv_b.astype(F32)
    for j in range(SSM_CONV):
        acc = acc + xpad[:, j:j + l].astype(F32) * conv_w[j].astype(F32)
    xbc = jax.nn.silu(acc)
    new_conv = xpad[:, xpad.shape[1] - (SSM_CONV - 1):]
    xs, bm, cm = split_cols(xbc, (SSM_D_INNER, SSM_GROUPS * SSM_STATE, SSM_GROUPS * SSM_STATE))
    xs = xs.reshape(b, l, SSM_HEADS, SSM_HEAD_DIM)
    bm = bm.reshape(b, l, SSM_GROUPS, SSM_STATE)
    cm = cm.reshape(b, l, SSM_GROUPS, SSM_STATE)
    dt = jax.nn.softplus(dt.astype(F32) + dt_bias.astype(F32))
    a = -jnp.exp(a_log.astype(F32))
    y, s = ssd_chunked(xs, dt, a, bm, cm, ssm_state)
    y = y + d_skip.astype(F32)[:, None] * xs
    y = (y.reshape(b, l, SSM_D_INNER) * jax.nn.silu(z.astype(F32))).reshape(b, l, SSM_GROUPS, SSM_D_INNER // SSM_GROUPS)
    y = (y * lax.rsqrt(jnp.mean(y * y, axis=-1, keepdims=True) + EPS)).reshape(b, l, SSM_D_INNER) * norm_g.astype(F32)
    return _mm(y.reshape(b * l, SSM_D_INNER).astype(BF16), w_out, res=x), new_conv, s


def ffn_ple(x, p, g_ffn, w_ffn_in, w_ffn_out, g_ple, w_ple_gate, w_ple_up):
    x = _mm(_mm_swiglu(_rms(x, g_ffn, BF16), w_ffn_in), w_ffn_out, res=x)
    return _mm_ple(_rms(x, g_ple, BF16), w_ple_gate, p, w_ple_up, x)


def kernel(x_prompt, x_sample, cache_nsa_kv, cache_swa_kv, state_hgrn, state_ssm, cache_conv, page_table, p_prompt, p_sample, norm_mix, norm_ffn, w_ab_in, w_cmp1, w_cmp2, cmp_pe, hg_lb_logits, hg_norm, w_ab_out, w_ssm_in, ssm_conv_w, ssm_conv_b, ssm_dt_bias, ssm_a_log, ssm_d, ssm_norm, w_ssm_out, w_ffn_in, w_ffn_out, w_ple_up, w_ple_gate, norm_ple, norm_final):
    depth = norm_mix.shape[0]
    bp, lp, d = x_prompt.shape
    bs, ls, _ = x_sample.shape
    n_pages = page_table.shape[1]
    pos_s = PAST_LEN + jnp.arange(ls)
    lb_all = jnp.cumsum(jax.nn.softmax(hg_lb_logits.astype(F32), axis=0), axis=0)
    xp = x_prompt.reshape(bp * lp, d)
    xs = x_sample.reshape(bs * ls, d)
    nsa_p, nsa_s, swa_p, swa_s, hg_p, hg_s, ssm_p, ssm_s, cv_p, cv_s = [], [], [], [], [], [], [], [], [], []
    for i in range(depth):
        hp = _rms(xp, norm_mix[i], BF16)
        hs = _rms(xs, norm_mix[i], BF16)
        if i % 2 == 0:
            a = i // 2
            wa = (lb_all[a], pack_ab_in(w_ab_in[a]), w_cmp1[a], w_cmp2[a], cmp_pe[a], hg_norm[a], w_ab_out[a])
            xp, r_p, b_p, s_p = ab_mixer_prompt(xp, hp, bp, lp, jnp.zeros((bp, HG_HEADS, HG_DK, HG_DV), F32), *wa)
            past = cache_nsa_kv[a][page_table].reshape((bs, n_pages * PAGE_SIZE) + cache_nsa_kv.shape[3:])
            xs, r_s, b_s, s_s = ab_mixer_sample(xs, hs, bs, ls, pos_s, past, cache_swa_kv[a], state_hgrn[a], *wa)
            nsa_p.append(r_p)
            nsa_s.append(r_s)
            swa_p.append(b_p)
            swa_s.append(b_s)
            hg_p.append(s_p)
            hg_s.append(s_s)
        else:
            c = i // 2
            wc = (w_ssm_in[c], ssm_conv_w[c], ssm_conv_b[c], ssm_dt_bias[c], ssm_a_log[c], ssm_d[c],
                  ssm_norm[c], w_ssm_out[c].astype(BF16))
            xp, c_p, t_p = mamba_prompt(xp, hp, bp, lp, *wc)
            xs, c_s, t_s = mamba_sample(xs, hs, bs, ls, cache_conv[c], state_ssm[c], *wc)
            cv_p.append(c_p)
            cv_s.append(c_s)
            ssm_p.append(t_p)
            ssm_s.append(t_s)
        wf = (norm_ffn[i], w_ffn_in[i], w_ffn_out[i].astype(BF16), norm_ple[i],
              w_ple_gate[i], w_ple_up[i])
        xp = ffn_ple(xp, p_prompt[i].reshape(bp * lp, PLE_DIM).astype(BF16), *wf)
        xs = ffn_ple(xs, p_sample[i].reshape(bs * ls, PLE_DIM).astype(BF16), *wf)
    y_prompt = _rms(xp, norm_final, F32).reshape(bp, lp, d)
    y_sample = _rms(xs, norm_final, F32).reshape(bs, ls, d)
    return (y_prompt, y_sample, jnp.stack(nsa_p), jnp.stack(nsa_s), jnp.stack(swa_p), jnp.stack(swa_s),
            jnp.stack(hg_p), jnp.stack(hg_s), jnp.stack(ssm_p), jnp.stack(ssm_s), jnp.stack(cv_p), jnp.stack(cv_s))
```

```python
import functools
import math

import jax
import jax.numpy as jnp
import numpy as np
from jax import lax
from jax.experimental import pallas as pl
from jax.experimental.pallas import tpu as pltpu

D_MODEL = 4096
PAST_LEN = 16384
PAGE_SIZE = 128
PLE_DIM = 256
NSA_HEADS = 16
NSA_KV_HEADS = 4
NSA_GROUP = NSA_HEADS // NSA_KV_HEADS
NSA_HEAD_DIM = 128
NSA_Q = NSA_HEADS * NSA_HEAD_DIM
NSA_KV_W = 6 * NSA_KV_HEADS * NSA_HEAD_DIM
NSA_GATE_W = 3 * NSA_HEADS
CMP_BLOCK = 32
CMP_STRIDE = 16
CMP_HIDDEN = 256
SLC_BLOCK = 64
N_SELECT = 16
WINDOW = 512
SLC_QBLOCK = 32
SWA_QBLOCK = 128
FORCE_SCORE = 1e4
HG_HEADS = 16
HG_DK = 128
HG_DV = (D_MODEL // 2) // HG_HEADS
HG_QF_W = HG_HEADS * HG_DK
HG_V_W = HG_HEADS * HG_DV
HG_CHUNK = 32
AB_SIZES = (NSA_Q, NSA_KV_W, NSA_GATE_W, HG_QF_W, HG_QF_W, HG_V_W, HG_V_W)
SSM_D_INNER = 2 * D_MODEL
SSM_HEAD_DIM = 64
SSM_HEADS = SSM_D_INNER // SSM_HEAD_DIM
SSM_GROUPS = 8
SSM_STATE = 128
SSM_CONV = 4
SSM_CONV_DIM = SSM_D_INNER + 2 * SSM_GROUPS * SSM_STATE
SSM_CHUNK = 64
D_FF = ((8 * D_MODEL + 3 * 256 - 1) // (3 * 256)) * 256
EPS = 1e-6
NEG = -1e30

LANE = 128
VMEM_LIMIT_BYTES = 56 * 1024 * 1024
BF16 = jnp.bfloat16
F32 = jnp.float32


MAX_ROW_TILE = 1024
NORM_ROW_TILE = 512


def _rms_kernel(x_ref, g_ref, o_ref):
    x = x_ref[...]
    y = x * lax.rsqrt(jnp.mean(x * x, axis=-1, keepdims=True) + EPS)
    o_ref[...] = (y * g_ref[...]).astype(o_ref.dtype)


def _rms(x, g, out_dtype):
    m, d = x.shape
    tm = min(m, NORM_ROW_TILE)
    assert m % tm == 0
    return pl.pallas_call(
        _rms_kernel,
        grid=(m // tm,),
        in_specs=[pl.BlockSpec((tm, d), lambda i: (i, 0)), pl.BlockSpec((1, d), lambda i: (0, 0))],
        out_specs=pl.BlockSpec((tm, d), lambda i: (i, 0)),
        out_shape=jax.ShapeDtypeStruct((m, d), out_dtype),
        compiler_params=pltpu.CompilerParams(dimension_semantics=("arbitrary",),
                                             vmem_limit_bytes=VMEM_LIMIT_BYTES),
        name="rms",
    )(x, g.astype(F32)[None])


def _mm_kernel(*refs, n_lhs, residual):
    x_refs = refs[:n_lhs]
    w_ref = refs[n_lhs]
    o_ref = refs[-1]
    k = pl.program_id(2)

    def first(x_ref):
        acc = jnp.dot(x_ref[...], w_ref[...].astype(BF16), preferred_element_type=F32)
        o_ref[...] = acc + refs[n_lhs + 1][...] if residual else acc

    def later(x_ref):
        o_ref[...] += jnp.dot(x_ref[...], w_ref[...].astype(BF16), preferred_element_type=F32)

    pl.when(k == 0)(functools.partial(first, x_refs[0]))
    if n_lhs == 1:
        pl.when(k > 0)(functools.partial(later, x_refs[0]))
    else:
        for p in range(1, n_lhs):
            pl.when(k == p)(functools.partial(later, x_refs[p]))


def _w_tile_bytes(w):
    return 2 * w.dtype.itemsize + (2 if w.dtype != BF16 else 0)


def _k_tile(k, x_bytes_per_k, w_bytes_per_k, budget):
    units = k // LANE
    for parts in range(1, units + 1):
        if units % parts == 0 and (x_bytes_per_k + w_bytes_per_k) * (units // parts) * LANE <= budget:
            return (units // parts) * LANE
    raise ValueError(f"no K tile for {k}")


def _mm(xs, w, res=None):
    xs = list(xs) if isinstance(xs, (list, tuple)) else [xs]
    m = xs[0].shape[0]
    k, n = w.shape
    tm = min(m, MAX_ROW_TILE)
    n_out_bufs = 4 if res is not None else 2
    widest = 1024 if w.dtype == BF16 else 512
    for tn in (min(n, widest), min(n, widest // 2)):
        budget = VMEM_LIMIT_BYTES - n_out_bufs * tm * tn * 4 - (4 << 20)
        x_per_k = 2 * 2 * tm * len(xs)
        w_per_k = _w_tile_bytes(w) * tn
        tk = _k_tile(k, x_per_k, w_per_k, budget) if len(xs) == 1 else k // len(xs)
        if tk == k or tk >= 2048:
            break
    if len(xs) == 1:
        x_specs = [pl.BlockSpec((tm, tk), lambda i, j, kk: (i, kk))]
    else:
        assert all(x.shape[1] == tk for x in xs) and (x_per_k + w_per_k) * tk <= budget
        x_specs = [pl.BlockSpec((tm, tk), lambda i, j, kk: (i, 0)) for _ in xs]
    assert m % tm == 0 and k % tk == 0 and tk % LANE == 0
    tile = pl.BlockSpec((tm, tn), lambda i, j, kk: (i, j))
    return pl.pallas_call(
        functools.partial(_mm_kernel, n_lhs=len(xs), residual=res is not None),
        grid=(m // tm, pl.cdiv(n, tn), k // tk),
        in_specs=x_specs + [pl.BlockSpec((tk, tn), lambda i, j, kk: (kk, j))] + ([tile] if res is not None else []),
        out_specs=tile,
        out_shape=jax.ShapeDtypeStruct((m, n), F32),
        compiler_params=pltpu.CompilerParams(
            dimension_semantics=("arbitrary", "arbitrary", "arbitrary"),
            vmem_limit_bytes=VMEM_LIMIT_BYTES),
        name="mm",
    )(*xs, w, *([res] if res is not None else []))


FFN_TILE = 256


def _swiglu_kernel(x_ref, wg_ref, wu_ref, o_ref):
    x = x_ref[...]
    gate = jnp.dot(x, wg_ref[...].astype(BF16), preferred_element_type=F32)
    up = jnp.dot(x, wu_ref[...].astype(BF16), preferred_element_type=F32)
    o_ref[...] = (gate * jax.nn.sigmoid(gate) * up).astype(o_ref.dtype)


def _mm_swiglu(x, w):
    m, k = x.shape
    hidden = w.shape[1] // 2
    tm = min(m, MAX_ROW_TILE)
    nj = hidden // FFN_TILE
    assert m % tm == 0 and hidden % FFN_TILE == 0 and w.shape[0] == k
    return pl.pallas_call(
        _swiglu_kernel,
        grid=(m // tm, nj),
        in_specs=[pl.BlockSpec((tm, k), lambda i, j: (i, 0)),
                  pl.BlockSpec((k, FFN_TILE), lambda i, j: (0, j)),
                  pl.BlockSpec((k, FFN_TILE), lambda i, j: (0, nj + j))],
        out_specs=pl.BlockSpec((tm, FFN_TILE), lambda i, j: (i, j)),
        out_shape=jax.ShapeDtypeStruct((m, hidden), BF16),
        compiler_params=pltpu.CompilerParams(dimension_semantics=("arbitrary", "arbitrary"),
                                             vmem_limit_bytes=VMEM_LIMIT_BYTES),
        name="mm_swiglu",
    )(x, w, w)


PLE_COL_TILE = 512


def _ple_kernel(t_ref, wg_ref, p_ref, wu_ref, res_ref, o_ref):
    gate = jnp.dot(t_ref[...], wg_ref[...].astype(BF16), preferred_element_type=F32)
    up = jnp.dot(p_ref[...], wu_ref[...].astype(BF16), preferred_element_type=F32)
    o_ref[...] = res_ref[...] + up * jax.nn.sigmoid(gate)


def _mm_ple(t, w_gate, p, w_up, res):
    m, k = t.shape
    n = w_gate.shape[1]
    kp = p.shape[1]
    tm = min(m, MAX_ROW_TILE)
    tn = PLE_COL_TILE
    assert m % tm == 0 and n % tn == 0
    tile = pl.BlockSpec((tm, tn), lambda i, j: (i, j))
    return pl.pallas_call(
        _ple_kernel,
        grid=(m // tm, n // tn),
        in_specs=[pl.BlockSpec((tm, k), lambda i, j: (i, 0)), pl.BlockSpec((k, tn), lambda i, j: (0, j)),
                  pl.BlockSpec((tm, kp), lambda i, j: (i, 0)), pl.BlockSpec((kp, tn), lambda i, j: (0, j)), tile],
        out_specs=tile,
        out_shape=jax.ShapeDtypeStruct((m, n), F32),
        compiler_params=pltpu.CompilerParams(dimension_semantics=("arbitrary", "arbitrary"),
                                             vmem_limit_bytes=VMEM_LIMIT_BYTES),
        name="mm_ple",
    )(t, w_gate, p, w_up, res)


NSA_TQ = 256
NSA_KEY_BUCKET = 512


def _dot_nt(a, b):
    return lax.dot_general(a, b, (((1,), (1,)), ((), ())), preferred_element_type=F32)


def _group_attend(q4, k, v, mask, scale, tq):
    s4 = _dot_nt(q4, k) * scale
    es, dens = [], []
    for r in range(NSA_GROUP):
        s = jnp.where(mask, s4[r * tq:(r + 1) * tq], NEG)
        e = jnp.exp(s - jnp.max(s, axis=-1, keepdims=True))
        dens.append(jnp.sum(e, axis=-1, keepdims=True))
        es.append(e.astype(BF16))
    o4 = jnp.dot(jnp.concatenate(es, axis=0), v, preferred_element_type=F32)
    return [o4[r * tq:(r + 1) * tq] / dens[r] for r in range(NSA_GROUP)]


def _nsa_prompt_kernel(q_ref, kc_ref, vc_ref, ks_ref, vs_ref, kw_ref, vw_ref, gate_ref, cover_ref, expand_ref,
                       o_ref, slc_s, *, seq):
    tq = NSA_TQ
    hd = NSA_HEAD_DIM
    n_cmp = seq // CMP_STRIDE - CMP_BLOCK // CMP_STRIDE + 1
    n_slc = seq // SLC_BLOCK
    span = WINDOW + tq
    scale = hd ** -0.5
    q0 = pl.program_id(2) * tq
    pos = q0 + lax.broadcasted_iota(jnp.int32, (tq, 1), 0)
    lane = lax.broadcasted_iota(jnp.int32, (tq, LANE), 1)
    q4 = jnp.concatenate([q_ref[:, r * hd:(r + 1) * hd].astype(BF16) for r in range(NSA_GROUP)], axis=0)

    cmp_ok = (lane * CMP_STRIDE + (CMP_BLOCK - 1) <= pos) & (lane < n_cmp)
    s4 = _dot_nt(q4, kc_ref[...]) * scale
    psum = jnp.zeros((tq, LANE), F32)
    ps = []
    for r in range(NSA_GROUP):
        s = jnp.where(cmp_ok, s4[r * tq:(r + 1) * tq], NEG)
        e = jnp.where(cmp_ok, jnp.exp(s - jnp.max(s, axis=-1, keepdims=True)), 0.0)
        den = jnp.sum(e, axis=-1, keepdims=True)
        p = e / jnp.where(den > 0.0, den, 1.0)
        psum = psum + p
        ps.append(p.astype(BF16))
    o_cmp4 = jnp.dot(jnp.concatenate(ps, axis=0), vc_ref[...], preferred_element_type=F32)

    score = jnp.dot(psum, cover_ref[...], preferred_element_type=F32, precision=lax.Precision.HIGHEST)
    cur = pos // SLC_BLOCK
    forced = (lane == 0) | (lane == cur) | (lane == cur - 1)
    causal = lane * SLC_BLOCK <= pos
    score = jnp.where(forced, FORCE_SCORE, jnp.where(causal, score, -1.0))
    score = jnp.where(lane < n_slc, score, -2.0)
    rank = jnp.zeros((tq, LANE), jnp.int32)
    for i in range(n_slc):
        ci = score[:, i:i + 1]
        beats = (ci > score) | ((ci == score) & (lane > i))
        rank = rank + beats.astype(jnp.int32)
    sel = ((rank < min(N_SELECT, n_slc)) & (lane < n_slc)).astype(BF16)

    def selected(nk):
        sel_keys = jnp.dot(sel, expand_ref[:, :nk], preferred_element_type=F32)
        kpos = lax.broadcasted_iota(jnp.int32, (tq, nk), 1)
        outs = _group_attend(q4, ks_ref[:nk, :].astype(BF16), vs_ref[:nk, :].astype(BF16),
                             (sel_keys > 0.5) & (kpos <= pos), scale, tq)
        for r in range(NSA_GROUP):
            slc_s[r] = outs[r]

    n_bucket = (q0 + tq + NSA_KEY_BUCKET - 1) // NSA_KEY_BUCKET
    for nb in range(1, seq // NSA_KEY_BUCKET + 1):
        pl.when(n_bucket == nb)(functools.partial(selected, nb * NSA_KEY_BUCKET))

    w0 = pl.multiple_of(jnp.maximum(q0 - WINDOW, 0), tq)
    wpos = w0 + lax.broadcasted_iota(jnp.int32, (tq, span), 1)
    o_swa = _group_attend(q4, kw_ref[pl.ds(w0, span), :].astype(BF16), vw_ref[pl.ds(w0, span), :].astype(BF16),
                          (wpos <= pos) & (pos - wpos < WINDOW), scale, tq)

    for r in range(NSA_GROUP):
        g = gate_ref[:, 3 * r:3 * r + 3]
        o = g[:, 0:1] * o_cmp4[r * tq:(r + 1) * tq] + g[:, 1:2] * slc_s[r] + g[:, 2:3] * o_swa[r]
        o_ref[:, r * hd:(r + 1) * hd] = o.astype(o_ref.dtype)


def _nsa_prompt(proj, kc, vc, gates, batch, seq):
    assert seq % NSA_TQ == 0 and seq // SLC_BLOCK <= LANE and seq // CMP_STRIDE <= LANE + 1
    assert seq % NSA_KEY_BUCKET == 0 and seq >= WINDOW + NSA_TQ and WINDOW % NSA_TQ == 0
    nt = seq // NSA_TQ
    n_cmp = seq // CMP_STRIDE - CMP_BLOCK // CMP_STRIDE + 1
    n_slc = seq // SLC_BLOCK
    c0 = np.arange(LANE)[:, None] * CMP_STRIDE
    s0 = np.arange(LANE)[None, :] * SLC_BLOCK
    cover = np.clip(np.minimum(c0 + CMP_BLOCK, s0 + SLC_BLOCK) - np.maximum(c0, s0), 0, None) / CMP_BLOCK
    cover = cover * (np.arange(LANE)[:, None] < n_cmp) * (np.arange(LANE)[None, :] < n_slc)
    expand = (np.arange(seq)[None, :] // SLC_BLOCK == np.arange(LANE)[:, None])
    hd = NSA_HEAD_DIM
    gw = NSA_GROUP * hd

    def kv_spec(slot):
        return pl.BlockSpec((seq, hd), lambda b, g, t: (b, NSA_Q // hd + slot * NSA_KV_HEADS + g))

    cmp_spec = pl.BlockSpec((None, None, LANE, hd), lambda b, g, t: (b, g, 0, 0))
    return pl.pallas_call(
        functools.partial(_nsa_prompt_kernel, seq=seq),
        grid=(batch, NSA_KV_HEADS, nt),
        in_specs=[pl.BlockSpec((NSA_TQ, gw), lambda b, g, t: (b * nt + t, g)),
                  cmp_spec, cmp_spec, kv_spec(2), kv_spec(3), kv_spec(4), kv_spec(5),
                  pl.BlockSpec((NSA_TQ, LANE), lambda b, g, t: (b * nt + t, g)),
                  pl.BlockSpec((LANE, LANE), lambda b, g, t: (0, 0)),
                  pl.BlockSpec((LANE, seq), lambda b, g, t: (0, 0))],
        out_specs=pl.BlockSpec((NSA_TQ, gw), lambda b, g, t: (b * nt + t, g)),
        out_shape=jax.ShapeDtypeStruct((batch * seq, NSA_Q), BF16),
        scratch_shapes=[pltpu.VMEM((NSA_GROUP, NSA_TQ, hd), F32)],
        compiler_params=pltpu.CompilerParams(
            dimension_semantics=("arbitrary", "arbitrary", "arbitrary"),
            vmem_limit_bytes=VMEM_LIMIT_BYTES),
        name="nsa_prompt",
    )(proj, kc, vc, proj, proj, proj, proj, gates, jnp.asarray(cover, F32), jnp.asarray(expand, BF16))


def _compress_kernel(x_ref, w1_ref, w2_ref, pe_ref, o_ref, *, seq):
    n_sub = seq // CMP_STRIDE
    n_cmp = n_sub - CMP_BLOCK // CMP_STRIDE + 1
    acc = jnp.zeros((n_sub, 2 * CMP_HIDDEN), F32)
    for r in range(CMP_STRIDE):
        xr = x_ref[pl.ds(r, n_sub, stride=CMP_STRIDE), :].astype(BF16)
        acc = acc + jnp.dot(xr, w1_ref[r], preferred_element_type=F32)
    hid = acc[:, :CMP_HIDDEN] + pltpu.roll(acc[:, CMP_HIDDEN:], n_sub - 1, 0) + pe_ref[...]
    c = jnp.dot((hid * jax.nn.sigmoid(hid)).astype(BF16), w2_ref[...], preferred_element_type=F32)
    row = lax.broadcasted_iota(jnp.int32, c.shape, 0)
    o_ref[...] = jnp.where(row < n_cmp, c, 0.0).astype(o_ref.dtype)


def _compress_prompt(proj, w1, w2, pe, batch, seq):
    assert CMP_BLOCK == 2 * CMP_STRIDE and seq // CMP_STRIDE == LANE and NSA_HEAD_DIM == LANE
    hd = NSA_HEAD_DIM
    w1p = w1.reshape(2, 2, CMP_STRIDE, hd, CMP_HIDDEN).transpose(0, 2, 3, 1, 4)
    w1p = w1p.reshape(2, CMP_STRIDE, hd, 2 * CMP_HIDDEN).astype(BF16)
    pe_h = jnp.einsum('sc,sch->sh', pe.reshape(2, -1), w1)[:, None, :]
    return pl.pallas_call(
        functools.partial(_compress_kernel, seq=seq),
        grid=(batch, 2, NSA_KV_HEADS),
        in_specs=[pl.BlockSpec((seq, hd), lambda b, s, g: (b, NSA_Q // hd + s * NSA_KV_HEADS + g)),
                  pl.BlockSpec((None, CMP_STRIDE, hd, 2 * CMP_HIDDEN), lambda b, s, g: (s, 0, 0, 0)),
                  pl.BlockSpec((None, CMP_HIDDEN, hd), lambda b, s, g: (s, 0, 0)),
                  pl.BlockSpec((None, 1, CMP_HIDDEN), lambda b, s, g: (s, 0, 0))],
        out_specs=pl.BlockSpec((None, None, None, LANE, hd), lambda b, s, g: (s, b, g, 0, 0)),
        out_shape=jax.ShapeDtypeStruct((2, batch, NSA_KV_HEADS, LANE, hd), BF16),
        compiler_params=pltpu.CompilerParams(dimension_semantics=("arbitrary", "arbitrary", "arbitrary"),
                                             vmem_limit_bytes=VMEM_LIMIT_BYTES),
        name="compress_prompt",
    )(proj, w1p, w2.astype(BF16), pe_h)


CMP_PAGES = 8
DEC_QROWS = 8


def _compress_paged_kernel(pt_ref, *refs, n_cmp):
    pages = refs[:CMP_PAGES + 1]
    perm_ref, w1_ref, w2_ref, pe_ref, o_ref = refs[CMP_PAGES + 1:]
    hd = NSA_HEAD_DIM
    sub_pp = PAGE_SIZE // CMP_STRIDE
    own = CMP_PAGES * sub_pp
    tot = own + sub_pp
    first = pl.program_id(1) * own
    perm = perm_ref[...]
    xp = [jnp.dot(perm, page[...].astype(BF16), preferred_element_type=F32) for page in pages]
    for slot in range(2):
        acc = jnp.zeros((NSA_KV_HEADS * tot, 2 * CMP_HIDDEN), F32)
        for r in range(CMP_STRIDE):
            cols = [slice((slot * NSA_KV_HEADS + g) * hd, (slot * NSA_KV_HEADS + g + 1) * hd)
                    for g in range(NSA_KV_HEADS)]
            xr = jnp.concatenate([x[r * sub_pp:(r + 1) * sub_pp, c] for c in cols for x in xp],
                                 axis=0).astype(BF16)
            acc = acc + jnp.dot(xr, w1_ref[slot, r], preferred_element_type=F32)
        for g in range(NSA_KV_HEADS):
            blk = acc[g * tot:(g + 1) * tot]
            hid = (blk[:, :CMP_HIDDEN] + pltpu.roll(blk[:, CMP_HIDDEN:], tot - 1, 0))[:own] + pe_ref[slot]
            c = jnp.dot((hid * jax.nn.sigmoid(hid)).astype(BF16), w2_ref[slot], preferred_element_type=F32)
            sub = first + lax.broadcasted_iota(jnp.int32, c.shape, 0)
            o_ref[slot, g] = jnp.where(sub < n_cmp, c, 0.0).astype(o_ref.dtype)


def _compress_paged(cache, page_table, w1, w2, pe):
    batch, n_pages = page_table.shape
    hd = NSA_HEAD_DIM
    sub_pp = PAGE_SIZE // CMP_STRIDE
    n_sub = n_pages * sub_pp
    assert n_pages % CMP_PAGES == 0 and PAGE_SIZE % CMP_STRIDE == 0 and CMP_BLOCK == 2 * CMP_STRIDE
    width = 2 * NSA_KV_HEADS * hd
    cache2 = cache.reshape(cache.shape[0], PAGE_SIZE, -1)
    w1p = w1.reshape(2, 2, CMP_STRIDE, hd, CMP_HIDDEN).transpose(0, 2, 3, 1, 4)
    w1p = w1p.reshape(2, CMP_STRIDE, hd, 2 * CMP_HIDDEN).astype(BF16)
    pe_h = jnp.einsum('sc,sch->sh', pe.reshape(2, -1), w1)[:, None, :]
    out_row = np.arange(PAGE_SIZE)
    perm = (out_row[None, :] == (out_row % sub_pp * CMP_STRIDE + out_row // sub_pp)[:, None])

    def page_spec(k):
        return pl.BlockSpec((None, PAGE_SIZE, width),
                            lambda b, s, pt: (pt[b, jnp.minimum(s * CMP_PAGES + k, n_pages - 1)], 0, 0))

    full = lambda shape: pl.BlockSpec(shape, lambda b, s, pt: (0,) * len(shape))
    return pl.pallas_call(
        functools.partial(_compress_paged_kernel, n_cmp=n_sub - 1),
        grid_spec=pltpu.PrefetchScalarGridSpec(
            num_scalar_prefetch=1,
            grid=(batch, n_pages // CMP_PAGES),
            in_specs=[page_spec(k) for k in range(CMP_PAGES + 1)] + [
                full((PAGE_SIZE, PAGE_SIZE)),
                full((2, CMP_STRIDE, hd, 2 * CMP_HIDDEN)), full((2, CMP_HIDDEN, hd)), full((2, 1, CMP_HIDDEN))],
            out_specs=pl.BlockSpec((2, None, NSA_KV_HEADS, CMP_PAGES * sub_pp, hd), lambda b, s, pt: (0, b, 0, s, 0))),
        out_shape=jax.ShapeDtypeStruct((2, batch, NSA_KV_HEADS, n_sub, hd), BF16),
        compiler_params=pltpu.CompilerParams(dimension_semantics=("arbitrary", "arbitrary"),
                                             vmem_limit_bytes=VMEM_LIMIT_BYTES),
        name="compress_paged",
    )(page_table, *([cache2] * (CMP_PAGES + 1)), jnp.asarray(perm, BF16), w1p, w2.astype(BF16), pe_h)


def _decode_select_kernel(q_ref, kc_ref, vc_ref, cover_ref, rsum_ref, ocmp_ref, idx_ref, *, n_cmp, n_past_blocks):
    nq = q_ref.shape[0]
    n_sub = kc_ref.shape[0]
    scale = NSA_HEAD_DIM ** -0.5
    pos = PAST_LEN + lax.broadcasted_iota(jnp.int32, (nq, 1), 0) // DEC_QROWS
    n = lax.broadcasted_iota(jnp.int32, (nq, n_sub), 1)
    ok = (n * CMP_STRIDE + (CMP_BLOCK - 1) <= pos) & (n < n_cmp)
    s = jnp.where(ok, _dot_nt(q_ref[...].astype(BF16), kc_ref[...]) * scale, NEG)
    e = jnp.where(ok, jnp.exp(s - jnp.max(s, axis=-1, keepdims=True)), 0.0)
    den = jnp.sum(e, axis=-1, keepdims=True)
    p = e / jnp.where(den > 0.0, den, 1.0)
    ocmp_ref[...] = jnp.dot(p.astype(BF16), vc_ref[...], preferred_element_type=F32)

    score = _dot_sel(_sel_dot(rsum_ref[...], p), cover_ref[...])
    post = PAST_LEN + lax.broadcasted_iota(jnp.int32, (score.shape[0], 1), 0)
    lane = lax.broadcasted_iota(jnp.int32, score.shape, 1)
    cur = post // SLC_BLOCK
    forced = (lane == 0) | (lane == cur) | (lane == cur - 1)
    score = jnp.where(forced, FORCE_SCORE, jnp.where(lane * SLC_BLOCK <= post, score, -1.0))
    rank = jnp.zeros(score.shape, jnp.int32)
    for i in range(n_past_blocks):
        ci = score[:, i:i + 1]
        rank = rank + ((ci > score) | ((ci == score) & (lane > i))).astype(jnp.int32)
    out_lane = lax.broadcasted_iota(jnp.int32, idx_ref.shape, 1)
    out = jnp.zeros(idx_ref.shape, jnp.int32)
    for k in range(N_SELECT - 1):
        idx_k = jnp.sum(jnp.where(rank == k, lane.astype(F32), 0.0), axis=-1, keepdims=True)
        out = jnp.where(out_lane == k, idx_k.astype(jnp.int32), out)
    idx_ref[...] = out


def _decode_select(q_pad, cmp, ls):
    batch = q_pad.shape[0]
    nq = ls * DEC_QROWS
    n_sub = cmp.shape[3]
    n_past_blocks = PAST_LEN // SLC_BLOCK
    assert PAST_LEN % SLC_BLOCK == 0 and ls <= min(SLC_BLOCK, 8) and n_past_blocks >= N_SELECT
    c0 = np.arange(n_sub)[:, None] * CMP_STRIDE
    s0 = np.arange(n_past_blocks)[None, :] * SLC_BLOCK
    cover = np.clip(np.minimum(c0 + CMP_BLOCK, s0 + SLC_BLOCK) - np.maximum(c0, s0), 0, None) / CMP_BLOCK
    cover = cover * (np.arange(n_sub)[:, None] < n_sub - 1)
    rows = np.arange(nq)[None, :]
    rsum = (rows // DEC_QROWS == np.arange(8)[:, None]) & (rows % DEC_QROWS < NSA_GROUP)
    hd = NSA_HEAD_DIM
    q_spec = pl.BlockSpec((None, None, nq, hd), lambda b, g: (b, g, 0, 0))
    return pl.pallas_call(
        functools.partial(_decode_select_kernel, n_cmp=n_sub - 1, n_past_blocks=n_past_blocks),
        grid=(batch, NSA_KV_HEADS),
        in_specs=[q_spec,
                  pl.BlockSpec((None, None, None, n_sub, hd), lambda b, g: (0, b, g, 0, 0)),
                  pl.BlockSpec((None, None, None, n_sub, hd), lambda b, g: (1, b, g, 0, 0)),
                  pl.BlockSpec((n_sub, n_past_blocks), lambda b, g: (0, 0)),
                  pl.BlockSpec((8, nq), lambda b, g: (0, 0))],
        out_specs=[q_spec, pl.BlockSpec((None, None, 8, LANE), lambda b, g: (b, g, 0, 0))],
        out_shape=[jax.ShapeDtypeStruct((batch, NSA_KV_HEADS, nq, hd), F32),
                   jax.ShapeDtypeStruct((batch, NSA_KV_HEADS, 8, LANE), jnp.int32)],
        compiler_params=pltpu.CompilerParams(dimension_semantics=("arbitrary", "arbitrary"),
                                             vmem_limit_bytes=VMEM_LIMIT_BYTES),
        name="decode_select",
    )(q_pad, cmp, cmp, jnp.asarray(cover, BF16), jnp.asarray(rsum, BF16))


def _decode_attend_kernel(pt_ref, idx_ref, q_ref, ocmp_ref, gate_ref, new_ref, kbuf_ref, vbuf_ref, cache_ref,
                          o_ref, kg_s, vg_s, sem, *, ls):
    b = pl.program_id(0)
    g = pl.program_id(1)
    n_sel = N_SELECT - 1
    hd = NSA_HEAD_DIM
    scale = hd ** -0.5

    def gather_copies():
        for t in range(ls):
            for k in range(n_sel):
                j = idx_ref[((b * NSA_KV_HEADS + g) * ls + t) * n_sel + k]
                page = pt_ref[b, j // (PAGE_SIZE // SLC_BLOCK)]
                row0 = (j % (PAGE_SIZE // SLC_BLOCK)) * SLC_BLOCK
                for which, (slot, dst) in enumerate(((2, kg_s), (3, vg_s))):
                    yield pltpu.make_async_copy(
                        cache_ref.at[page, pl.ds(row0, SLC_BLOCK), slot * NSA_KV_HEADS + g, :],
                        dst.at[t, k], sem.at[which])

    for cp in gather_copies():
        cp.start()

    nq = ls * DEC_QROWS
    new = new_ref[...].astype(BF16)
    q = q_ref[...].astype(BF16)
    tok = lax.broadcasted_iota(jnp.int32, (nq, 1), 0) // DEC_QROWS
    wb = kbuf_ref.shape[0]
    i_buf = lax.broadcasted_iota(jnp.int32, (nq, wb), 1)
    i_new = lax.broadcasted_iota(jnp.int32, (nq, new.shape[1]), 1)
    m_new = (i_new <= tok) & (i_new < ls)
    m_buf = (wb - i_buf + tok) < WINDOW
    s_b = jnp.where(m_buf, _dot_nt(q, kbuf_ref[...].astype(BF16)) * scale, NEG)
    s_n = jnp.where(m_new, _dot_nt(q, new[2]) * scale, NEG)
    mx = jnp.maximum(jnp.max(s_b, axis=-1, keepdims=True), jnp.max(s_n, axis=-1, keepdims=True))
    e_b = jnp.exp(s_b - mx)
    e_n = jnp.exp(s_n - mx)
    den = jnp.sum(e_b, axis=-1, keepdims=True) + jnp.sum(e_n, axis=-1, keepdims=True)
    o_swa = (jnp.dot(e_b.astype(BF16), vbuf_ref[...].astype(BF16), preferred_element_type=F32)
             + jnp.dot(e_n.astype(BF16), new[3], preferred_element_type=F32)) / den

    for cp in gather_copies():
        cp.wait()

    gates = gate_ref[...]
    for t in range(ls):
        rows = slice(t * DEC_QROWS, (t + 1) * DEC_QROWS)
        qt = q[rows]
        s_p = _dot_nt(qt, kg_s[t].reshape(n_sel * SLC_BLOCK, hd).astype(BF16)) * scale
        s_n = jnp.where(m_new[rows], _dot_nt(qt, new[0]) * scale, NEG)
        mx = jnp.maximum(jnp.max(s_p, axis=-1, keepdims=True), jnp.max(s_n, axis=-1, keepdims=True))
        e_p = jnp.exp(s_p - mx)
        e_n = jnp.exp(s_n - mx)
        den = jnp.sum(e_p, axis=-1, keepdims=True) + jnp.sum(e_n, axis=-1, keepdims=True)
        o_slc = (jnp.dot(e_p.astype(BF16), vg_s[t].reshape(n_sel * SLC_BLOCK, hd).astype(BF16),
                         preferred_element_type=F32)
                 + jnp.dot(e_n.astype(BF16), new[1], preferred_element_type=F32)) / den
        gt = gates[rows]
        o_ref[rows, :] = gt[:, 0:1] * ocmp_ref[rows, :] + gt[:, 1:2] * o_slc + gt[:, 2:3] * o_swa[rows]


def _decode_attend(q_pad, o_cmp, gates, new_rows, swa_buf, cache, page_table, idx, ls):
    batch = q_pad.shape[0]
    nq = ls * DEC_QROWS
    hd = NSA_HEAD_DIM
    wb = swa_buf.shape[1]
    assert wb == WINDOW and PAGE_SIZE % SLC_BLOCK == 0
    n_sel = N_SELECT - 1
    q_spec = pl.BlockSpec((None, None, nq, hd), lambda b, g, pt, ix: (b, g, 0, 0))

    def buf_spec(slot):
        return pl.BlockSpec((None, wb, hd), lambda b, g, pt, ix: (b, 0, slot * NSA_KV_HEADS + g))

    return pl.pallas_call(
        functools.partial(_decode_attend_kernel, ls=ls),
        grid_spec=pltpu.PrefetchScalarGridSpec(
            num_scalar_prefetch=2,
            grid=(batch, NSA_KV_HEADS),
            in_specs=[q_spec, q_spec,
                      pl.BlockSpec((None, None, nq, LANE), lambda b, g, pt, ix: (b, g, 0, 0)),
                      pl.BlockSpec((None, None, 4, 8, hd), lambda b, g, pt, ix: (b, g, 0, 0, 0)),
                      buf_spec(0), buf_spec(1),
                      pl.BlockSpec(memory_space=pl.ANY)],
            out_specs=q_spec,
            scratch_shapes=[pltpu.VMEM((ls, n_sel, SLC_BLOCK, hd), F32), pltpu.VMEM((ls, n_sel, SLC_BLOCK, hd), F32),
                            pltpu.SemaphoreType.DMA((2,))]),
        out_shape=jax.ShapeDtypeStruct((batch, NSA_KV_HEADS, nq, hd), F32),
        compiler_params=pltpu.CompilerParams(dimension_semantics=("arbitrary", "arbitrary"),
                                             vmem_limit_bytes=VMEM_LIMIT_BYTES),
        name="decode_attend",
    )(page_table, idx, q_pad, o_cmp, gates, new_rows,
      swa_buf.reshape(batch, wb, 2 * NSA_KV_HEADS * hd), swa_buf.reshape(batch, wb, 2 * NSA_KV_HEADS * hd),
      cache.reshape(cache.shape[0], PAGE_SIZE, 4 * NSA_KV_HEADS, hd))


HG_SLAB = 256
HG_UNROLL = 8


def _hgrn_kernel(hq_ref, hf_ref, hi_ref, hg_ref, lb_ref, gn_ref, s0_ref, tri_ref, o_ref, s_ref,
                 qs_s, el_s, o_s, u_s, st_s, *, seq):
    c = HG_CHUNK
    per_slab = HG_SLAB // c
    lb = lb_ref[...]

    def prep(i, carry):
        rows = pl.ds(pl.multiple_of(i * HG_SLAB, HG_SLAB), HG_SLAB)
        hq = hq_ref[rows, :]
        q = hq * jax.nn.sigmoid(hq)
        f = lb + (1.0 - lb) * jax.nn.sigmoid(hf_ref[rows, :])
        k = 1.0 - f
        v = hi_ref[rows, :].astype(BF16)
        bc = _sel_dot(tri_ref[0], jnp.log(f))
        bc3 = bc.reshape(per_slab, c, LANE)
        mid = (c - 1) // 2
        bm = jnp.broadcast_to(bc3[:, mid:mid + 1, :], bc3.shape).reshape(HG_SLAB, LANE)
        bl = jnp.broadcast_to(bc3[:, c - 1:c, :], bc3.shape).reshape(HG_SLAB, LANE)
        a = _dot_nt((q * jnp.exp(bc - bm)).astype(BF16), (k * jnp.exp(bm - bc)).astype(BF16))
        a = jnp.where(tri_ref[0] > 0, a, 0.0)
        o_s[rows, :] = jnp.dot(a.astype(BF16), v, preferred_element_type=F32)
        qs_s[rows, :] = (q * jnp.exp(bc)).astype(qs_s.dtype)
        kd = (k * jnp.exp(bl - bc)).astype(BF16)
        el = jnp.exp(bl)
        for j in range(per_slab):
            ch = slice(j * c, (j + 1) * c)
            u_s[i * per_slab + j] = lax.dot_general(v[ch], kd[ch], (((0,), (0,)), ((), ())),
                                                    preferred_element_type=F32)
            el_s[pl.ds(i * per_slab + j, 1), :] = el[j * c:j * c + 1]
        return carry

    lax.fori_loop(0, seq // HG_SLAB, prep, 0)

    def scan(ci, st):
        st_s[ci] = st.astype(st_s.dtype)
        return st * el_s[pl.ds(ci, 1), :] + u_s[ci]

    st = lax.fori_loop(0, seq // c, scan, s0_ref[...].T, unroll=HG_UNROLL)
    s_ref[...] = st.T

    gn = gn_ref[...]

    def finish(i, carry):
        rows = pl.ds(pl.multiple_of(i * HG_SLAB, HG_SLAB), HG_SLAB)
        qs = qs_s[rows, :]
        inter = [_dot_nt(qs[j * c:(j + 1) * c], st_s[i * per_slab + j]) for j in range(per_slab)]
        o = o_s[rows, :] + jnp.concatenate(inter, axis=0)
        hg = hg_ref[rows, :]
        o = o * lax.rsqrt(jnp.mean(o * o, axis=-1, keepdims=True) + EPS)
        o_ref[rows, :] = (o * gn * (hg * jax.nn.sigmoid(hg))).astype(o_ref.dtype)
        return carry

    lax.fori_loop(0, seq // HG_SLAB, finish, 0)


def _hgrn_tri():
    i = np.arange(HG_SLAB)[:, None]
    j = np.arange(HG_SLAB)[None, :]
    cum = ((i // HG_CHUNK) == (j // HG_CHUNK)) & (j <= i)
    return jnp.asarray(cum[None], BF16)


def _hgrn_prompt(proj, col0, lb, g_norm, s0, batch, seq):
    assert seq % HG_SLAB == 0 and col0 % LANE == 0 and HG_DK == LANE and HG_DV == LANE
    c0 = col0 // LANE

    def col_spec(group):
        return pl.BlockSpec((seq, LANE), lambda b, h: (b, c0 + group * HG_HEADS + h))

    vec_spec = pl.BlockSpec((1, LANE), lambda b, h: (0, h))
    st_spec = pl.BlockSpec((None, None, HG_DK, HG_DV), lambda b, h: (b, h, 0, 0))
    return pl.pallas_call(
        functools.partial(_hgrn_kernel, seq=seq),
        grid=(batch, HG_HEADS),
        in_specs=[col_spec(0), col_spec(1), col_spec(2), col_spec(3), vec_spec, vec_spec, st_spec,
                  pl.BlockSpec((1, HG_SLAB, HG_SLAB), lambda b, h: (0, 0, 0))],
        out_specs=[pl.BlockSpec((seq, LANE), lambda b, h: (b, h)), st_spec],
        out_shape=[jax.ShapeDtypeStruct((batch * seq, HG_V_W), BF16),
                   jax.ShapeDtypeStruct((batch, HG_HEADS, HG_DK, HG_DV), F32)],
        scratch_shapes=[pltpu.VMEM((seq, LANE), BF16), pltpu.VMEM((seq // HG_CHUNK, LANE), F32),
                        pltpu.VMEM((seq, LANE), F32), pltpu.VMEM((seq // HG_CHUNK, HG_DV, HG_DK), F32),
                        pltpu.VMEM((seq // HG_CHUNK, HG_DV, HG_DK), BF16)],
        compiler_params=pltpu.CompilerParams(
            dimension_semantics=("arbitrary", "arbitrary"),
            vmem_limit_bytes=VMEM_LIMIT_BYTES),
        name="hgrn_prompt",
    )(proj, proj, proj, proj, lb, g_norm, s0, _hgrn_tri())


SSD_TL = 256
SSD_GW = SSM_D_INNER // SSM_GROUPS
SSD_HPG = SSM_HEADS // SSM_GROUPS


def _split3(x):
    hi = x.astype(BF16)
    r1 = x - hi.astype(F32)
    mid = r1.astype(BF16)
    lo = (r1 - mid.astype(F32)).astype(BF16)
    return hi, mid, lo


def _sel_dot(sel, x):
    hi, mid, lo = _split3(x)
    d = functools.partial(jnp.dot, preferred_element_type=F32)
    return d(sel, hi) + d(sel, mid) + d(sel, lo)


def _dot_sel(x, sel):
    hi, mid, lo = _split3(x)
    d = functools.partial(jnp.dot, preferred_element_type=F32)
    return d(hi, sel) + d(mid, sel) + d(lo, sel)


def _causal_conv_silu(x, prev8, w, bias):
    row8 = lax.broadcasted_iota(jnp.int32, prev8.shape, 0)
    acc = bias
    for k in range(SSM_CONV - 1, 0, -1):
        r = pltpu.roll(x, k, 0)
        top = jnp.where(row8 < k, pltpu.roll(prev8, k, 0), r[:8])
        acc = acc + jnp.concatenate([top, r[8:]], axis=0) * w[SSM_CONV - 1 - k:SSM_CONV - k]
    acc = acc + x * w[SSM_CONV - 1:SSM_CONV]
    return acc * jax.nn.sigmoid(acc)


def _ssd_kernel(z_ref, x_ref, b_ref, c_ref, dt_ref, cx0_ref, cb0_ref, cc0_ref, s0_ref,
                wx_ref, wb_ref, wc_ref, bx_ref, bb_ref, bc_ref, dtb_ref, alog_ref, d_ref, gn_ref,
                tri_ref, hsel_ref, o_ref, s_ref,
                st_s, tx_s, tb_s, tc_s, xdt_s, xdd_s, ce_s, le_s, bm_s, cm_s, y_s):
    lt = pl.program_id(2)
    c = SSM_CHUNK
    tl = SSD_TL
    hp = SSM_HEAD_DIM

    @pl.when(lt == 0)
    def _():
        st_s[...] = s0_ref[...].T
        for tail, c0 in ((tx_s, cx0_ref), (tb_s, cb0_ref), (tc_s, cc0_ref)):
            tail[...] = jnp.zeros(tail.shape, F32)
            tail[8 - (SSM_CONV - 1):8, :] = c0[...]

    x_raw = x_ref[...]
    b_raw = b_ref[...]
    c_raw = c_ref[...]
    xs = _causal_conv_silu(x_raw, tx_s[...], wx_ref[...], bx_ref[...])
    bm_s[...] = _causal_conv_silu(b_raw, tb_s[...], wb_ref[...], bb_ref[...]).astype(bm_s.dtype)
    cm_s[...] = _causal_conv_silu(c_raw, tc_s[...], wc_ref[...], bc_ref[...]).astype(cm_s.dtype)
    tx_s[...] = x_raw[tl - 8:]
    tb_s[...] = b_raw[tl - 8:]
    tc_s[...] = c_raw[tl - 8:]

    dt = jax.nn.softplus(dt_ref[...] + dtb_ref[...])
    cum = _sel_dot(tri_ref[0], dt * -jnp.exp(alog_ref[...]))
    hsel = hsel_ref[...]
    dt_e = _dot_sel(dt, hsel)
    cum_e = _dot_sel(cum, hsel)
    last_e = _sel_dot(tri_ref[1], cum_e)
    xdt = xs * dt_e
    xdt_s[...] = xdt.astype(xdt_s.dtype)
    xdd_s[...] = (xdt * jnp.exp(last_e - cum_e)).astype(xdd_s.dtype)
    ce_s[...] = cum_e
    le_s[...] = jnp.exp(last_e)

    trow = lax.broadcasted_iota(jnp.int32, (c, SSD_GW), 0)
    lane_g = lax.broadcasted_iota(jnp.int32, (c, SSD_GW), 1)
    diag = (lane_g % hp) == trow
    t2 = lax.broadcasted_iota(jnp.int32, (c, LANE), 0)
    l2 = lax.broadcasted_iota(jnp.int32, (c, LANE), 1)
    tril2 = (l2 % hp) <= t2
    r3 = lax.broadcasted_iota(jnp.int32, (2 * c, LANE), 0)
    l3 = lax.broadcasted_iota(jnp.int32, (2 * c, LANE), 1)
    blockdiag = (r3 // c) == (l3 // hp)

    for ci in range(tl // c):
        rows = pl.ds(ci * c, c)
        cm = cm_s[rows, :]
        bm = bm_s[rows, :]
        cum_c = ce_s[rows, :]
        cum_row = jnp.sum(jnp.where(diag, cum_c, 0.0), axis=0, keepdims=True)
        cb2 = _dot_nt(cm, jnp.concatenate([bm, bm], axis=0))
        xdt_c = xdt_s[rows, :]
        st = st_s[...]
        y_inter = jnp.dot(cm, st.astype(cm.dtype), preferred_element_type=F32) * jnp.exp(cum_c)
        for j in range(SSD_GW // LANE):
            cols = slice(j * LANE, (j + 1) * LANE)
            seg = cum_c[:, cols] - cum_row[:, cols]
            w = (cb2 * jnp.where(tril2, jnp.exp(seg), 0.0)).astype(xdt_c.dtype)
            xj = xdt_c[:, cols]
            xbd = jnp.where(blockdiag, jnp.concatenate([xj, xj], axis=0), jnp.zeros_like(xj[:1, :1]))
            y_s[rows, cols] = y_inter[:, cols] + jnp.dot(w, xbd, preferred_element_type=F32)
        upd = lax.dot_general(bm, xdd_s[rows, :], (((0,), (0,)), ((), ())), preferred_element_type=F32)
        st_s[...] = st * le_s[pl.ds(ci * c, 1), :] + upd

    z = z_ref[...]
    y = (y_s[...] + d_ref[...] * xs) * (z * jax.nn.sigmoid(z))
    y = y * lax.rsqrt(jnp.mean(y * y, axis=-1, keepdims=True) + EPS)
    o_ref[...] = (y * gn_ref[...]).astype(o_ref.dtype)

    @pl.when(lt == pl.num_programs(2) - 1)
    def _():
        s_ref[...] = st_s[...].T


def _ssd_consts():
    i = np.arange(SSD_TL)[:, None]
    j = np.arange(SSD_TL)[None, :]
    cum = ((i // SSM_CHUNK) == (j // SSM_CHUNK)) & (j <= i)
    last = j == (i // SSM_CHUNK) * SSM_CHUNK + SSM_CHUNK - 1
    tri = jnp.asarray(np.stack([cum, last]), BF16)
    h = np.arange(SSM_HEADS)[None, :, None]
    g = np.arange(SSM_GROUPS)[:, None, None]
    lane = np.arange(SSD_GW)[None, None, :]
    hsel = jnp.asarray(h == g * SSD_HPG + lane // SSM_HEAD_DIM, BF16)
    return tri, hsel


def _ssd_prompt(proj, conv0, s0, conv_w, conv_b, dt_bias, a_log, d_skip, norm_g, batch, seq):
    assert seq % SSD_TL == 0 and SSM_STATE == LANE and SSM_HEADS == LANE and SSM_CHUNK == SSM_HEAD_DIM
    nl = seq // SSD_TL
    gpl = SSD_GW // LANE
    xb = SSM_D_INNER // SSD_GW
    bb = 2 * SSM_D_INNER // LANE
    cb = bb + SSM_GROUPS
    db = cb + SSM_GROUPS
    tri, hsel = _ssd_consts()
    d_e = jnp.repeat(d_skip.astype(F32), SSM_HEAD_DIM)[None]
    row = lambda a: a.astype(F32)[None]

    def rows_spec(width, col_fn):
        return pl.BlockSpec((SSD_TL, width), lambda b, g, t: (b * nl + t, col_fn(g)))

    def conv0_spec(width, col_fn):
        return pl.BlockSpec((None, SSM_CONV - 1, width), lambda b, g, t: (b, 0, col_fn(g)))

    def vec_spec(rows_, width, col_fn):
        return pl.BlockSpec((rows_, width), lambda b, g, t: (0, col_fn(g)))

    xcol = lambda g: g
    bcol = lambda g: SSM_D_INNER // LANE + g
    ccol = lambda g: SSM_D_INNER // LANE + SSM_GROUPS + g
    zero = lambda g: 0
    st_spec = pl.BlockSpec((None, SSD_GW, SSM_STATE), lambda b, g, t: (b, g, 0))
    return pl.pallas_call(
        _ssd_kernel,
        grid=(batch, SSM_GROUPS, nl),
        in_specs=[rows_spec(SSD_GW, xcol), rows_spec(SSD_GW, lambda g: xb + g),
                  rows_spec(LANE, lambda g: bb + g), rows_spec(LANE, lambda g: cb + g),
                  rows_spec(LANE, lambda g: db),
                  conv0_spec(SSD_GW, xcol), conv0_spec(LANE, bcol), conv0_spec(LANE, ccol), st_spec,
                  vec_spec(SSM_CONV, SSD_GW, xcol), vec_spec(SSM_CONV, LANE, bcol), vec_spec(SSM_CONV, LANE, ccol),
                  vec_spec(1, SSD_GW, xcol), vec_spec(1, LANE, bcol), vec_spec(1, LANE, ccol),
                  vec_spec(1, LANE, zero), vec_spec(1, LANE, zero),
                  vec_spec(1, SSD_GW, xcol), vec_spec(1, SSD_GW, xcol),
                  pl.BlockSpec((2, SSD_TL, SSD_TL), lambda b, g, t: (0, 0, 0)),
                  pl.BlockSpec((None, SSM_HEADS, SSD_GW), lambda b, g, t: (g, 0, 0))],
        out_specs=[rows_spec(SSD_GW, xcol), st_spec],
        out_shape=[jax.ShapeDtypeStruct((batch * seq, SSM_D_INNER), BF16),
                   jax.ShapeDtypeStruct((batch, SSM_HEADS * SSM_HEAD_DIM, SSM_STATE), F32)],
        scratch_shapes=[pltpu.VMEM((SSM_STATE, SSD_GW), F32),
                        pltpu.VMEM((8, SSD_GW), F32), pltpu.VMEM((8, LANE), F32), pltpu.VMEM((8, LANE), F32),
                        pltpu.VMEM((SSD_TL, SSD_GW), BF16), pltpu.VMEM((SSD_TL, SSD_GW), BF16),
                        pltpu.VMEM((SSD_TL, SSD_GW), F32), pltpu.VMEM((SSD_TL, SSD_GW), F32),
                        pltpu.VMEM((SSD_TL, LANE), BF16), pltpu.VMEM((SSD_TL, LANE), BF16),
                        pltpu.VMEM((SSD_TL, SSD_GW), F32)],
        compiler_params=pltpu.CompilerParams(
            dimension_semantics=("arbitrary", "arbitrary", "arbitrary"),
            vmem_limit_bytes=VMEM_LIMIT_BYTES),
        name="ssd_prompt",
    )(proj, proj, proj, proj, proj, conv0, conv0, conv0, s0,
      conv_w, conv_w, conv_w, row(conv_b), row(conv_b), row(conv_b), row(dt_bias), row(a_log),
      d_e, row(norm_g), tri, hsel)


def split_cols(a, sizes):
    out, o = [], 0
    for s in sizes:
        out.append(a[..., o:o + s])
        o += s
    return out


def to_chunks(a, c):
    b, l = a.shape[:2]
    pad = (-l) % c
    a = jnp.pad(a.astype(F32), [(0, 0), (0, pad)] + [(0, 0)] * (a.ndim - 2))
    return jnp.moveaxis(a.reshape((b, (l + pad) // c, c) + a.shape[2:]), 1, 0)


def from_chunks(a, l):
    a = jnp.moveaxis(a, 0, 1)
    return a.reshape((a.shape[0], a.shape[1] * a.shape[2]) + a.shape[3:])[:, :l]


def gqa_attend(q, k, v, mask):
    s = jnp.einsum('...qgrd,...kgd->...qgrk', q, k, preferred_element_type=F32) * (q.shape[-1] ** -0.5)
    m = mask[..., :, None, None, :]
    p = jnp.where(m, jax.nn.softmax(jnp.where(m, s, NEG), axis=-1), 0.0)
    o = jnp.einsum('...qgrk,...kgd->...qgrd', p.astype(v.dtype), v)
    return o, p


def compress_rows(rows, w1, w2, pe):
    b, t = rows.shape[:2]
    n_sub = t // CMP_STRIDE
    n_per = CMP_BLOCK // CMP_STRIDE
    n_cmp = n_sub - n_per + 1
    sub = rows[:, :n_sub * CMP_STRIDE].reshape(b, n_sub, CMP_STRIDE, NSA_KV_HEADS, NSA_HEAD_DIM)
    sub = jnp.moveaxis(sub, 3, 2).reshape(b, n_sub, NSA_KV_HEADS, CMP_STRIDE * NSA_HEAD_DIM)
    w1r = w1.reshape(n_per, CMP_STRIDE * NSA_HEAD_DIM, CMP_HIDDEN)
    hid = pe.reshape(-1) @ w1
    for m in range(n_per):
        hid = hid + jnp.einsum('bngc,ch->bngh', sub[:, m:m + n_cmp], w1r[m])
    return jax.nn.silu(hid) @ w2


def slc_attend(q, pos, kb, vb, idx, valid):
    b, lq, g, r, d = q.shape
    k = idx.shape[-1]
    ii = jnp.moveaxis(idx, 2, 1).reshape(b, g, lq * k)
    bi = jnp.arange(b)[:, None, None]
    gi = jnp.arange(g)[None, :, None]
    kg = kb[bi, gi, ii].reshape(b, g, lq, k, SLC_BLOCK, d)
    vg = vb[bi, gi, ii].reshape(b, g, lq, k, SLC_BLOCK, d)
    s = jnp.einsum('bqgrd,bgqksd->bqgrks', q, kg, preferred_element_type=F32) * (d ** -0.5)
    kpos = idx[..., None] * SLC_BLOCK + jnp.arange(SLC_BLOCK)
    mask = ((kpos <= pos[None, :, None, None, None]) & valid[..., None])[:, :, :, None]
    s = jnp.where(mask, s, NEG).reshape(b, lq, g, r, k * SLC_BLOCK)
    p = jax.nn.softmax(s, axis=-1).reshape(b, lq, g, r, k, SLC_BLOCK)
    return jnp.einsum('bqgrks,bgqksd->bqgrd', p.astype(vg.dtype), vg)


def nsa_cmp_slc(q, pos, rows, w1, w2, pe):
    b, l, g, r, d = q.shape
    t = rows.shape[1]
    kc = compress_rows(rows[:, :, 0], w1[0], w2[0], pe[0])
    vc = compress_rows(rows[:, :, 1], w1[1], w2[1], pe[1])
    n_cmp = kc.shape[1]
    cmp_last = jnp.arange(n_cmp) * CMP_STRIDE + (CMP_BLOCK - 1)
    o_cmp, p_cmp = gqa_attend(q, kc, vc, cmp_last[None, :] <= pos[:, None])
    n_slc = -(-t // SLC_BLOCK)
    c0 = np.arange(n_cmp)[:, None] * CMP_STRIDE
    s0 = np.arange(n_slc)[None, :] * SLC_BLOCK
    cover = np.clip(np.minimum(c0 + CMP_BLOCK, s0 + SLC_BLOCK) - np.maximum(c0, s0), 0, None) / CMP_BLOCK
    score = jnp.einsum('blgrn,nj->blgj', p_cmp, jnp.asarray(cover, F32))
    blk = jnp.arange(n_slc)[None, :]
    cur = (pos // SLC_BLOCK)[:, None]
    causal = blk * SLC_BLOCK <= pos[:, None]
    forced = (blk == 0) | (blk == cur) | (blk == cur - 1)
    score = jnp.where(forced[None, :, None, :], FORCE_SCORE, jnp.where(causal[None, :, None, :], score, -1.0))
    vals, idx = lax.top_k(score, min(N_SELECT, n_slc))
    valid = vals >= 0.0
    kv = jnp.pad(rows[:, :, 2:4], ((0, 0), (0, n_slc * SLC_BLOCK - t), (0, 0), (0, 0), (0, 0)))
    kv = jnp.transpose(kv.reshape(b, n_slc, SLC_BLOCK, 2, g, d), (3, 0, 4, 1, 2, 5))
    kb, vb = kv[0], kv[1]
    qb = math.gcd(l, SLC_QBLOCK)
    nb = l // qb

    def blocks(a):
        return jnp.moveaxis(a.reshape((b, nb, qb) + a.shape[2:]), 1, 0)

    o_slc = lax.map(lambda a: slc_attend(a[0], a[1], kb, vb, a[2], a[3]),
                    (blocks(q), pos.reshape(nb, qb), blocks(idx), blocks(valid)))
    o_slc = jnp.moveaxis(o_slc, 0, 1).reshape(b, l, g, r, d)
    return o_cmp, o_slc


def swa_buffered(q, pos, kv_new, buf):
    wb = buf.shape[1]
    keys = jnp.concatenate([buf.astype(kv_new.dtype), kv_new], axis=1)
    kpos = PAST_LEN - wb + jnp.arange(keys.shape[1])
    mask = (kpos[None, :] <= pos[:, None]) & (pos[:, None] - kpos[None, :] < WINDOW)
    o, _ = gqa_attend(q, keys[:, :, 0], keys[:, :, 1], mask)
    return o, keys[:, keys.shape[1] - wb:]


def gla_chunked(q, k, v, logf, s0):
    b, l = q.shape[:2]
    c = min(HG_CHUNK, l)
    mid = (c - 1) // 2
    tril = jnp.tril(jnp.ones((c, c), bool))

    def step(s, inp):
        qc, kc, vc, gc = inp
        bc = jnp.cumsum(gc, axis=1)
        bm = bc[:, mid:mid + 1]
        a = jnp.einsum('bthd,bshd->bhts', qc * jnp.exp(bc - bm), kc * jnp.exp(bm - bc))
        a = jnp.where(tril, a, 0.0)
        o = jnp.einsum('bhts,bshv->bthv', a, vc) + jnp.einsum('bthd,bhdv->bthv', qc * jnp.exp(bc), s)
        bl = bc[:, -1]
        s = jnp.exp(bl)[..., None] * s + jnp.einsum('bshd,bshv->bhdv', kc * jnp.exp(bl[:, None] - bc), vc)
        return s, o

    s, o = lax.scan(step, s0.astype(F32), (to_chunks(q, c), to_chunks(k, c), to_chunks(v, c), to_chunks(logf, c)))
    return from_chunks(o, l), s


def hgrn2(hq, hf, hi, hg, lb, s0, g_norm):
    b, l, _ = hq.shape
    q = jax.nn.silu(hq.astype(F32)).reshape(b, l, HG_HEADS, HG_DK)
    f = (lb + (1.0 - lb) * jax.nn.sigmoid(hf.astype(F32))).reshape(b, l, HG_HEADS, HG_DK)
    v = hi.astype(F32).reshape(b, l, HG_HEADS, HG_DV)
    o, s = gla_chunked(q, 1.0 - f, v, jnp.log(f), s0)
    o = o * lax.rsqrt(jnp.mean(o * o, axis=-1, keepdims=True) + EPS)
    o = o.reshape(b, l, HG_V_W) * g_norm.astype(F32) * jax.nn.silu(hg.astype(F32))
    return o, s


def nsa_prompt_branch(proj, b, l, w1, w2, pe):
    cmp = _compress_prompt(proj, w1, w2, pe, b, l)
    g = proj[:, AB_GATE_COL:AB_GATE_COL + NSA_GATE_W]
    gates = jax.nn.sigmoid(g).reshape(b * l, NSA_KV_HEADS, 3 * NSA_GROUP)
    gates = jnp.pad(gates, ((0, 0), (0, 0), (0, LANE - 3 * NSA_GROUP))).reshape(b * l, NSA_KV_HEADS * LANE)
    return _nsa_prompt(proj, cmp[0], cmp[1], gates, b, l)


AB_HG_COL = NSA_Q + NSA_KV_W
AB_GATE_COL = AB_HG_COL + 2 * HG_QF_W + 2 * HG_V_W
AB_PACKED = -(-(AB_GATE_COL + NSA_GATE_W) // LANE) * LANE


def pack_ab_in(w):
    g0 = NSA_Q + NSA_KV_W
    pad = jnp.zeros((w.shape[0], AB_PACKED - w.shape[1]), BF16)
    return jnp.concatenate([w[:, :g0].astype(BF16), w[:, g0 + NSA_GATE_W:].astype(BF16),
                            w[:, g0:g0 + NSA_GATE_W].astype(BF16), pad], axis=1)


def ab_mixer_prompt(x, h, b, l, hg_state, lb, w_in, w1, w2, pe, hg_g, w_out):
    proj = _mm(h, w_in)
    o_nsa = nsa_prompt_branch(proj, b, l, w1, w2, pe)
    o_hg, s_new = _hgrn_prompt(proj, AB_HG_COL, lb[None], hg_g[None], hg_state, b, l)
    x = _mm([o_nsa, o_hg], w_out, res=x)
    kv = proj[:, NSA_Q:NSA_Q + NSA_KV_W].reshape(b, l, 6, NSA_KV_HEADS, NSA_HEAD_DIM)
    return x, kv[:, :, :4], kv[:, l - min(WINDOW, l):, 4:], s_new


def nsa_decode_branch(q, kv, g, cache, page_table, swa_buf, w1, w2, pe):
    b, ls = q.shape[:2]
    hd = NSA_HEAD_DIM
    row_pad = DEC_QROWS - NSA_GROUP
    q_pad = q.reshape(b, ls, NSA_KV_HEADS, NSA_GROUP, hd).transpose(0, 2, 1, 3, 4)
    q_pad = jnp.pad(q_pad, ((0, 0), (0, 0), (0, 0), (0, row_pad), (0, 0))).reshape(b, NSA_KV_HEADS, ls * DEC_QROWS, hd)
    gates = jax.nn.sigmoid(g).reshape(b, ls, NSA_KV_HEADS, NSA_GROUP, 3).transpose(0, 2, 1, 3, 4)
    gates = jnp.pad(gates, ((0, 0), (0, 0), (0, 0), (0, row_pad), (0, LANE - 3)))
    gates = gates.reshape(b, NSA_KV_HEADS, ls * DEC_QROWS, LANE)
    new_rows = jnp.pad(kv[:, :, 2:6].transpose(0, 3, 2, 1, 4), ((0, 0), (0, 0), (0, 0), (0, 8 - ls), (0, 0)))
    cmp = _compress_paged(cache, page_table, w1, w2, pe)
    o_cmp, idx = _decode_select(q_pad, cmp, ls)
    o = _decode_attend(q_pad, o_cmp, gates, new_rows, swa_buf, cache, page_table,
                       idx[:, :, :ls, :N_SELECT - 1].reshape(-1), ls)
    o = o.reshape(b, NSA_KV_HEADS, ls, DEC_QROWS, hd)[:, :, :, :NSA_GROUP]
    return o.transpose(0, 2, 1, 3, 4).reshape(b * ls, NSA_Q)


def ab_mixer_sample(x, h, b, l, cache, page_table, swa_buf, hg_state, lb, w_in, w1, w2, pe, hg_g, w_out):
    proj = _mm(h, w_in).reshape(b, l, AB_PACKED)
    q, kv, hq, hf, hi, hgate, g = split_cols(
        proj, (NSA_Q, NSA_KV_W, HG_QF_W, HG_QF_W, HG_V_W, HG_V_W, NSA_GATE_W))
    kv = kv.reshape(b, l, 6, NSA_KV_HEADS, NSA_HEAD_DIM)
    o_nsa = nsa_decode_branch(q, kv, g, cache, page_table, swa_buf, w1, w2, pe)
    new_buf = jnp.concatenate([swa_buf.astype(kv.dtype), kv[:, :, 4:]], axis=1)[:, l:]
    o_hg, s_new = hgrn2(hq, hf, hi, hgate, lb, hg_state, hg_g)
    mix = jnp.concatenate([o_nsa, o_hg.reshape(b * l, HG_V_W)], axis=-1).astype(BF16)
    return _mm(mix, w_out, res=x), kv[:, :, :4], new_buf, s_new


def ssd_chunked(x, dt, a, bm, cm, s0):
    b, l, nh, p = x.shape
    g, n = bm.shape[2], bm.shape[3]
    r = nh // g
    c = min(SSM_CHUNK, l)
    tril = jnp.tril(jnp.ones((c, c), bool))

    def step(s, inp):
        xc, dtc, bc, cc = inp
        cum = jnp.cumsum(dtc * a, axis=1)
        seg = cum[:, :, None, :] - cum[:, None, :, :]
        lm = jnp.exp(jnp.where(tril[None, :, :, None], seg, -jnp.inf)).reshape(b, c, c, g, r)
        xdt = (xc * dtc[..., None]).reshape(b, c, g, r, p)
        cb = jnp.einsum('btgn,bsgn->btsg', cc, bc)
        sg = s.reshape(b, g, r, p, n)
        y = jnp.einsum('btsg,btsgr,bsgrp->btgrp', cb, lm, xdt)
        y = y + jnp.einsum('btgn,bgrpn->btgrp', cc, sg) * jnp.exp(cum).reshape(b, c, g, r)[..., None]
        dec = jnp.exp(cum[:, -1:] - cum).reshape(b, c, g, r)
        sg = jnp.exp(cum[:, -1]).reshape(b, g, r)[..., None, None] * sg + jnp.einsum('bsgn,bsgrp->bgrpn', bc, xdt * dec[..., None])
        return sg.reshape(b, nh, p, n), y.reshape(b, c, nh, p)

    s, y = lax.scan(step, s0.astype(F32), (to_chunks(x, c), to_chunks(dt, c), to_chunks(bm, c), to_chunks(cm, c)))
    return from_chunks(y, l), s


def mamba_prompt(x, h, b, l, w_in, conv_w, conv_b, dt_bias, a_log, d_skip, norm_g, w_out):
    proj = _mm(h, w_in)
    y, s = _ssd_prompt(proj, jnp.zeros((b, SSM_CONV - 1, SSM_CONV_DIM), F32),
                       jnp.zeros((b, SSM_HEADS * SSM_HEAD_DIM, SSM_STATE), F32),
                       conv_w, conv_b, dt_bias, a_log, d_skip, norm_g, b, l)
    assert l >= SSM_CONV - 1
    new_conv = proj.reshape(b, l, -1)[:, l - (SSM_CONV - 1):, SSM_D_INNER:SSM_D_INNER + SSM_CONV_DIM]
    return _mm(y, w_out, res=x), new_conv, s.reshape(b, SSM_HEADS, SSM_HEAD_DIM, SSM_STATE)


def mamba_sample(x, h, b, l, conv_state, ssm_state, w_in, conv_w, conv_b, dt_bias, a_log, d_skip, norm_g, w_out):
    z, xbc, dt = split_cols(_mm(h, w_in).reshape(b, l, -1), (SSM_D_INNER, SSM_CONV_DIM, SSM_HEADS))
    xpad = jnp.concatenate([conv_state.astype(xbc.dtype), xbc], axis=1)
    acc = conv_b.astype(F32)
    for j in range(SSM_CONV):
        acc = acc + xpad[:, j:j + l].astype(F32) * conv_w[j].astype(F32)
    xbc = jax.nn.silu(acc)
    new_conv = xpad[:, xpad.shape[1] - (SSM_CONV - 1):]
    xs, bm, cm = split_cols(xbc, (SSM_D_INNER, SSM_GROUPS * SSM_STATE, SSM_GROUPS * SSM_STATE))
    xs = xs.reshape(b, l, SSM_HEADS, SSM_HEAD_DIM)
    bm = bm.reshape(b, l, SSM_GROUPS, SSM_STATE)
    cm = cm.reshape(b, l, SSM_GROUPS, SSM_STATE)
    dt = jax.nn.softplus(dt.astype(F32) + dt_bias.astype(F32))
    a = -jnp.exp(a_log.astype(F32))
    y, s = ssd_chunked(xs, dt, a, bm, cm, ssm_state)
    y = y + d_skip.astype(F32)[:, None] * xs
    y = (y.reshape(b, l, SSM_D_INNER) * jax.nn.silu(z.astype(F32))).reshape(b, l, SSM_GROUPS, SSM_D_INNER // SSM_GROUPS)
    y = (y * lax.rsqrt(jnp.mean(y * y, axis=-1, keepdims=True) + EPS)).reshape(b, l, SSM_D_INNER) * norm_g.astype(F32)
    return _mm(y.reshape(b * l, SSM_D_INNER).astype(BF16), w_out, res=x), new_conv, s


def ffn_ple(x, p, g_ffn, w_ffn_in, w_ffn_out, g_ple, w_ple_gate, w_ple_up):
    x = _mm(_mm_swiglu(_rms(x, g_ffn, BF16), w_ffn_in), w_ffn_out, res=x)
    return _mm_ple(_rms(x, g_ple, BF16), w_ple_gate, p, w_ple_up, x)


def kernel(x_prompt, x_sample, cache_nsa_kv, cache_swa_kv, state_hgrn, state_ssm, cache_conv, page_table, p_prompt, p_sample, norm_mix, norm_ffn, w_ab_in, w_cmp1, w_cmp2, cmp_pe, hg_lb_logits, hg_norm, w_ab_out, w_ssm_in, ssm_conv_w, ssm_conv_b, ssm_dt_bias, ssm_a_log, ssm_d, ssm_norm, w_ssm_out, w_ffn_in, w_ffn_out, w_ple_up, w_ple_gate, norm_ple, norm_final):
    depth = norm_mix.shape[0]
    bp, lp, d = x_prompt.shape
    bs, ls, _ = x_sample.shape
    n_pages = page_table.shape[1]
    pos_s = PAST_LEN + jnp.arange(ls)
    lb_all = jnp.cumsum(jax.nn.softmax(hg_lb_logits.astype(F32), axis=0), axis=0)
    xp = x_prompt.reshape(bp * lp, d)
    xs = x_sample.reshape(bs * ls, d)
    nsa_p, nsa_s, swa_p, swa_s, hg_p, hg_s, ssm_p, ssm_s, cv_p, cv_s = [], [], [], [], [], [], [], [], [], []
    for i in range(depth):
        hp = _rms(xp, norm_mix[i], BF16)
        hs = _rms(xs, norm_mix[i], BF16)
        if i % 2 == 0:
            a = i // 2
            wa = (lb_all[a], pack_ab_in(w_ab_in[a]), w_cmp1[a], w_cmp2[a], cmp_pe[a], hg_norm[a], w_ab_out[a])
            xp, r_p, b_p, s_p = ab_mixer_prompt(xp, hp, bp, lp, jnp.zeros((bp, HG_HEADS, HG_DK, HG_DV), F32), *wa)
            xs, r_s, b_s, s_s = ab_mixer_sample(xs, hs, bs, ls, cache_nsa_kv[a], page_table, cache_swa_kv[a],
                                                state_hgrn[a], *wa)
            nsa_p.append(r_p)
            nsa_s.append(r_s)
            swa_p.append(b_p)
            swa_s.append(b_s)
            hg_p.append(s_p)
            hg_s.append(s_s)
        else:
            c = i // 2
            wc = (w_ssm_in[c], ssm_conv_w[c], ssm_conv_b[c], ssm_dt_bias[c], ssm_a_log[c], ssm_d[c],
                  ssm_norm[c], w_ssm_out[c].astype(BF16))
            xp, c_p, t_p = mamba_prompt(xp, hp, bp, lp, *wc)
            xs, c_s, t_s = mamba_sample(xs, hs, bs, ls, cache_conv[c], state_ssm[c], *wc)
            cv_p.append(c_p)
            cv_s.append(c_s)
            ssm_p.append(t_p)
            ssm_s.append(t_s)
        wf = (norm_ffn[i], w_ffn_in[i], w_ffn_out[i].astype(BF16), norm_ple[i],
              w_ple_gate[i], w_ple_up[i])
        xp = ffn_ple(xp, p_prompt[i].reshape(bp * lp, PLE_DIM).astype(BF16), *wf)
        xs = ffn_ple(xs, p_sample[i].reshape(bs * ls, PLE_DIM).astype(BF16), *wf)
    y_prompt = _rms(xp, norm_final, F32).reshape(bp, lp, d)
    y_sample = _rms(xs, norm_final, F32).reshape(bs, ls, d)
    return (y_prompt, y_sample, jnp.stack(nsa_p), jnp.stack(nsa_s), jnp.stack(swa_p), jnp.stack(swa_s),
            jnp.stack(hg_p), jnp.stack(hg_s), jnp.stack(ssm_p), jnp.stack(ssm_s), jnp.stack(cv_p), jnp.stack(cv_s))
```

```python
import functools
import math
from typing import NamedTuple

import jax
import jax.numpy as jnp
import numpy as np
from jax import lax
from jax.experimental import pallas as pl
from jax.experimental.pallas import tpu as pltpu

D_MODEL = 4096
PAST_LEN = 16384
PAGE_SIZE = 128
PLE_DIM = 256
NSA_HEADS = 16
NSA_KV_HEADS = 4
NSA_GROUP = NSA_HEADS // NSA_KV_HEADS
NSA_HEAD_DIM = 128
NSA_Q = NSA_HEADS * NSA_HEAD_DIM
NSA_KV_W = 6 * NSA_KV_HEADS * NSA_HEAD_DIM
NSA_GATE_W = 3 * NSA_HEADS
CMP_BLOCK = 32
CMP_STRIDE = 16
CMP_HIDDEN = 256
SLC_BLOCK = 64
N_SELECT = 16
WINDOW = 512
SLC_QBLOCK = 32
SWA_QBLOCK = 128
FORCE_SCORE = 1e4
HG_HEADS = 16
HG_DK = 128
HG_DV = (D_MODEL // 2) // HG_HEADS
HG_QF_W = HG_HEADS * HG_DK
HG_V_W = HG_HEADS * HG_DV
HG_CHUNK = 32
AB_SIZES = (NSA_Q, NSA_KV_W, NSA_GATE_W, HG_QF_W, HG_QF_W, HG_V_W, HG_V_W)
SSM_D_INNER = 2 * D_MODEL
SSM_HEAD_DIM = 64
SSM_HEADS = SSM_D_INNER // SSM_HEAD_DIM
SSM_GROUPS = 8
SSM_STATE = 128
SSM_CONV = 4
SSM_CONV_DIM = SSM_D_INNER + 2 * SSM_GROUPS * SSM_STATE
SSM_CHUNK = 64
D_FF = ((8 * D_MODEL + 3 * 256 - 1) // (3 * 256)) * 256
EPS = 1e-6
NEG = -1e30

LANE = 128
VMEM_LIMIT_BYTES = 56 * 1024 * 1024
BF16 = jnp.bfloat16
F32 = jnp.float32


MAX_ROW_TILE = 1024
NORM_ROW_TILE = 512


def _rms_kernel(x_ref, g_ref, o_ref):
    x = x_ref[...]
    y = x * lax.rsqrt(jnp.mean(x * x, axis=-1, keepdims=True) + EPS)
    o_ref[...] = (y * g_ref[...]).astype(o_ref.dtype)


def _rms(x, g, out_dtype):
    m, d = x.shape
    tm = min(m, NORM_ROW_TILE)
    assert m % tm == 0
    return pl.pallas_call(
        _rms_kernel,
        grid=(m // tm,),
        in_specs=[pl.BlockSpec((tm, d), lambda i: (i, 0)), pl.BlockSpec((1, d), lambda i: (0, 0))],
        out_specs=pl.BlockSpec((tm, d), lambda i: (i, 0)),
        out_shape=jax.ShapeDtypeStruct((m, d), out_dtype),
        compiler_params=pltpu.CompilerParams(dimension_semantics=("arbitrary",),
                                             vmem_limit_bytes=VMEM_LIMIT_BYTES),
        name="rms",
    )(x, g.astype(F32)[None])


def _mm_kernel(*refs, n_lhs, residual):
    x_refs = refs[:n_lhs]
    w_ref = refs[n_lhs]
    o_ref = refs[-1]
    k = pl.program_id(2)

    def first(x_ref):
        acc = jnp.dot(x_ref[...], w_ref[...].astype(BF16), preferred_element_type=F32)
        o_ref[...] = acc + refs[n_lhs + 1][...] if residual else acc

    def later(x_ref):
        o_ref[...] += jnp.dot(x_ref[...], w_ref[...].astype(BF16), preferred_element_type=F32)

    pl.when(k == 0)(functools.partial(first, x_refs[0]))
    if n_lhs == 1:
        pl.when(k > 0)(functools.partial(later, x_refs[0]))
    else:
        for p in range(1, n_lhs):
            pl.when(k == p)(functools.partial(later, x_refs[p]))


def _w_tile_bytes(w):
    return 2 * w.dtype.itemsize + (2 if w.dtype != BF16 else 0)


def _k_tile(k, x_bytes_per_k, w_bytes_per_k, budget):
    units = k // LANE
    for parts in range(1, units + 1):
        if units % parts == 0 and (x_bytes_per_k + w_bytes_per_k) * (units // parts) * LANE <= budget:
            return (units // parts) * LANE
    raise ValueError(f"no K tile for {k}")


class Stacked(NamedTuple):
    w: jax.Array
    layer: int


def _unstack(w):
    return (w.w, w.layer) if isinstance(w, Stacked) else (w, None)


def _w_spec(block, index_map, layer):
    if layer is None:
        return pl.BlockSpec(block, index_map)
    return pl.BlockSpec((None,) + block, lambda *g: (layer,) + index_map(*g))


def _mm(xs, w, res=None):
    xs = list(xs) if isinstance(xs, (list, tuple)) else [xs]
    m = xs[0].shape[0]
    w, layer = _unstack(w)
    k, n = w.shape[-2:]
    tm = min(m, MAX_ROW_TILE)
    n_out_bufs = 4 if res is not None else 2
    widest = 1024 if w.dtype == BF16 else 512
    for tn in (min(n, widest), min(n, widest // 2)):
        budget = VMEM_LIMIT_BYTES - n_out_bufs * tm * tn * 4 - (4 << 20)
        x_per_k = 2 * 2 * tm * len(xs)
        w_per_k = _w_tile_bytes(w) * tn
        tk = _k_tile(k, x_per_k, w_per_k, budget) if len(xs) == 1 else k // len(xs)
        if tk == k or tk >= 2048:
            break
    if len(xs) == 1:
        x_specs = [pl.BlockSpec((tm, tk), lambda i, j, kk: (i, kk))]
    else:
        assert all(x.shape[1] == tk for x in xs) and (x_per_k + w_per_k) * tk <= budget
        x_specs = [pl.BlockSpec((tm, tk), lambda i, j, kk: (i, 0)) for _ in xs]
    assert m % tm == 0 and k % tk == 0 and tk % LANE == 0
    tile = pl.BlockSpec((tm, tn), lambda i, j, kk: (i, j))
    return pl.pallas_call(
        functools.partial(_mm_kernel, n_lhs=len(xs), residual=res is not None),
        grid=(m // tm, pl.cdiv(n, tn), k // tk),
        in_specs=x_specs + [_w_spec((tk, tn), lambda i, j, kk: (kk, j), layer)] + ([tile] if res is not None else []),
        out_specs=tile,
        out_shape=jax.ShapeDtypeStruct((m, n), F32),
        compiler_params=pltpu.CompilerParams(
            dimension_semantics=("arbitrary", "arbitrary", "arbitrary"),
            vmem_limit_bytes=VMEM_LIMIT_BYTES),
        name="mm",
    )(*xs, w, *([res] if res is not None else []))


FFN_TILE = 256


def _swiglu_kernel(x_ref, wg_ref, wu_ref, o_ref):
    x = x_ref[...]
    gate = jnp.dot(x, wg_ref[...].astype(BF16), preferred_element_type=F32)
    up = jnp.dot(x, wu_ref[...].astype(BF16), preferred_element_type=F32)
    o_ref[...] = (gate * jax.nn.sigmoid(gate) * up).astype(o_ref.dtype)


def _mm_swiglu(x, w):
    m, k = x.shape
    w, layer = w
    hidden = w.shape[2] // 2
    tm = min(m, MAX_ROW_TILE)
    nj = hidden // FFN_TILE
    assert m % tm == 0 and hidden % FFN_TILE == 0 and w.shape[1] == k
    return pl.pallas_call(
        _swiglu_kernel,
        grid=(m // tm, nj),
        in_specs=[pl.BlockSpec((tm, k), lambda i, j: (i, 0)),
                  _w_spec((k, FFN_TILE), lambda i, j: (0, j), layer),
                  _w_spec((k, FFN_TILE), lambda i, j: (0, nj + j), layer)],
        out_specs=pl.BlockSpec((tm, FFN_TILE), lambda i, j: (i, j)),
        out_shape=jax.ShapeDtypeStruct((m, hidden), BF16),
        compiler_params=pltpu.CompilerParams(dimension_semantics=("arbitrary", "arbitrary"),
                                             vmem_limit_bytes=VMEM_LIMIT_BYTES),
        name="mm_swiglu",
    )(x, w, w)


PLE_COL_TILE = 512


def _ple_kernel(t_ref, wg_ref, p_ref, wu_ref, res_ref, o_ref):
    gate = jnp.dot(t_ref[...], wg_ref[...].astype(BF16), preferred_element_type=F32)
    up = jnp.dot(p_ref[...], wu_ref[...].astype(BF16), preferred_element_type=F32)
    o_ref[...] = res_ref[...] + up * jax.nn.sigmoid(gate)


def _mm_ple(t, w_gate, p, w_up, res):
    m, k = t.shape
    w_gate, layer = w_gate
    w_up, layer_up = w_up
    assert layer == layer_up
    n = w_gate.shape[2]
    kp = p.shape[1]
    tm = min(m, MAX_ROW_TILE)
    tn = PLE_COL_TILE
    assert m % tm == 0 and n % tn == 0
    tile = pl.BlockSpec((tm, tn), lambda i, j: (i, j))
    return pl.pallas_call(
        _ple_kernel,
        grid=(m // tm, n // tn),
        in_specs=[pl.BlockSpec((tm, k), lambda i, j: (i, 0)), _w_spec((k, tn), lambda i, j: (0, j), layer),
                  pl.BlockSpec((tm, kp), lambda i, j: (i, 0)), _w_spec((kp, tn), lambda i, j: (0, j), layer), tile],
        out_specs=tile,
        out_shape=jax.ShapeDtypeStruct((m, n), F32),
        compiler_params=pltpu.CompilerParams(dimension_semantics=("arbitrary", "arbitrary"),
                                             vmem_limit_bytes=VMEM_LIMIT_BYTES),
        name="mm_ple",
    )(t, w_gate, p, w_up, res)


NSA_TQ = 256
NSA_KEY_BUCKET = 512


def _dot_nt(a, b):
    return lax.dot_general(a, b, (((1,), (1,)), ((), ())), preferred_element_type=F32)


def _group_attend(q4, k, v, mask, scale, tq):
    s4 = _dot_nt(q4, k) * scale
    es, dens = [], []
    for r in range(NSA_GROUP):
        s = jnp.where(mask, s4[r * tq:(r + 1) * tq], NEG)
        e = jnp.exp(s - jnp.max(s, axis=-1, keepdims=True))
        dens.append(jnp.sum(e, axis=-1, keepdims=True))
        es.append(e.astype(BF16))
    o4 = jnp.dot(jnp.concatenate(es, axis=0), v, preferred_element_type=F32)
    return [o4[r * tq:(r + 1) * tq] / dens[r] for r in range(NSA_GROUP)]


def _nsa_prompt_kernel(q_ref, kc_ref, vc_ref, ks_ref, vs_ref, kw_ref, vw_ref, gate_ref, cover_ref, expand_ref,
                       o_ref, slc_s, *, seq):
    tq = NSA_TQ
    hd = NSA_HEAD_DIM
    n_cmp = seq // CMP_STRIDE - CMP_BLOCK // CMP_STRIDE + 1
    n_slc = seq // SLC_BLOCK
    span = WINDOW + tq
    scale = hd ** -0.5
    q0 = pl.program_id(2) * tq
    pos = q0 + lax.broadcasted_iota(jnp.int32, (tq, 1), 0)
    lane = lax.broadcasted_iota(jnp.int32, (tq, LANE), 1)
    q4 = jnp.concatenate([q_ref[:, r * hd:(r + 1) * hd].astype(BF16) for r in range(NSA_GROUP)], axis=0)

    cmp_ok = (lane * CMP_STRIDE + (CMP_BLOCK - 1) <= pos) & (lane < n_cmp)
    s4 = _dot_nt(q4, kc_ref[...]) * scale
    psum = jnp.zeros((tq, LANE), F32)
    ps = []
    for r in range(NSA_GROUP):
        s = jnp.where(cmp_ok, s4[r * tq:(r + 1) * tq], NEG)
        e = jnp.where(cmp_ok, jnp.exp(s - jnp.max(s, axis=-1, keepdims=True)), 0.0)
        den = jnp.sum(e, axis=-1, keepdims=True)
        p = e / jnp.where(den > 0.0, den, 1.0)
        psum = psum + p
        ps.append(p.astype(BF16))
    o_cmp4 = jnp.dot(jnp.concatenate(ps, axis=0), vc_ref[...], preferred_element_type=F32)

    score = jnp.dot(psum, cover_ref[...], preferred_element_type=F32, precision=lax.Precision.HIGHEST)
    cur = pos // SLC_BLOCK
    forced = (lane == 0) | (lane == cur) | (lane == cur - 1)
    causal = lane * SLC_BLOCK <= pos
    score = jnp.where(forced, FORCE_SCORE, jnp.where(causal, score, -1.0))
    score = jnp.where(lane < n_slc, score, -2.0)
    rank = jnp.zeros((tq, LANE), jnp.int32)
    for i in range(n_slc):
        ci = score[:, i:i + 1]
        beats = (ci > score) | ((ci == score) & (lane > i))
        rank = rank + beats.astype(jnp.int32)
    sel = ((rank < min(N_SELECT, n_slc)) & (lane < n_slc)).astype(BF16)

    def selected(nk):
        sel_keys = jnp.dot(sel, expand_ref[:, :nk], preferred_element_type=F32)
        kpos = lax.broadcasted_iota(jnp.int32, (tq, nk), 1)
        outs = _group_attend(q4, ks_ref[:nk, :].astype(BF16), vs_ref[:nk, :].astype(BF16),
                             (sel_keys > 0.5) & (kpos <= pos), scale, tq)
        for r in range(NSA_GROUP):
            slc_s[r] = outs[r]

    n_bucket = (q0 + tq + NSA_KEY_BUCKET - 1) // NSA_KEY_BUCKET
    for nb in range(1, seq // NSA_KEY_BUCKET + 1):
        pl.when(n_bucket == nb)(functools.partial(selected, nb * NSA_KEY_BUCKET))

    w0 = pl.multiple_of(jnp.maximum(q0 - WINDOW, 0), tq)
    wpos = w0 + lax.broadcasted_iota(jnp.int32, (tq, span), 1)
    o_swa = _group_attend(q4, kw_ref[pl.ds(w0, span), :].astype(BF16), vw_ref[pl.ds(w0, span), :].astype(BF16),
                          (wpos <= pos) & (pos - wpos < WINDOW), scale, tq)

    for r in range(NSA_GROUP):
        g = gate_ref[:, 3 * r:3 * r + 3]
        o = g[:, 0:1] * o_cmp4[r * tq:(r + 1) * tq] + g[:, 1:2] * slc_s[r] + g[:, 2:3] * o_swa[r]
        o_ref[:, r * hd:(r + 1) * hd] = o.astype(o_ref.dtype)


def _nsa_prompt(proj, kc, vc, gates, batch, seq):
    assert seq % NSA_TQ == 0 and seq // SLC_BLOCK <= LANE and seq // CMP_STRIDE <= LANE + 1
    assert seq % NSA_KEY_BUCKET == 0 and seq >= WINDOW + NSA_TQ and WINDOW % NSA_TQ == 0
    nt = seq // NSA_TQ
    n_cmp = seq // CMP_STRIDE - CMP_BLOCK // CMP_STRIDE + 1
    n_slc = seq // SLC_BLOCK
    c0 = np.arange(LANE)[:, None] * CMP_STRIDE
    s0 = np.arange(LANE)[None, :] * SLC_BLOCK
    cover = np.clip(np.minimum(c0 + CMP_BLOCK, s0 + SLC_BLOCK) - np.maximum(c0, s0), 0, None) / CMP_BLOCK
    cover = cover * (np.arange(LANE)[:, None] < n_cmp) * (np.arange(LANE)[None, :] < n_slc)
    expand = (np.arange(seq)[None, :] // SLC_BLOCK == np.arange(LANE)[:, None])
    hd = NSA_HEAD_DIM
    gw = NSA_GROUP * hd

    def kv_spec(slot):
        return pl.BlockSpec((seq, hd), lambda b, g, t: (b, NSA_Q // hd + slot * NSA_KV_HEADS + g))

    cmp_spec = pl.BlockSpec((None, None, LANE, hd), lambda b, g, t: (b, g, 0, 0))
    return pl.pallas_call(
        functools.partial(_nsa_prompt_kernel, seq=seq),
        grid=(batch, NSA_KV_HEADS, nt),
        in_specs=[pl.BlockSpec((NSA_TQ, gw), lambda b, g, t: (b * nt + t, g)),
                  cmp_spec, cmp_spec, kv_spec(2), kv_spec(3), kv_spec(4), kv_spec(5),
                  pl.BlockSpec((NSA_TQ, LANE), lambda b, g, t: (b * nt + t, g)),
                  pl.BlockSpec((LANE, LANE), lambda b, g, t: (0, 0)),
                  pl.BlockSpec((LANE, seq), lambda b, g, t: (0, 0))],
        out_specs=pl.BlockSpec((NSA_TQ, gw), lambda b, g, t: (b * nt + t, g)),
        out_shape=jax.ShapeDtypeStruct((batch * seq, NSA_Q), BF16),
        scratch_shapes=[pltpu.VMEM((NSA_GROUP, NSA_TQ, hd), F32)],
        compiler_params=pltpu.CompilerParams(
            dimension_semantics=("arbitrary", "arbitrary", "arbitrary"),
            vmem_limit_bytes=VMEM_LIMIT_BYTES),
        name="nsa_prompt",
    )(proj, kc, vc, proj, proj, proj, proj, gates, jnp.asarray(cover, F32), jnp.asarray(expand, BF16))


def _compress_kernel(x_ref, w1_ref, w2_ref, pe_ref, o_ref, *, seq):
    n_sub = seq // CMP_STRIDE
    n_cmp = n_sub - CMP_BLOCK // CMP_STRIDE + 1
    acc = jnp.zeros((n_sub, 2 * CMP_HIDDEN), F32)
    for r in range(CMP_STRIDE):
        xr = x_ref[pl.ds(r, n_sub, stride=CMP_STRIDE), :].astype(BF16)
        acc = acc + jnp.dot(xr, w1_ref[r], preferred_element_type=F32)
    hid = acc[:, :CMP_HIDDEN] + pltpu.roll(acc[:, CMP_HIDDEN:], n_sub - 1, 0) + pe_ref[...]
    c = jnp.dot((hid * jax.nn.sigmoid(hid)).astype(BF16), w2_ref[...], preferred_element_type=F32)
    row = lax.broadcasted_iota(jnp.int32, c.shape, 0)
    o_ref[...] = jnp.where(row < n_cmp, c, 0.0).astype(o_ref.dtype)


def _compress_prompt(proj, w1, w2, pe, batch, seq):
    assert CMP_BLOCK == 2 * CMP_STRIDE and seq // CMP_STRIDE == LANE and NSA_HEAD_DIM == LANE
    hd = NSA_HEAD_DIM
    w1p = w1.reshape(2, 2, CMP_STRIDE, hd, CMP_HIDDEN).transpose(0, 2, 3, 1, 4)
    w1p = w1p.reshape(2, CMP_STRIDE, hd, 2 * CMP_HIDDEN).astype(BF16)
    pe_h = jnp.einsum('sc,sch->sh', pe.reshape(2, -1), w1)[:, None, :]
    return pl.pallas_call(
        functools.partial(_compress_kernel, seq=seq),
        grid=(batch, 2, NSA_KV_HEADS),
        in_specs=[pl.BlockSpec((seq, hd), lambda b, s, g: (b, NSA_Q // hd + s * NSA_KV_HEADS + g)),
                  pl.BlockSpec((None, CMP_STRIDE, hd, 2 * CMP_HIDDEN), lambda b, s, g: (s, 0, 0, 0)),
                  pl.BlockSpec((None, CMP_HIDDEN, hd), lambda b, s, g: (s, 0, 0)),
                  pl.BlockSpec((None, 1, CMP_HIDDEN), lambda b, s, g: (s, 0, 0))],
        out_specs=pl.BlockSpec((None, None, None, LANE, hd), lambda b, s, g: (s, b, g, 0, 0)),
        out_shape=jax.ShapeDtypeStruct((2, batch, NSA_KV_HEADS, LANE, hd), BF16),
        compiler_params=pltpu.CompilerParams(dimension_semantics=("arbitrary", "arbitrary", "arbitrary"),
                                             vmem_limit_bytes=VMEM_LIMIT_BYTES),
        name="compress_prompt",
    )(proj, w1p, w2.astype(BF16), pe_h)


CMP_PAGES = 8
DEC_QROWS = 8


def _compress_paged_kernel(pt_ref, *refs, n_cmp):
    pages = refs[:CMP_PAGES + 1]
    w1_ref, w2_ref, pe_ref, o_ref = refs[CMP_PAGES + 1:]
    per_row = 4 * NSA_KV_HEADS
    sub_pp = PAGE_SIZE // CMP_STRIDE
    own = CMP_PAGES * sub_pp
    tot = own + sub_pp
    first = pl.program_id(1) * own
    for slot in range(2):
        acc = jnp.zeros((NSA_KV_HEADS * tot, 2 * CMP_HIDDEN), F32)
        for r in range(CMP_STRIDE):
            xr = jnp.concatenate(
                [page[pl.ds(r * per_row + slot * NSA_KV_HEADS + g, sub_pp, stride=CMP_STRIDE * per_row), :]
                 for g in range(NSA_KV_HEADS) for page in pages], axis=0).astype(BF16)
            acc = acc + jnp.dot(xr, w1_ref[slot, r], preferred_element_type=F32)
        for g in range(NSA_KV_HEADS):
            blk = acc[g * tot:(g + 1) * tot]
            hid = (blk[:, :CMP_HIDDEN] + pltpu.roll(blk[:, CMP_HIDDEN:], tot - 1, 0))[:own] + pe_ref[slot]
            c = jnp.dot((hid * jax.nn.sigmoid(hid)).astype(BF16), w2_ref[slot], preferred_element_type=F32)
            sub = first + lax.broadcasted_iota(jnp.int32, c.shape, 0)
            o_ref[slot, g] = jnp.where(sub < n_cmp, c, 0.0).astype(o_ref.dtype)


def _compress_paged(cache, page0, page_table, w1, w2, pe):
    batch, n_pages = page_table.shape
    hd = NSA_HEAD_DIM
    sub_pp = PAGE_SIZE // CMP_STRIDE
    n_sub = n_pages * sub_pp
    assert n_pages % CMP_PAGES == 0 and PAGE_SIZE % CMP_STRIDE == 0 and CMP_BLOCK == 2 * CMP_STRIDE
    w1p = w1.reshape(2, 2, CMP_STRIDE, hd, CMP_HIDDEN).transpose(0, 2, 3, 1, 4)
    w1p = w1p.reshape(2, CMP_STRIDE, hd, 2 * CMP_HIDDEN).astype(BF16)
    pe_h = jnp.einsum('sc,sch->sh', pe.reshape(2, -1), w1)[:, None, :]

    def page_spec(k):
        return pl.BlockSpec((PAGE_SIZE * 4 * NSA_KV_HEADS, hd),
                            lambda b, s, pt: (page0 + pt[b, jnp.minimum(s * CMP_PAGES + k, n_pages - 1)], 0))

    full = lambda shape: pl.BlockSpec(shape, lambda b, s, pt: (0,) * len(shape))
    return pl.pallas_call(
        functools.partial(_compress_paged_kernel, n_cmp=n_sub - 1),
        grid_spec=pltpu.PrefetchScalarGridSpec(
            num_scalar_prefetch=1,
            grid=(batch, n_pages // CMP_PAGES),
            in_specs=[page_spec(k) for k in range(CMP_PAGES + 1)] + [
                full((2, CMP_STRIDE, hd, 2 * CMP_HIDDEN)), full((2, CMP_HIDDEN, hd)), full((2, 1, CMP_HIDDEN))],
            out_specs=pl.BlockSpec((2, None, NSA_KV_HEADS, CMP_PAGES * sub_pp, hd), lambda b, s, pt: (0, b, 0, s, 0))),
        out_shape=jax.ShapeDtypeStruct((2, batch, NSA_KV_HEADS, n_sub, hd), BF16),
        compiler_params=pltpu.CompilerParams(dimension_semantics=("arbitrary", "arbitrary"),
                                             vmem_limit_bytes=VMEM_LIMIT_BYTES),
        name="compress_paged",
    )(page_table, *([cache.reshape(-1, hd)] * (CMP_PAGES + 1)), w1p, w2.astype(BF16), pe_h)


def _decode_select_kernel(q_ref, kc_ref, vc_ref, cover_ref, rsum_ref, ocmp_ref, idx_ref, *, n_cmp, n_past_blocks):
    nq = q_ref.shape[0]
    n_sub = kc_ref.shape[0]
    scale = NSA_HEAD_DIM ** -0.5
    pos = PAST_LEN + lax.broadcasted_iota(jnp.int32, (nq, 1), 0) // DEC_QROWS
    n = lax.broadcasted_iota(jnp.int32, (nq, n_sub), 1)
    ok = (n * CMP_STRIDE + (CMP_BLOCK - 1) <= pos) & (n < n_cmp)
    s = jnp.where(ok, _dot_nt(q_ref[...].astype(BF16), kc_ref[...]) * scale, NEG)
    e = jnp.where(ok, jnp.exp(s - jnp.max(s, axis=-1, keepdims=True)), 0.0)
    den = jnp.sum(e, axis=-1, keepdims=True)
    p = e / jnp.where(den > 0.0, den, 1.0)
    ocmp_ref[...] = jnp.dot(p.astype(BF16), vc_ref[...], preferred_element_type=F32)

    score = _dot_sel(_sel_dot(rsum_ref[...], p), cover_ref[...])
    post = PAST_LEN + lax.broadcasted_iota(jnp.int32, (score.shape[0], 1), 0)
    lane = lax.broadcasted_iota(jnp.int32, score.shape, 1)
    cur = post // SLC_BLOCK
    forced = (lane == 0) | (lane == cur) | (lane == cur - 1)
    score = jnp.where(forced, FORCE_SCORE, jnp.where(lane * SLC_BLOCK <= post, score, -1.0))
    rank = jnp.zeros(score.shape, jnp.int32)
    for i in range(n_past_blocks):
        ci = score[:, i:i + 1]
        rank = rank + ((ci > score) | ((ci == score) & (lane > i))).astype(jnp.int32)
    out_lane = lax.broadcasted_iota(jnp.int32, idx_ref.shape, 1)
    out = jnp.zeros(idx_ref.shape, jnp.int32)
    for k in range(N_SELECT - 1):
        idx_k = jnp.sum(jnp.where(rank == k, lane.astype(F32), 0.0), axis=-1, keepdims=True)
        out = jnp.where(out_lane == k, idx_k.astype(jnp.int32), out)
    idx_ref[...] = out


def _decode_select(q_pad, cmp, ls):
    batch = q_pad.shape[0]
    nq = ls * DEC_QROWS
    n_sub = cmp.shape[3]
    n_past_blocks = PAST_LEN // SLC_BLOCK
    assert PAST_LEN % SLC_BLOCK == 0 and ls <= min(SLC_BLOCK, 8) and n_past_blocks >= N_SELECT
    c0 = np.arange(n_sub)[:, None] * CMP_STRIDE
    s0 = np.arange(n_past_blocks)[None, :] * SLC_BLOCK
    cover = np.clip(np.minimum(c0 + CMP_BLOCK, s0 + SLC_BLOCK) - np.maximum(c0, s0), 0, None) / CMP_BLOCK
    cover = cover * (np.arange(n_sub)[:, None] < n_sub - 1)
    rows = np.arange(nq)[None, :]
    rsum = (rows // DEC_QROWS == np.arange(8)[:, None]) & (rows % DEC_QROWS < NSA_GROUP)
    hd = NSA_HEAD_DIM
    q_spec = pl.BlockSpec((None, None, nq, hd), lambda b, g: (b, g, 0, 0))
    return pl.pallas_call(
        functools.partial(_decode_select_kernel, n_cmp=n_sub - 1, n_past_blocks=n_past_blocks),
        grid=(batch, NSA_KV_HEADS),
        in_specs=[q_spec,
                  pl.BlockSpec((None, None, None, n_sub, hd), lambda b, g: (0, b, g, 0, 0)),
                  pl.BlockSpec((None, None, None, n_sub, hd), lambda b, g: (1, b, g, 0, 0)),
                  pl.BlockSpec((n_sub, n_past_blocks), lambda b, g: (0, 0)),
                  pl.BlockSpec((8, nq), lambda b, g: (0, 0))],
        out_specs=[q_spec, pl.BlockSpec((None, None, 8, LANE), lambda b, g: (b, g, 0, 0))],
        out_shape=[jax.ShapeDtypeStruct((batch, NSA_KV_HEADS, nq, hd), F32),
                   jax.ShapeDtypeStruct((batch, NSA_KV_HEADS, 8, LANE), jnp.int32)],
        compiler_params=pltpu.CompilerParams(dimension_semantics=("arbitrary", "arbitrary"),
                                             vmem_limit_bytes=VMEM_LIMIT_BYTES),
        name="decode_select",
    )(q_pad, cmp, cmp, jnp.asarray(cover, BF16), jnp.asarray(rsum, BF16))


def _decode_attend_kernel(pt_ref, idx_ref, q_ref, ocmp_ref, gate_ref, new_ref, kbuf_ref, vbuf_ref, cache_ref,
                          o_ref, kg_s, vg_s, sem, *, ls, page0):
    b = pl.program_id(0)
    g = pl.program_id(1)
    n_sel = N_SELECT - 1
    hd = NSA_HEAD_DIM
    scale = hd ** -0.5

    def gather_copies():
        for t in range(ls):
            for k in range(n_sel):
                j = idx_ref[((b * NSA_KV_HEADS + g) * ls + t) * n_sel + k]
                page = page0 + pt_ref[b, j // (PAGE_SIZE // SLC_BLOCK)]
                row0 = (j % (PAGE_SIZE // SLC_BLOCK)) * SLC_BLOCK
                for which, (slot, dst) in enumerate(((2, kg_s), (3, vg_s))):
                    yield pltpu.make_async_copy(
                        cache_ref.at[page, pl.ds(row0, SLC_BLOCK), slot * NSA_KV_HEADS + g, :],
                        dst.at[t, k], sem.at[which])

    for cp in gather_copies():
        cp.start()

    nq = ls * DEC_QROWS
    new = new_ref[...].astype(BF16)
    q = q_ref[...].astype(BF16)
    tok = lax.broadcasted_iota(jnp.int32, (nq, 1), 0) // DEC_QROWS
    wb = kbuf_ref.shape[0]
    i_buf = lax.broadcasted_iota(jnp.int32, (nq, wb), 1)
    i_new = lax.broadcasted_iota(jnp.int32, (nq, new.shape[1]), 1)
    m_new = (i_new <= tok) & (i_new < ls)
    m_buf = (wb - i_buf + tok) < WINDOW
    s_b = jnp.where(m_buf, _dot_nt(q, kbuf_ref[...].astype(BF16)) * scale, NEG)
    s_n = jnp.where(m_new, _dot_nt(q, new[2]) * scale, NEG)
    mx = jnp.maximum(jnp.max(s_b, axis=-1, keepdims=True), jnp.max(s_n, axis=-1, keepdims=True))
    e_b = jnp.exp(s_b - mx)
    e_n = jnp.exp(s_n - mx)
    den = jnp.sum(e_b, axis=-1, keepdims=True) + jnp.sum(e_n, axis=-1, keepdims=True)
    o_swa = (jnp.dot(e_b.astype(BF16), vbuf_ref[...].astype(BF16), preferred_element_type=F32)
             + jnp.dot(e_n.astype(BF16), new[3], preferred_element_type=F32)) / den

    for cp in gather_copies():
        cp.wait()

    gates = gate_ref[...]
    for t in range(ls):
        rows = slice(t * DEC_QROWS, (t + 1) * DEC_QROWS)
        qt = q[rows]
        s_p = _dot_nt(qt, kg_s[t].reshape(n_sel * SLC_BLOCK, hd).astype(BF16)) * scale
        s_n = jnp.where(m_new[rows], _dot_nt(qt, new[0]) * scale, NEG)
        mx = jnp.maximum(jnp.max(s_p, axis=-1, keepdims=True), jnp.max(s_n, axis=-1, keepdims=True))
        e_p = jnp.exp(s_p - mx)
        e_n = jnp.exp(s_n - mx)
        den = jnp.sum(e_p, axis=-1, keepdims=True) + jnp.sum(e_n, axis=-1, keepdims=True)
        o_slc = (jnp.dot(e_p.astype(BF16), vg_s[t].reshape(n_sel * SLC_BLOCK, hd).astype(BF16),
                         preferred_element_type=F32)
                 + jnp.dot(e_n.astype(BF16), new[1], preferred_element_type=F32)) / den
        gt = gates[rows]
        o_ref[rows, :] = gt[:, 0:1] * ocmp_ref[rows, :] + gt[:, 1:2] * o_slc + gt[:, 2:3] * o_swa[rows]


def _decode_attend(q_pad, o_cmp, gates, new_rows, swa_buf, cache, page0, page_table, idx, ls):
    batch = q_pad.shape[0]
    nq = ls * DEC_QROWS
    hd = NSA_HEAD_DIM
    wb = swa_buf.shape[1]
    assert wb == WINDOW and PAGE_SIZE % SLC_BLOCK == 0
    n_sel = N_SELECT - 1
    q_spec = pl.BlockSpec((None, None, nq, hd), lambda b, g, pt, ix: (b, g, 0, 0))

    def buf_spec(slot):
        return pl.BlockSpec((None, wb, hd), lambda b, g, pt, ix: (b, 0, slot * NSA_KV_HEADS + g))

    return pl.pallas_call(
        functools.partial(_decode_attend_kernel, ls=ls, page0=page0),
        grid_spec=pltpu.PrefetchScalarGridSpec(
            num_scalar_prefetch=2,
            grid=(batch, NSA_KV_HEADS),
            in_specs=[q_spec, q_spec,
                      pl.BlockSpec((None, None, nq, LANE), lambda b, g, pt, ix: (b, g, 0, 0)),
                      pl.BlockSpec((None, None, 4, 8, hd), lambda b, g, pt, ix: (b, g, 0, 0, 0)),
                      buf_spec(0), buf_spec(1),
                      pl.BlockSpec(memory_space=pl.ANY)],
            out_specs=q_spec,
            scratch_shapes=[pltpu.VMEM((ls, n_sel, SLC_BLOCK, hd), F32), pltpu.VMEM((ls, n_sel, SLC_BLOCK, hd), F32),
                            pltpu.SemaphoreType.DMA((2,))]),
        out_shape=jax.ShapeDtypeStruct((batch, NSA_KV_HEADS, nq, hd), F32),
        compiler_params=pltpu.CompilerParams(dimension_semantics=("arbitrary", "arbitrary"),
                                             vmem_limit_bytes=VMEM_LIMIT_BYTES),
        name="decode_attend",
    )(page_table, idx, q_pad, o_cmp, gates, new_rows,
      swa_buf.reshape(batch, wb, 2 * NSA_KV_HEADS * hd), swa_buf.reshape(batch, wb, 2 * NSA_KV_HEADS * hd),
      cache)


HG_SLAB = 256
HG_UNROLL = 8


def _hgrn_kernel(hq_ref, hf_ref, hi_ref, hg_ref, lb_ref, gn_ref, s0_ref, tri_ref, o_ref, s_ref,
                 qs_s, el_s, o_s, u_s, st_s, *, seq):
    c = HG_CHUNK
    per_slab = HG_SLAB // c
    lb = lb_ref[...]

    def prep(i, carry):
        rows = pl.ds(pl.multiple_of(i * HG_SLAB, HG_SLAB), HG_SLAB)
        hq = hq_ref[rows, :]
        q = hq * jax.nn.sigmoid(hq)
        f = lb + (1.0 - lb) * jax.nn.sigmoid(hf_ref[rows, :])
        k = 1.0 - f
        v = hi_ref[rows, :].astype(BF16)
        bc = _sel_dot(tri_ref[0], jnp.log(f))
        bc3 = bc.reshape(per_slab, c, LANE)
        mid = (c - 1) // 2
        bm = jnp.broadcast_to(bc3[:, mid:mid + 1, :], bc3.shape).reshape(HG_SLAB, LANE)
        bl = jnp.broadcast_to(bc3[:, c - 1:c, :], bc3.shape).reshape(HG_SLAB, LANE)
        a = _dot_nt((q * jnp.exp(bc - bm)).astype(BF16), (k * jnp.exp(bm - bc)).astype(BF16))
        a = jnp.where(tri_ref[0] > 0, a, 0.0)
        o_s[rows, :] = jnp.dot(a.astype(BF16), v, preferred_element_type=F32)
        qs_s[rows, :] = (q * jnp.exp(bc)).astype(qs_s.dtype)
        kd = (k * jnp.exp(bl - bc)).astype(BF16)
        el = jnp.exp(bl)
        for j in range(per_slab):
            ch = slice(j * c, (j + 1) * c)
            u_s[i * per_slab + j] = lax.dot_general(v[ch], kd[ch], (((0,), (0,)), ((), ())),
                                                    preferred_element_type=F32)
            el_s[pl.ds(i * per_slab + j, 1), :] = el[j * c:j * c + 1]
        return carry

    lax.fori_loop(0, seq // HG_SLAB, prep, 0)

    def scan(ci, st):
        st_s[ci] = st.astype(st_s.dtype)
        return st * el_s[pl.ds(ci, 1), :] + u_s[ci]

    st = lax.fori_loop(0, seq // c, scan, s0_ref[...].T, unroll=HG_UNROLL)
    s_ref[...] = st.T

    gn = gn_ref[...]

    def finish(i, carry):
        rows = pl.ds(pl.multiple_of(i * HG_SLAB, HG_SLAB), HG_SLAB)
        qs = qs_s[rows, :]
        inter = [_dot_nt(qs[j * c:(j + 1) * c], st_s[i * per_slab + j]) for j in range(per_slab)]
        o = o_s[rows, :] + jnp.concatenate(inter, axis=0)
        hg = hg_ref[rows, :]
        o = o * lax.rsqrt(jnp.mean(o * o, axis=-1, keepdims=True) + EPS)
        o_ref[rows, :] = (o * gn * (hg * jax.nn.sigmoid(hg))).astype(o_ref.dtype)
        return carry

    lax.fori_loop(0, seq // HG_SLAB, finish, 0)


def _hgrn_tri():
    i = np.arange(HG_SLAB)[:, None]
    j = np.arange(HG_SLAB)[None, :]
    cum = ((i // HG_CHUNK) == (j // HG_CHUNK)) & (j <= i)
    return jnp.asarray(cum[None], BF16)


def _hgrn_prompt(proj, col0, lb, g_norm, s0, batch, seq):
    assert seq % HG_SLAB == 0 and col0 % LANE == 0 and HG_DK == LANE and HG_DV == LANE
    c0 = col0 // LANE

    def col_spec(group):
        return pl.BlockSpec((seq, LANE), lambda b, h: (b, c0 + group * HG_HEADS + h))

    vec_spec = pl.BlockSpec((1, LANE), lambda b, h: (0, h))
    st_spec = pl.BlockSpec((None, None, HG_DK, HG_DV), lambda b, h: (b, h, 0, 0))
    return pl.pallas_call(
        functools.partial(_hgrn_kernel, seq=seq),
        grid=(batch, HG_HEADS),
        in_specs=[col_spec(0), col_spec(1), col_spec(2), col_spec(3), vec_spec, vec_spec, st_spec,
                  pl.BlockSpec((1, HG_SLAB, HG_SLAB), lambda b, h: (0, 0, 0))],
        out_specs=[pl.BlockSpec((seq, LANE), lambda b, h: (b, h)), st_spec],
        out_shape=[jax.ShapeDtypeStruct((batch * seq, HG_V_W), BF16),
                   jax.ShapeDtypeStruct((batch, HG_HEADS, HG_DK, HG_DV), F32)],
        scratch_shapes=[pltpu.VMEM((seq, LANE), BF16), pltpu.VMEM((seq // HG_CHUNK, LANE), F32),
                        pltpu.VMEM((seq, LANE), F32), pltpu.VMEM((seq // HG_CHUNK, HG_DV, HG_DK), F32),
                        pltpu.VMEM((seq // HG_CHUNK, HG_DV, HG_DK), BF16)],
        compiler_params=pltpu.CompilerParams(
            dimension_semantics=("arbitrary", "arbitrary"),
            vmem_limit_bytes=VMEM_LIMIT_BYTES),
        name="hgrn_prompt",
    )(proj, proj, proj, proj, lb, g_norm, s0, _hgrn_tri())


SSD_TL = 256
SSD_GW = SSM_D_INNER // SSM_GROUPS
SSD_HPG = SSM_HEADS // SSM_GROUPS


def _split3(x):
    hi = x.astype(BF16)
    r1 = x - hi.astype(F32)
    mid = r1.astype(BF16)
    lo = (r1 - mid.astype(F32)).astype(BF16)
    return hi, mid, lo


def _sel_dot(sel, x):
    hi, mid, lo = _split3(x)
    d = functools.partial(jnp.dot, preferred_element_type=F32)
    return d(sel, hi) + d(sel, mid) + d(sel, lo)


def _dot_sel(x, sel):
    hi, mid, lo = _split3(x)
    d = functools.partial(jnp.dot, preferred_element_type=F32)
    return d(hi, sel) + d(mid, sel) + d(lo, sel)


def _causal_conv_silu(x, prev8, w, bias):
    row8 = lax.broadcasted_iota(jnp.int32, prev8.shape, 0)
    acc = bias
    for k in range(SSM_CONV - 1, 0, -1):
        r = pltpu.roll(x, k, 0)
        top = jnp.where(row8 < k, pltpu.roll(prev8, k, 0), r[:8])
        acc = acc + jnp.concatenate([top, r[8:]], axis=0) * w[SSM_CONV - 1 - k:SSM_CONV - k]
    acc = acc + x * w[SSM_CONV - 1:SSM_CONV]
    return acc * jax.nn.sigmoid(acc)


def _ssd_kernel(z_ref, x_ref, b_ref, c_ref, dt_ref, cx0_ref, cb0_ref, cc0_ref, s0_ref,
                wx_ref, wb_ref, wc_ref, bx_ref, bb_ref, bc_ref, dtb_ref, alog_ref, d_ref, gn_ref,
                tri_ref, hsel_ref, o_ref, s_ref,
                st_s, tx_s, tb_s, tc_s, xdt_s, xdd_s, ce_s, le_s, bm_s, cm_s, y_s):
    lt = pl.program_id(2)
    c = SSM_CHUNK
    tl = SSD_TL
    hp = SSM_HEAD_DIM

    @pl.when(lt == 0)
    def _():
        st_s[...] = s0_ref[...].T
        for tail, c0 in ((tx_s, cx0_ref), (tb_s, cb0_ref), (tc_s, cc0_ref)):
            tail[...] = jnp.zeros(tail.shape, F32)
            tail[8 - (SSM_CONV - 1):8, :] = c0[...]

    x_raw = x_ref[...]
    b_raw = b_ref[...]
    c_raw = c_ref[...]
    xs = _causal_conv_silu(x_raw, tx_s[...], wx_ref[...], bx_ref[...])
    bm_s[...] = _causal_conv_silu(b_raw, tb_s[...], wb_ref[...], bb_ref[...]).astype(bm_s.dtype)
    cm_s[...] = _causal_conv_silu(c_raw, tc_s[...], wc_ref[...], bc_ref[...]).astype(cm_s.dtype)
    tx_s[...] = x_raw[tl - 8:]
    tb_s[...] = b_raw[tl - 8:]
    tc_s[...] = c_raw[tl - 8:]

    dt = jax.nn.softplus(dt_ref[...] + dtb_ref[...])
    cum = _sel_dot(tri_ref[0], dt * -jnp.exp(alog_ref[...]))
    hsel = hsel_ref[...]
    dt_e = _dot_sel(dt, hsel)
    cum_e = _dot_sel(cum, hsel)
    last_e = _sel_dot(tri_ref[1], cum_e)
    xdt = xs * dt_e
    xdt_s[...] = xdt.astype(xdt_s.dtype)
    xdd_s[...] = (xdt * jnp.exp(last_e - cum_e)).astype(xdd_s.dtype)
    ce_s[...] = cum_e
    le_s[...] = jnp.exp(last_e)

    trow = lax.broadcasted_iota(jnp.int32, (c, SSD_GW), 0)
    lane_g = lax.broadcasted_iota(jnp.int32, (c, SSD_GW), 1)
    diag = (lane_g % hp) == trow
    t2 = lax.broadcasted_iota(jnp.int32, (c, LANE), 0)
    l2 = lax.broadcasted_iota(jnp.int32, (c, LANE), 1)
    tril2 = (l2 % hp) <= t2
    r3 = lax.broadcasted_iota(jnp.int32, (2 * c, LANE), 0)
    l3 = lax.broadcasted_iota(jnp.int32, (2 * c, LANE), 1)
    blockdiag = (r3 // c) == (l3 // hp)

    for ci in range(tl // c):
        rows = pl.ds(ci * c, c)
        cm = cm_s[rows, :]
        bm = bm_s[rows, :]
        cum_c = ce_s[rows, :]
        cum_row = jnp.sum(jnp.where(diag, cum_c, 0.0), axis=0, keepdims=True)
        cb2 = _dot_nt(cm, jnp.concatenate([bm, bm], axis=0))
        xdt_c = xdt_s[rows, :]
        st = st_s[...]
        y_inter = jnp.dot(cm, st.astype(cm.dtype), preferred_element_type=F32) * jnp.exp(cum_c)
        for j in range(SSD_GW // LANE):
            cols = slice(j * LANE, (j + 1) * LANE)
            seg = cum_c[:, cols] - cum_row[:, cols]
            w = (cb2 * jnp.where(tril2, jnp.exp(seg), 0.0)).astype(xdt_c.dtype)
            xj = xdt_c[:, cols]
            xbd = jnp.where(blockdiag, jnp.concatenate([xj, xj], axis=0), jnp.zeros_like(xj[:1, :1]))
            y_s[rows, cols] = y_inter[:, cols] + jnp.dot(w, xbd, preferred_element_type=F32)
        upd = lax.dot_general(bm, xdd_s[rows, :], (((0,), (0,)), ((), ())), preferred_element_type=F32)
        st_s[...] = st * le_s[pl.ds(ci * c, 1), :] + upd

    z = z_ref[...]
    y = (y_s[...] + d_ref[...] * xs) * (z * jax.nn.sigmoid(z))
    y = y * lax.rsqrt(jnp.mean(y * y, axis=-1, keepdims=True) + EPS)
    o_ref[...] = (y * gn_ref[...]).astype(o_ref.dtype)

    @pl.when(lt == pl.num_programs(2) - 1)
    def _():
        s_ref[...] = st_s[...].T


def _ssd_consts():
    i = np.arange(SSD_TL)[:, None]
    j = np.arange(SSD_TL)[None, :]
    cum = ((i // SSM_CHUNK) == (j // SSM_CHUNK)) & (j <= i)
    last = j == (i // SSM_CHUNK) * SSM_CHUNK + SSM_CHUNK - 1
    tri = jnp.asarray(np.stack([cum, last]), BF16)
    h = np.arange(SSM_HEADS)[None, :, None]
    g = np.arange(SSM_GROUPS)[:, None, None]
    lane = np.arange(SSD_GW)[None, None, :]
    hsel = jnp.asarray(h == g * SSD_HPG + lane // SSM_HEAD_DIM, BF16)
    return tri, hsel


def _ssd_prompt(proj, conv0, s0, conv_w, conv_b, dt_bias, a_log, d_skip, norm_g, batch, seq):
    assert seq % SSD_TL == 0 and SSM_STATE == LANE and SSM_HEADS == LANE and SSM_CHUNK == SSM_HEAD_DIM
    nl = seq // SSD_TL
    gpl = SSD_GW // LANE
    xb = SSM_D_INNER // SSD_GW
    bb = 2 * SSM_D_INNER // LANE
    cb = bb + SSM_GROUPS
    db = cb + SSM_GROUPS
    tri, hsel = _ssd_consts()
    d_e = jnp.repeat(d_skip.astype(F32), SSM_HEAD_DIM)[None]
    row = lambda a: a.astype(F32)[None]

    def rows_spec(width, col_fn):
        return pl.BlockSpec((SSD_TL, width), lambda b, g, t: (b * nl + t, col_fn(g)))

    def conv0_spec(width, col_fn):
        return pl.BlockSpec((None, SSM_CONV - 1, width), lambda b, g, t: (b, 0, col_fn(g)))

    def vec_spec(rows_, width, col_fn):
        return pl.BlockSpec((rows_, width), lambda b, g, t: (0, col_fn(g)))

    xcol = lambda g: g
    bcol = lambda g: SSM_D_INNER // LANE + g
    ccol = lambda g: SSM_D_INNER // LANE + SSM_GROUPS + g
    zero = lambda g: 0
    st_spec = pl.BlockSpec((None, SSD_GW, SSM_STATE), lambda b, g, t: (b, g, 0))
    return pl.pallas_call(
        _ssd_kernel,
        grid=(batch, SSM_GROUPS, nl),
        in_specs=[rows_spec(SSD_GW, xcol), rows_spec(SSD_GW, lambda g: xb + g),
                  rows_spec(LANE, lambda g: bb + g), rows_spec(LANE, lambda g: cb + g),
                  rows_spec(LANE, lambda g: db),
                  conv0_spec(SSD_GW, xcol), conv0_spec(LANE, bcol), conv0_spec(LANE, ccol), st_spec,
                  vec_spec(SSM_CONV, SSD_GW, xcol), vec_spec(SSM_CONV, LANE, bcol), vec_spec(SSM_CONV, LANE, ccol),
                  vec_spec(1, SSD_GW, xcol), vec_spec(1, LANE, bcol), vec_spec(1, LANE, ccol),
                  vec_spec(1, LANE, zero), vec_spec(1, LANE, zero),
                  vec_spec(1, SSD_GW, xcol), vec_spec(1, SSD_GW, xcol),
                  pl.BlockSpec((2, SSD_TL, SSD_TL), lambda b, g, t: (0, 0, 0)),
                  pl.BlockSpec((None, SSM_HEADS, SSD_GW), lambda b, g, t: (g, 0, 0))],
        out_specs=[rows_spec(SSD_GW, xcol), st_spec],
        out_shape=[jax.ShapeDtypeStruct((batch * seq, SSM_D_INNER), BF16),
                   jax.ShapeDtypeStruct((batch, SSM_HEADS * SSM_HEAD_DIM, SSM_STATE), F32)],
        scratch_shapes=[pltpu.VMEM((SSM_STATE, SSD_GW), F32),
                        pltpu.VMEM((8, SSD_GW), F32), pltpu.VMEM((8, LANE), F32), pltpu.VMEM((8, LANE), F32),
                        pltpu.VMEM((SSD_TL, SSD_GW), BF16), pltpu.VMEM((SSD_TL, SSD_GW), BF16),
                        pltpu.VMEM((SSD_TL, SSD_GW), F32), pltpu.VMEM((SSD_TL, SSD_GW), F32),
                        pltpu.VMEM((SSD_TL, LANE), BF16), pltpu.VMEM((SSD_TL, LANE), BF16),
                        pltpu.VMEM((SSD_TL, SSD_GW), F32)],
        compiler_params=pltpu.CompilerParams(
            dimension_semantics=("arbitrary", "arbitrary", "arbitrary"),
            vmem_limit_bytes=VMEM_LIMIT_BYTES),
        name="ssd_prompt",
    )(proj, proj, proj, proj, proj, conv0, conv0, conv0, s0,
      conv_w, conv_w, conv_w, row(conv_b), row(conv_b), row(conv_b), row(dt_bias), row(a_log),
      d_e, row(norm_g), tri, hsel)


def split_cols(a, sizes):
    out, o = [], 0
    for s in sizes:
        out.append(a[..., o:o + s])
        o += s
    return out


def to_chunks(a, c):
    b, l = a.shape[:2]
    pad = (-l) % c
    a = jnp.pad(a.astype(F32), [(0, 0), (0, pad)] + [(0, 0)] * (a.ndim - 2))
    return jnp.moveaxis(a.reshape((b, (l + pad) // c, c) + a.shape[2:]), 1, 0)


def from_chunks(a, l):
    a = jnp.moveaxis(a, 0, 1)
    return a.reshape((a.shape[0], a.shape[1] * a.shape[2]) + a.shape[3:])[:, :l]


def gqa_attend(q, k, v, mask):
    s = jnp.einsum('...qgrd,...kgd->...qgrk', q, k, preferred_element_type=F32) * (q.shape[-1] ** -0.5)
    m = mask[..., :, None, None, :]
    p = jnp.where(m, jax.nn.softmax(jnp.where(m, s, NEG), axis=-1), 0.0)
    o = jnp.einsum('...qgrk,...kgd->...qgrd', p.astype(v.dtype), v)
    return o, p


def compress_rows(rows, w1, w2, pe):
    b, t = rows.shape[:2]
    n_sub = t // CMP_STRIDE
    n_per = CMP_BLOCK // CMP_STRIDE
    n_cmp = n_sub - n_per + 1
    sub = rows[:, :n_sub * CMP_STRIDE].reshape(b, n_sub, CMP_STRIDE, NSA_KV_HEADS, NSA_HEAD_DIM)
    sub = jnp.moveaxis(sub, 3, 2).reshape(b, n_sub, NSA_KV_HEADS, CMP_STRIDE * NSA_HEAD_DIM)
    w1r = w1.reshape(n_per, CMP_STRIDE * NSA_HEAD_DIM, CMP_HIDDEN)
    hid = pe.reshape(-1) @ w1
    for m in range(n_per):
        hid = hid + jnp.einsum('bngc,ch->bngh', sub[:, m:m + n_cmp], w1r[m])
    return jax.nn.silu(hid) @ w2


def slc_attend(q, pos, kb, vb, idx, valid):
    b, lq, g, r, d = q.shape
    k = idx.shape[-1]
    ii = jnp.moveaxis(idx, 2, 1).reshape(b, g, lq * k)
    bi = jnp.arange(b)[:, None, None]
    gi = jnp.arange(g)[None, :, None]
    kg = kb[bi, gi, ii].reshape(b, g, lq, k, SLC_BLOCK, d)
    vg = vb[bi, gi, ii].reshape(b, g, lq, k, SLC_BLOCK, d)
    s = jnp.einsum('bqgrd,bgqksd->bqgrks', q, kg, preferred_element_type=F32) * (d ** -0.5)
    kpos = idx[..., None] * SLC_BLOCK + jnp.arange(SLC_BLOCK)
    mask = ((kpos <= pos[None, :, None, None, None]) & valid[..., None])[:, :, :, None]
    s = jnp.where(mask, s, NEG).reshape(b, lq, g, r, k * SLC_BLOCK)
    p = jax.nn.softmax(s, axis=-1).reshape(b, lq, g, r, k, SLC_BLOCK)
    return jnp.einsum('bqgrks,bgqksd->bqgrd', p.astype(vg.dtype), vg)


def nsa_cmp_slc(q, pos, rows, w1, w2, pe):
    b, l, g, r, d = q.shape
    t = rows.shape[1]
    kc = compress_rows(rows[:, :, 0], w1[0], w2[0], pe[0])
    vc = compress_rows(rows[:, :, 1], w1[1], w2[1], pe[1])
    n_cmp = kc.shape[1]
    cmp_last = jnp.arange(n_cmp) * CMP_STRIDE + (CMP_BLOCK - 1)
    o_cmp, p_cmp = gqa_attend(q, kc, vc, cmp_last[None, :] <= pos[:, None])
    n_slc = -(-t // SLC_BLOCK)
    c0 = np.arange(n_cmp)[:, None] * CMP_STRIDE
    s0 = np.arange(n_slc)[None, :] * SLC_BLOCK
    cover = np.clip(np.minimum(c0 + CMP_BLOCK, s0 + SLC_BLOCK) - np.maximum(c0, s0), 0, None) / CMP_BLOCK
    score = jnp.einsum('blgrn,nj->blgj', p_cmp, jnp.asarray(cover, F32))
    blk = jnp.arange(n_slc)[None, :]
    cur = (pos // SLC_BLOCK)[:, None]
    causal = blk * SLC_BLOCK <= pos[:, None]
    forced = (blk == 0) | (blk == cur) | (blk == cur - 1)
    score = jnp.where(forced[None, :, None, :], FORCE_SCORE, jnp.where(causal[None, :, None, :], score, -1.0))
    vals, idx = lax.top_k(score, min(N_SELECT, n_slc))
    valid = vals >= 0.0
    kv = jnp.pad(rows[:, :, 2:4], ((0, 0), (0, n_slc * SLC_BLOCK - t), (0, 0), (0, 0), (0, 0)))
    kv = jnp.transpose(kv.reshape(b, n_slc, SLC_BLOCK, 2, g, d), (3, 0, 4, 1, 2, 5))
    kb, vb = kv[0], kv[1]
    qb = math.gcd(l, SLC_QBLOCK)
    nb = l // qb

    def blocks(a):
        return jnp.moveaxis(a.reshape((b, nb, qb) + a.shape[2:]), 1, 0)

    o_slc = lax.map(lambda a: slc_attend(a[0], a[1], kb, vb, a[2], a[3]),
                    (blocks(q), pos.reshape(nb, qb), blocks(idx), blocks(valid)))
    o_slc = jnp.moveaxis(o_slc, 0, 1).reshape(b, l, g, r, d)
    return o_cmp, o_slc


def swa_buffered(q, pos, kv_new, buf):
    wb = buf.shape[1]
    keys = jnp.concatenate([buf.astype(kv_new.dtype), kv_new], axis=1)
    kpos = PAST_LEN - wb + jnp.arange(keys.shape[1])
    mask = (kpos[None, :] <= pos[:, None]) & (pos[:, None] - kpos[None, :] < WINDOW)
    o, _ = gqa_attend(q, keys[:, :, 0], keys[:, :, 1], mask)
    return o, keys[:, keys.shape[1] - wb:]


def gla_chunked(q, k, v, logf, s0):
    b, l = q.shape[:2]
    c = min(HG_CHUNK, l)
    mid = (c - 1) // 2
    tril = jnp.tril(jnp.ones((c, c), bool))

    def step(s, inp):
        qc, kc, vc, gc = inp
        bc = jnp.cumsum(gc, axis=1)
        bm = bc[:, mid:mid + 1]
        a = jnp.einsum('bthd,bshd->bhts', qc * jnp.exp(bc - bm), kc * jnp.exp(bm - bc))
        a = jnp.where(tril, a, 0.0)
        o = jnp.einsum('bhts,bshv->bthv', a, vc) + jnp.einsum('bthd,bhdv->bthv', qc * jnp.exp(bc), s)
        bl = bc[:, -1]
        s = jnp.exp(bl)[..., None] * s + jnp.einsum('bshd,bshv->bhdv', kc * jnp.exp(bl[:, None] - bc), vc)
        return s, o

    s, o = lax.scan(step, s0.astype(F32), (to_chunks(q, c), to_chunks(k, c), to_chunks(v, c), to_chunks(logf, c)))
    return from_chunks(o, l), s


def hgrn2(hq, hf, hi, hg, lb, s0, g_norm):
    b, l, _ = hq.shape
    q = jax.nn.silu(hq.astype(F32)).reshape(b, l, HG_HEADS, HG_DK)
    f = (lb + (1.0 - lb) * jax.nn.sigmoid(hf.astype(F32))).reshape(b, l, HG_HEADS, HG_DK)
    v = hi.astype(F32).reshape(b, l, HG_HEADS, HG_DV)
    o, s = gla_chunked(q, 1.0 - f, v, jnp.log(f), s0)
    o = o * lax.rsqrt(jnp.mean(o * o, axis=-1, keepdims=True) + EPS)
    o = o.reshape(b, l, HG_V_W) * g_norm.astype(F32) * jax.nn.silu(hg.astype(F32))
    return o, s


def nsa_prompt_branch(proj, b, l, w1, w2, pe):
    cmp = _compress_prompt(proj, w1, w2, pe, b, l)
    g = proj[:, AB_GATE_COL:AB_GATE_COL + NSA_GATE_W]
    gates = jax.nn.sigmoid(g).reshape(b * l, NSA_KV_HEADS, 3 * NSA_GROUP)
    gates = jnp.pad(gates, ((0, 0), (0, 0), (0, LANE - 3 * NSA_GROUP))).reshape(b * l, NSA_KV_HEADS * LANE)
    return _nsa_prompt(proj, cmp[0], cmp[1], gates, b, l)


AB_HG_COL = NSA_Q + NSA_KV_W
AB_GATE_COL = AB_HG_COL + 2 * HG_QF_W + 2 * HG_V_W
AB_PACKED = -(-(AB_GATE_COL + NSA_GATE_W) // LANE) * LANE


def pack_ab_in(w):
    g0 = NSA_Q + NSA_KV_W
    pad = jnp.zeros((w.shape[0], AB_PACKED - w.shape[1]), BF16)
    return jnp.concatenate([w[:, :g0].astype(BF16), w[:, g0 + NSA_GATE_W:].astype(BF16),
                            w[:, g0:g0 + NSA_GATE_W].astype(BF16), pad], axis=1)


def ab_mixer_prompt(x, h, b, l, hg_state, lb, w_in, w1, w2, pe, hg_g, w_out):
    proj = _mm(h, w_in)
    o_nsa = nsa_prompt_branch(proj, b, l, w1, w2, pe)
    o_hg, s_new = _hgrn_prompt(proj, AB_HG_COL, lb[None], hg_g[None], hg_state, b, l)
    x = _mm([o_nsa, o_hg], w_out, res=x)
    slot_w = NSA_KV_HEADS * NSA_HEAD_DIM
    rows = proj[:, NSA_Q:NSA_Q + 4 * slot_w].reshape(b, l, 4, NSA_KV_HEADS, NSA_HEAD_DIM)
    wb = min(WINDOW, l)
    buf = proj.reshape(b, l, AB_PACKED)[:, l - wb:, NSA_Q + 4 * slot_w:NSA_Q + 6 * slot_w]
    return x, rows, buf.reshape(b, wb, 2, NSA_KV_HEADS, NSA_HEAD_DIM), s_new


def nsa_decode_branch(q, kv, g, cache, page0, page_table, swa_buf, w1, w2, pe):
    b, ls = q.shape[:2]
    assert page_table.shape[1] * PAGE_SIZE == PAST_LEN
    hd = NSA_HEAD_DIM
    row_pad = DEC_QROWS - NSA_GROUP
    q_pad = q.reshape(b, ls, NSA_KV_HEADS, NSA_GROUP, hd).transpose(0, 2, 1, 3, 4)
    q_pad = jnp.pad(q_pad, ((0, 0), (0, 0), (0, 0), (0, row_pad), (0, 0))).reshape(b, NSA_KV_HEADS, ls * DEC_QROWS, hd)
    gates = jax.nn.sigmoid(g).reshape(b, ls, NSA_KV_HEADS, NSA_GROUP, 3).transpose(0, 2, 1, 3, 4)
    gates = jnp.pad(gates, ((0, 0), (0, 0), (0, 0), (0, row_pad), (0, LANE - 3)))
    gates = gates.reshape(b, NSA_KV_HEADS, ls * DEC_QROWS, LANE)
    new_rows = jnp.pad(kv[:, :, 2:6].transpose(0, 3, 2, 1, 4), ((0, 0), (0, 0), (0, 0), (0, 8 - ls), (0, 0)))
    cmp = _compress_paged(cache, page0, page_table, w1, w2, pe)
    o_cmp, idx = _decode_select(q_pad, cmp, ls)
    o = _decode_attend(q_pad, o_cmp, gates, new_rows, swa_buf, cache, page0, page_table,
                       idx[:, :, :ls, :N_SELECT - 1].reshape(-1), ls)
    o = o.reshape(b, NSA_KV_HEADS, ls, DEC_QROWS, hd)[:, :, :, :NSA_GROUP]
    return o.transpose(0, 2, 1, 3, 4).reshape(b * ls, NSA_Q)


def ab_mixer_sample(x, h, b, l, cache, page0, page_table, swa_buf, hg_state, lb, w_in, w1, w2, pe, hg_g, w_out):
    proj = _mm(h, w_in).reshape(b, l, AB_PACKED)
    q, kv, hq, hf, hi, hgate, g = split_cols(
        proj, (NSA_Q, NSA_KV_W, HG_QF_W, HG_QF_W, HG_V_W, HG_V_W, NSA_GATE_W))
    kv = kv.reshape(b, l, 6, NSA_KV_HEADS, NSA_HEAD_DIM)
    o_nsa = nsa_decode_branch(q, kv, g, cache, page0, page_table, swa_buf, w1, w2, pe)
    new_buf = jnp.concatenate([swa_buf.astype(kv.dtype), kv[:, :, 4:]], axis=1)[:, l:]
    o_hg, s_new = hgrn2(hq, hf, hi, hgate, lb, hg_state, hg_g)
    mix = jnp.concatenate([o_nsa, o_hg.reshape(b * l, HG_V_W)], axis=-1).astype(BF16)
    return _mm(mix, w_out, res=x), kv[:, :, :4], new_buf, s_new


def ssd_chunked(x, dt, a, bm, cm, s0):
    b, l, nh, p = x.shape
    g, n = bm.shape[2], bm.shape[3]
    r = nh // g
    c = min(SSM_CHUNK, l)
    tril = jnp.tril(jnp.ones((c, c), bool))

    def step(s, inp):
        xc, dtc, bc, cc = inp
        cum = jnp.cumsum(dtc * a, axis=1)
        seg = cum[:, :, None, :] - cum[:, None, :, :]
        lm = jnp.exp(jnp.where(tril[None, :, :, None], seg, -jnp.inf)).reshape(b, c, c, g, r)
        xdt = (xc * dtc[..., None]).reshape(b, c, g, r, p)
        cb = jnp.einsum('btgn,bsgn->btsg', cc, bc)
        sg = s.reshape(b, g, r, p, n)
        y = jnp.einsum('btsg,btsgr,bsgrp->btgrp', cb, lm, xdt)
        y = y + jnp.einsum('btgn,bgrpn->btgrp', cc, sg) * jnp.exp(cum).reshape(b, c, g, r)[..., None]
        dec = jnp.exp(cum[:, -1:] - cum).reshape(b, c, g, r)
        sg = jnp.exp(cum[:, -1]).reshape(b, g, r)[..., None, None] * sg + jnp.einsum('bsgn,bsgrp->bgrpn', bc, xdt * dec[..., None])
        return sg.reshape(b, nh, p, n), y.reshape(b, c, nh, p)

    s, y = lax.scan(step, s0.astype(F32), (to_chunks(x, c), to_chunks(dt, c), to_chunks(bm, c), to_chunks(cm, c)))
    return from_chunks(y, l), s


def mamba_prompt(x, h, b, l, w_in, conv_w, conv_b, dt_bias, a_log, d_skip, norm_g, w_out):
    proj = _mm(h, w_in)
    y, s = _ssd_prompt(proj, jnp.zeros((b, SSM_CONV - 1, SSM_CONV_DIM), F32),
                       jnp.zeros((b, SSM_HEADS * SSM_HEAD_DIM, SSM_STATE), F32),
                       conv_w, conv_b, dt_bias, a_log, d_skip, norm_g, b, l)
    assert l >= SSM_CONV - 1
    new_conv = proj.reshape(b, l, -1)[:, l - (SSM_CONV - 1):, SSM_D_INNER:SSM_D_INNER + SSM_CONV_DIM]
    return _mm(y, w_out, res=x), new_conv, s.reshape(b, SSM_HEADS, SSM_HEAD_DIM, SSM_STATE)


def mamba_sample(x, h, b, l, conv_state, ssm_state, w_in, conv_w, conv_b, dt_bias, a_log, d_skip, norm_g, w_out):
    z, xbc, dt = split_cols(_mm(h, w_in).reshape(b, l, -1), (SSM_D_INNER, SSM_CONV_DIM, SSM_HEADS))
    xpad = jnp.concatenate([conv_state.astype(xbc.dtype), xbc], axis=1)
    acc = conv_b.astype(F32)
    for j in range(SSM_CONV):
        acc = acc + xpad[:, j:j + l].astype(F32) * conv_w[j].astype(F32)
    xbc = jax.nn.silu(acc)
    new_conv = xpad[:, xpad.shape[1] - (SSM_CONV - 1):]
    xs, bm, cm = split_cols(xbc, (SSM_D_INNER, SSM_GROUPS * SSM_STATE, SSM_GROUPS * SSM_STATE))
    xs = xs.reshape(b, l, SSM_HEADS, SSM_HEAD_DIM)
    bm = bm.reshape(b, l, SSM_GROUPS, SSM_STATE)
    cm = cm.reshape(b, l, SSM_GROUPS, SSM_STATE)
    dt = jax.nn.softplus(dt.astype(F32) + dt_bias.astype(F32))
    a = -jnp.exp(a_log.astype(F32))
    y, s = ssd_chunked(xs, dt, a, bm, cm, ssm_state)
    y = y + d_skip.astype(F32)[:, None] * xs
    y = (y.reshape(b, l, SSM_D_INNER) * jax.nn.silu(z.astype(F32))).reshape(b, l, SSM_GROUPS, SSM_D_INNER // SSM_GROUPS)
    y = (y * lax.rsqrt(jnp.mean(y * y, axis=-1, keepdims=True) + EPS)).reshape(b, l, SSM_D_INNER) * norm_g.astype(F32)
    return _mm(y.reshape(b * l, SSM_D_INNER).astype(BF16), w_out, res=x), new_conv, s


def ffn_ple(x, p, g_ffn, w_ffn_in, w_ffn_out, g_ple, w_ple_gate, w_ple_up):
    x = _mm(_mm_swiglu(_rms(x, g_ffn, BF16), w_ffn_in), w_ffn_out, res=x)
    return _mm_ple(_rms(x, g_ple, BF16), w_ple_gate, p, w_ple_up, x)


def kernel(x_prompt, x_sample, cache_nsa_kv, cache_swa_kv, state_hgrn, state_ssm, cache_conv, page_table, p_prompt, p_sample, norm_mix, norm_ffn, w_ab_in, w_cmp1, w_cmp2, cmp_pe, hg_lb_logits, hg_norm, w_ab_out, w_ssm_in, ssm_conv_w, ssm_conv_b, ssm_dt_bias, ssm_a_log, ssm_d, ssm_norm, w_ssm_out, w_ffn_in, w_ffn_out, w_ple_up, w_ple_gate, norm_ple, norm_final):
    depth = norm_mix.shape[0]
    bp, lp, d = x_prompt.shape
    bs, ls, _ = x_sample.shape
    n_pool = cache_nsa_kv.shape[1]
    cache_rows = cache_nsa_kv.reshape(cache_nsa_kv.shape[0] * n_pool, PAGE_SIZE, 4 * NSA_KV_HEADS, NSA_HEAD_DIM)
    lb_all = jnp.cumsum(jax.nn.softmax(hg_lb_logits.astype(F32), axis=0), axis=0)
    w_ffn_out_b = w_ffn_out.astype(BF16)
    w_ssm_out_b = w_ssm_out.astype(BF16)
    xp = x_prompt.reshape(bp * lp, d)
    xs = x_sample.reshape(bs * ls, d)
    nsa_p, nsa_s, swa_p, swa_s, hg_p, hg_s, ssm_p, ssm_s, cv_p, cv_s = [], [], [], [], [], [], [], [], [], []
    for i in range(depth):
        hp = _rms(xp, norm_mix[i], BF16)
        hs = _rms(xs, norm_mix[i], BF16)
        if i % 2 == 0:
            a = i // 2
            wa = (lb_all[a], pack_ab_in(w_ab_in[a]), w_cmp1[a], w_cmp2[a], cmp_pe[a], hg_norm[a],
                  Stacked(w_ab_out, a))
            xp, r_p, b_p, s_p = ab_mixer_prompt(xp, hp, bp, lp, jnp.zeros((bp, HG_HEADS, HG_DK, HG_DV), F32), *wa)
            xs, r_s, b_s, s_s = ab_mixer_sample(xs, hs, bs, ls, cache_rows, a * n_pool, page_table,
                                                cache_swa_kv[a], state_hgrn[a], *wa)
            nsa_p.append(r_p)
            nsa_s.append(r_s)
            swa_p.append(b_p)
            swa_s.append(b_s)
            hg_p.append(s_p)
            hg_s.append(s_s)
        else:
            c = i // 2
            wc = (Stacked(w_ssm_in, c), ssm_conv_w[c], ssm_conv_b[c], ssm_dt_bias[c], ssm_a_log[c], ssm_d[c],
                  ssm_norm[c], Stacked(w_ssm_out_b, c))
            xp, c_p, t_p = mamba_prompt(xp, hp, bp, lp, *wc)
            xs, c_s, t_s = mamba_sample(xs, hs, bs, ls, cache_conv[c], state_ssm[c], *wc)
            cv_p.append(c_p)
            cv_s.append(c_s)
            ssm_p.append(t_p)
            ssm_s.append(t_s)
        wf = (norm_ffn[i], Stacked(w_ffn_in, i), Stacked(w_ffn_out_b, i), norm_ple[i],
              Stacked(w_ple_gate, i), Stacked(w_ple_up, i))
        xp = ffn_ple(xp, p_prompt[i].reshape(bp * lp, PLE_DIM).astype(BF16), *wf)
        xs = ffn_ple(xs, p_sample[i].reshape(bs * ls, PLE_DIM).astype(BF16), *wf)
    y_prompt = _rms(xp, norm_final, F32).reshape(bp, lp, d)
    y_sample = _rms(xs, norm_final, F32).reshape(bs, ls, d)
    return (y_prompt, y_sample, jnp.stack(nsa_p), jnp.stack(nsa_s), jnp.stack(swa_p), jnp.stack(swa_s),
            jnp.stack(hg_p), jnp.stack(hg_s), jnp.stack(ssm_p), jnp.stack(ssm_s), jnp.stack(cv_p), jnp.stack(cv_s))
```

```python
import functools
import math
from typing import NamedTuple

import jax
import jax.numpy as jnp
import numpy as np
from jax import lax
from jax.experimental import pallas as pl
from jax.experimental.pallas import tpu as pltpu

D_MODEL = 4096
PAST_LEN = 16384
PAGE_SIZE = 128
PLE_DIM = 256
NSA_HEADS = 16
NSA_KV_HEADS = 4
NSA_GROUP = NSA_HEADS // NSA_KV_HEADS
NSA_HEAD_DIM = 128
NSA_Q = NSA_HEADS * NSA_HEAD_DIM
NSA_KV_W = 6 * NSA_KV_HEADS * NSA_HEAD_DIM
NSA_GATE_W = 3 * NSA_HEADS
CMP_BLOCK = 32
CMP_STRIDE = 16
CMP_HIDDEN = 256
SLC_BLOCK = 64
N_SELECT = 16
WINDOW = 512
SLC_QBLOCK = 32
SWA_QBLOCK = 128
FORCE_SCORE = 1e4
HG_HEADS = 16
HG_DK = 128
HG_DV = (D_MODEL // 2) // HG_HEADS
HG_QF_W = HG_HEADS * HG_DK
HG_V_W = HG_HEADS * HG_DV
HG_CHUNK = 32
SSM_D_INNER = 2 * D_MODEL
SSM_HEAD_DIM = 64
SSM_HEADS = SSM_D_INNER // SSM_HEAD_DIM
SSM_GROUPS = 8
SSM_STATE = 128
SSM_CONV = 4
SSM_CONV_DIM = SSM_D_INNER + 2 * SSM_GROUPS * SSM_STATE
SSM_CHUNK = 64
D_FF = ((8 * D_MODEL + 3 * 256 - 1) // (3 * 256)) * 256
EPS = 1e-6
NEG = -1e30

LANE = 128
VMEM_LIMIT_BYTES = 56 * 1024 * 1024
BF16 = jnp.bfloat16
F32 = jnp.float32


MAX_ROW_TILE = 1024
NORM_ROW_TILE = 512


def _rms_kernel(x_ref, g_ref, o_ref):
    x = x_ref[...]
    y = x * lax.rsqrt(jnp.mean(x * x, axis=-1, keepdims=True) + EPS)
    o_ref[...] = (y * g_ref[...]).astype(o_ref.dtype)


def _rms(x, g, out_dtype):
    m, d = x.shape
    tm = min(m, NORM_ROW_TILE)
    assert m % tm == 0
    return pl.pallas_call(
        _rms_kernel,
        grid=(m // tm,),
        in_specs=[pl.BlockSpec((tm, d), lambda i: (i, 0)), pl.BlockSpec((1, d), lambda i: (0, 0))],
        out_specs=pl.BlockSpec((tm, d), lambda i: (i, 0)),
        out_shape=jax.ShapeDtypeStruct((m, d), out_dtype),
        compiler_params=pltpu.CompilerParams(dimension_semantics=("arbitrary",),
                                             vmem_limit_bytes=VMEM_LIMIT_BYTES),
        name="rms",
    )(x, g.astype(F32)[None])


def _mm_kernel(*refs, n_lhs, residual):
    x_refs = refs[:n_lhs]
    w_ref = refs[n_lhs]
    o_ref = refs[-1]
    k = pl.program_id(2)

    def first(x_ref):
        acc = jnp.dot(x_ref[...], w_ref[...].astype(BF16), preferred_element_type=F32)
        o_ref[...] = acc + refs[n_lhs + 1][...] if residual else acc

    def later(x_ref):
        o_ref[...] += jnp.dot(x_ref[...], w_ref[...].astype(BF16), preferred_element_type=F32)

    pl.when(k == 0)(functools.partial(first, x_refs[0]))
    if n_lhs == 1:
        pl.when(k > 0)(functools.partial(later, x_refs[0]))
    else:
        for p in range(1, n_lhs):
            pl.when(k == p)(functools.partial(later, x_refs[p]))


def _w_tile_bytes(w):
    return 2 * w.dtype.itemsize + (2 if w.dtype != BF16 else 0)


def _k_tile(k, x_bytes_per_k, w_bytes_per_k, budget):
    units = k // LANE
    for parts in range(1, units + 1):
        if units % parts == 0 and (x_bytes_per_k + w_bytes_per_k) * (units // parts) * LANE <= budget:
            return (units // parts) * LANE
    raise ValueError(f"no K tile for {k}")


class Stacked(NamedTuple):
    w: jax.Array
    layer: int


def _unstack(w):
    return (w.w, w.layer) if isinstance(w, Stacked) else (w, None)


def _w_spec(block, index_map, layer):
    if layer is None:
        return pl.BlockSpec(block, index_map)
    return pl.BlockSpec((None,) + block, lambda *g: (layer,) + index_map(*g))


def _mm(xs, w, res=None):
    xs = list(xs) if isinstance(xs, (list, tuple)) else [xs]
    m = xs[0].shape[0]
    w, layer = _unstack(w)
    k, n = w.shape[-2:]
    tm = min(m, MAX_ROW_TILE)
    n_out_bufs = 4 if res is not None else 2
    widest = 1024 if w.dtype == BF16 else 512
    for tn in (min(n, widest), min(n, widest // 2)):
        budget = VMEM_LIMIT_BYTES - n_out_bufs * tm * tn * 4 - (4 << 20)
        x_per_k = 2 * 2 * tm * len(xs)
        w_per_k = _w_tile_bytes(w) * tn
        tk = _k_tile(k, x_per_k, w_per_k, budget) if len(xs) == 1 else k // len(xs)
        if tk == k or tk >= 2048:
            break
    if len(xs) == 1:
        x_specs = [pl.BlockSpec((tm, tk), lambda i, j, kk: (i, kk))]
    else:
        assert all(x.shape[1] == tk for x in xs) and (x_per_k + w_per_k) * tk <= budget
        x_specs = [pl.BlockSpec((tm, tk), lambda i, j, kk: (i, 0)) for _ in xs]
    assert m % tm == 0 and k % tk == 0 and tk % LANE == 0
    tile = pl.BlockSpec((tm, tn), lambda i, j, kk: (i, j))
    return pl.pallas_call(
        functools.partial(_mm_kernel, n_lhs=len(xs), residual=res is not None),
        grid=(m // tm, pl.cdiv(n, tn), k // tk),
        in_specs=x_specs + [_w_spec((tk, tn), lambda i, j, kk: (kk, j), layer)] + ([tile] if res is not None else []),
        out_specs=tile,
        out_shape=jax.ShapeDtypeStruct((m, n), F32),
        compiler_params=pltpu.CompilerParams(
            dimension_semantics=("arbitrary", "arbitrary", "arbitrary"),
            vmem_limit_bytes=VMEM_LIMIT_BYTES),
        name="mm",
    )(*xs, w, *([res] if res is not None else []))


FFN_TILE = 256


def _swiglu_kernel(x_ref, wg_ref, wu_ref, o_ref):
    x = x_ref[...]
    gate = jnp.dot(x, wg_ref[...].astype(BF16), preferred_element_type=F32)
    up = jnp.dot(x, wu_ref[...].astype(BF16), preferred_element_type=F32)
    o_ref[...] = (gate * jax.nn.sigmoid(gate) * up).astype(o_ref.dtype)


def _mm_swiglu(x, w):
    m, k = x.shape
    w, layer = w
    hidden = w.shape[2] // 2
    tm = min(m, MAX_ROW_TILE)
    nj = hidden // FFN_TILE
    assert m % tm == 0 and hidden % FFN_TILE == 0 and w.shape[1] == k
    return pl.pallas_call(
        _swiglu_kernel,
        grid=(m // tm, nj),
        in_specs=[pl.BlockSpec((tm, k), lambda i, j: (i, 0)),
                  _w_spec((k, FFN_TILE), lambda i, j: (0, j), layer),
                  _w_spec((k, FFN_TILE), lambda i, j: (0, nj + j), layer)],
        out_specs=pl.BlockSpec((tm, FFN_TILE), lambda i, j: (i, j)),
        out_shape=jax.ShapeDtypeStruct((m, hidden), BF16),
        compiler_params=pltpu.CompilerParams(dimension_semantics=("arbitrary", "arbitrary"),
                                             vmem_limit_bytes=VMEM_LIMIT_BYTES),
        name="mm_swiglu",
    )(x, w, w)


PLE_COL_TILE = 512


def _ple_kernel(t_ref, wg_ref, p_ref, wu_ref, res_ref, o_ref):
    gate = jnp.dot(t_ref[...], wg_ref[...].astype(BF16), preferred_element_type=F32)
    up = jnp.dot(p_ref[...], wu_ref[...].astype(BF16), preferred_element_type=F32)
    o_ref[...] = res_ref[...] + up * jax.nn.sigmoid(gate)


def _mm_ple(t, w_gate, p, w_up, res):
    m, k = t.shape
    w_gate, layer = w_gate
    w_up, layer_up = w_up
    assert layer == layer_up
    n = w_gate.shape[2]
    kp = p.shape[1]
    tm = min(m, MAX_ROW_TILE)
    tn = PLE_COL_TILE
    assert m % tm == 0 and n % tn == 0
    tile = pl.BlockSpec((tm, tn), lambda i, j: (i, j))
    return pl.pallas_call(
        _ple_kernel,
        grid=(m // tm, n // tn),
        in_specs=[pl.BlockSpec((tm, k), lambda i, j: (i, 0)), _w_spec((k, tn), lambda i, j: (0, j), layer),
                  pl.BlockSpec((tm, kp), lambda i, j: (i, 0)), _w_spec((kp, tn), lambda i, j: (0, j), layer), tile],
        out_specs=tile,
        out_shape=jax.ShapeDtypeStruct((m, n), F32),
        compiler_params=pltpu.CompilerParams(dimension_semantics=("arbitrary", "arbitrary"),
                                             vmem_limit_bytes=VMEM_LIMIT_BYTES),
        name="mm_ple",
    )(t, w_gate, p, w_up, res)


NSA_TQ = 256
NSA_KEY_BUCKET = 512


def _dot_nt(a, b):
    return lax.dot_general(a, b, (((1,), (1,)), ((), ())), preferred_element_type=F32)


def _group_attend(q4, k, v, mask, scale, tq):
    s4 = _dot_nt(q4, k) * scale
    es, dens = [], []
    for r in range(NSA_GROUP):
        s = jnp.where(mask, s4[r * tq:(r + 1) * tq], NEG)
        e = jnp.exp(s - jnp.max(s, axis=-1, keepdims=True))
        dens.append(jnp.sum(e, axis=-1, keepdims=True))
        es.append(e.astype(BF16))
    o4 = jnp.dot(jnp.concatenate(es, axis=0), v, preferred_element_type=F32)
    return [o4[r * tq:(r + 1) * tq] / dens[r] for r in range(NSA_GROUP)]


def _nsa_prompt_kernel(q_ref, kc_ref, vc_ref, ks_ref, vs_ref, kw_ref, vw_ref, gate_ref, cover_ref, expand_ref,
                       o_ref, slc_s, *, seq):
    tq = NSA_TQ
    hd = NSA_HEAD_DIM
    n_cmp = seq // CMP_STRIDE - CMP_BLOCK // CMP_STRIDE + 1
    n_slc = seq // SLC_BLOCK
    span = WINDOW + tq
    scale = hd ** -0.5
    q0 = pl.program_id(2) * tq
    pos = q0 + lax.broadcasted_iota(jnp.int32, (tq, 1), 0)
    lane = lax.broadcasted_iota(jnp.int32, (tq, LANE), 1)
    q4 = jnp.concatenate([q_ref[:, r * hd:(r + 1) * hd].astype(BF16) for r in range(NSA_GROUP)], axis=0)

    cmp_ok = (lane * CMP_STRIDE + (CMP_BLOCK - 1) <= pos) & (lane < n_cmp)
    s4 = _dot_nt(q4, kc_ref[...]) * scale
    psum = jnp.zeros((tq, LANE), F32)
    ps = []
    for r in range(NSA_GROUP):
        s = jnp.where(cmp_ok, s4[r * tq:(r + 1) * tq], NEG)
        e = jnp.where(cmp_ok, jnp.exp(s - jnp.max(s, axis=-1, keepdims=True)), 0.0)
        den = jnp.sum(e, axis=-1, keepdims=True)
        p = e / jnp.where(den > 0.0, den, 1.0)
        psum = psum + p
        ps.append(p.astype(BF16))
    o_cmp4 = jnp.dot(jnp.concatenate(ps, axis=0), vc_ref[...], preferred_element_type=F32)

    score = jnp.dot(psum, cover_ref[...], preferred_element_type=F32, precision=lax.Precision.HIGHEST)
    cur = pos // SLC_BLOCK
    forced = (lane == 0) | (lane == cur) | (lane == cur - 1)
    causal = lane * SLC_BLOCK <= pos
    score = jnp.where(forced, FORCE_SCORE, jnp.where(causal, score, -1.0))
    score = jnp.where(lane < n_slc, score, -2.0)
    rank = jnp.zeros((tq, LANE), jnp.int32)
    for i in range(n_slc):
        ci = score[:, i:i + 1]
        beats = (ci > score) | ((ci == score) & (lane > i))
        rank = rank + beats.astype(jnp.int32)
    sel = ((rank < min(N_SELECT, n_slc)) & (lane < n_slc)).astype(BF16)

    def selected(nk):
        sel_keys = jnp.dot(sel, expand_ref[:, :nk], preferred_element_type=F32)
        kpos = lax.broadcasted_iota(jnp.int32, (tq, nk), 1)
        outs = _group_attend(q4, ks_ref[:nk, :].astype(BF16), vs_ref[:nk, :].astype(BF16),
                             (sel_keys > 0.5) & (kpos <= pos), scale, tq)
        for r in range(NSA_GROUP):
            slc_s[r] = outs[r]

    n_bucket = (q0 + tq + NSA_KEY_BUCKET - 1) // NSA_KEY_BUCKET
    for nb in range(1, seq // NSA_KEY_BUCKET + 1):
        pl.when(n_bucket == nb)(functools.partial(selected, nb * NSA_KEY_BUCKET))

    w0 = pl.multiple_of(jnp.maximum(q0 - WINDOW, 0), tq)
    wpos = w0 + lax.broadcasted_iota(jnp.int32, (tq, span), 1)
    o_swa = _group_attend(q4, kw_ref[pl.ds(w0, span), :].astype(BF16), vw_ref[pl.ds(w0, span), :].astype(BF16),
                          (wpos <= pos) & (pos - wpos < WINDOW), scale, tq)

    for r in range(NSA_GROUP):
        g = gate_ref[:, 3 * r:3 * r + 3]
        o = g[:, 0:1] * o_cmp4[r * tq:(r + 1) * tq] + g[:, 1:2] * slc_s[r] + g[:, 2:3] * o_swa[r]
        o_ref[:, r * hd:(r + 1) * hd] = o.astype(o_ref.dtype)


def _nsa_prompt(proj, kc, vc, gates, batch, seq):
    assert seq % NSA_TQ == 0 and seq // SLC_BLOCK <= LANE and seq // CMP_STRIDE <= LANE + 1
    assert seq % NSA_KEY_BUCKET == 0 and seq >= WINDOW + NSA_TQ and WINDOW % NSA_TQ == 0
    nt = seq // NSA_TQ
    n_cmp = seq // CMP_STRIDE - CMP_BLOCK // CMP_STRIDE + 1
    n_slc = seq // SLC_BLOCK
    c0 = np.arange(LANE)[:, None] * CMP_STRIDE
    s0 = np.arange(LANE)[None, :] * SLC_BLOCK
    cover = np.clip(np.minimum(c0 + CMP_BLOCK, s0 + SLC_BLOCK) - np.maximum(c0, s0), 0, None) / CMP_BLOCK
    cover = cover * (np.arange(LANE)[:, None] < n_cmp) * (np.arange(LANE)[None, :] < n_slc)
    expand = (np.arange(seq)[None, :] // SLC_BLOCK == np.arange(LANE)[:, None])
    hd = NSA_HEAD_DIM
    gw = NSA_GROUP * hd

    def kv_spec(slot):
        return pl.BlockSpec((seq, hd), lambda b, g, t: (b, NSA_Q // hd + slot * NSA_KV_HEADS + g))

    cmp_spec = pl.BlockSpec((None, None, LANE, hd), lambda b, g, t: (b, g, 0, 0))
    return pl.pallas_call(
        functools.partial(_nsa_prompt_kernel, seq=seq),
        grid=(batch, NSA_KV_HEADS, nt),
        in_specs=[pl.BlockSpec((NSA_TQ, gw), lambda b, g, t: (b * nt + t, g)),
                  cmp_spec, cmp_spec, kv_spec(2), kv_spec(3), kv_spec(4), kv_spec(5),
                  pl.BlockSpec((NSA_TQ, LANE), lambda b, g, t: (b * nt + t, g)),
                  pl.BlockSpec((LANE, LANE), lambda b, g, t: (0, 0)),
                  pl.BlockSpec((LANE, seq), lambda b, g, t: (0, 0))],
        out_specs=pl.BlockSpec((NSA_TQ, gw), lambda b, g, t: (b * nt + t, g)),
        out_shape=jax.ShapeDtypeStruct((batch * seq, NSA_Q), BF16),
        scratch_shapes=[pltpu.VMEM((NSA_GROUP, NSA_TQ, hd), F32)],
        compiler_params=pltpu.CompilerParams(
            dimension_semantics=("arbitrary", "arbitrary", "arbitrary"),
            vmem_limit_bytes=VMEM_LIMIT_BYTES),
        name="nsa_prompt",
    )(proj, kc, vc, proj, proj, proj, proj, gates, jnp.asarray(cover, F32), jnp.asarray(expand, BF16))


def _compress_kernel(x_ref, w1_ref, w2_ref, pe_ref, o_ref, *, seq):
    n_sub = seq // CMP_STRIDE
    n_cmp = n_sub - CMP_BLOCK // CMP_STRIDE + 1
    acc = jnp.zeros((n_sub, 2 * CMP_HIDDEN), F32)
    for r in range(CMP_STRIDE):
        xr = x_ref[pl.ds(r, n_sub, stride=CMP_STRIDE), :].astype(BF16)
        acc = acc + jnp.dot(xr, w1_ref[r], preferred_element_type=F32)
    hid = acc[:, :CMP_HIDDEN] + pltpu.roll(acc[:, CMP_HIDDEN:], n_sub - 1, 0) + pe_ref[...]
    c = jnp.dot((hid * jax.nn.sigmoid(hid)).astype(BF16), w2_ref[...], preferred_element_type=F32)
    row = lax.broadcasted_iota(jnp.int32, c.shape, 0)
    o_ref[...] = jnp.where(row < n_cmp, c, 0.0).astype(o_ref.dtype)


def _compress_prompt(proj, w1, w2, pe, batch, seq):
    assert CMP_BLOCK == 2 * CMP_STRIDE and seq // CMP_STRIDE == LANE and NSA_HEAD_DIM == LANE
    hd = NSA_HEAD_DIM
    w1p = w1.reshape(2, 2, CMP_STRIDE, hd, CMP_HIDDEN).transpose(0, 2, 3, 1, 4)
    w1p = w1p.reshape(2, CMP_STRIDE, hd, 2 * CMP_HIDDEN).astype(BF16)
    pe_h = jnp.einsum('sc,sch->sh', pe.reshape(2, -1), w1)[:, None, :]
    return pl.pallas_call(
        functools.partial(_compress_kernel, seq=seq),
        grid=(batch, 2, NSA_KV_HEADS),
        in_specs=[pl.BlockSpec((seq, hd), lambda b, s, g: (b, NSA_Q // hd + s * NSA_KV_HEADS + g)),
                  pl.BlockSpec((None, CMP_STRIDE, hd, 2 * CMP_HIDDEN), lambda b, s, g: (s, 0, 0, 0)),
                  pl.BlockSpec((None, CMP_HIDDEN, hd), lambda b, s, g: (s, 0, 0)),
                  pl.BlockSpec((None, 1, CMP_HIDDEN), lambda b, s, g: (s, 0, 0))],
        out_specs=pl.BlockSpec((None, None, None, LANE, hd), lambda b, s, g: (s, b, g, 0, 0)),
        out_shape=jax.ShapeDtypeStruct((2, batch, NSA_KV_HEADS, LANE, hd), BF16),
        compiler_params=pltpu.CompilerParams(dimension_semantics=("arbitrary", "arbitrary", "arbitrary"),
                                             vmem_limit_bytes=VMEM_LIMIT_BYTES),
        name="compress_prompt",
    )(proj, w1p, w2.astype(BF16), pe_h)


CMP_PAGES = 16
DEC_QROWS = 8


def _compress_paged_kernel(pt_ref, *refs, n_cmp):
    pages = refs[:CMP_PAGES + 1]
    w1_ref, w2_ref, pe_ref, o_ref = refs[CMP_PAGES + 1:]
    per_row = 4 * NSA_KV_HEADS
    sub_pp = PAGE_SIZE // CMP_STRIDE
    own = CMP_PAGES * sub_pp
    tot = own + sub_pp
    first = pl.program_id(1) * own
    for slot in range(2):
        acc = jnp.zeros((NSA_KV_HEADS * tot, 2 * CMP_HIDDEN), F32)
        for r in range(CMP_STRIDE):
            xr = jnp.concatenate(
                [page[pl.ds(r * per_row + slot * NSA_KV_HEADS + g, sub_pp, stride=CMP_STRIDE * per_row), :]
                 for g in range(NSA_KV_HEADS) for page in pages], axis=0).astype(BF16)
            acc = acc + jnp.dot(xr, w1_ref[slot, r], preferred_element_type=F32)
        for g in range(NSA_KV_HEADS):
            blk = acc[g * tot:(g + 1) * tot]
            hid = (blk[:, :CMP_HIDDEN] + pltpu.roll(blk[:, CMP_HIDDEN:], tot - 1, 0))[:own] + pe_ref[slot]
            c = jnp.dot((hid * jax.nn.sigmoid(hid)).astype(BF16), w2_ref[slot], preferred_element_type=F32)
            sub = first + lax.broadcasted_iota(jnp.int32, c.shape, 0)
            o_ref[slot, g] = jnp.where(sub < n_cmp, c, 0.0).astype(o_ref.dtype)


def _compress_paged(cache, page0, page_table, w1, w2, pe):
    batch, n_pages = page_table.shape
    hd = NSA_HEAD_DIM
    sub_pp = PAGE_SIZE // CMP_STRIDE
    n_sub = n_pages * sub_pp
    assert n_pages % CMP_PAGES == 0 and PAGE_SIZE % CMP_STRIDE == 0 and CMP_BLOCK == 2 * CMP_STRIDE
    w1p = w1.reshape(2, 2, CMP_STRIDE, hd, CMP_HIDDEN).transpose(0, 2, 3, 1, 4)
    w1p = w1p.reshape(2, CMP_STRIDE, hd, 2 * CMP_HIDDEN).astype(BF16)
    pe_h = jnp.einsum('sc,sch->sh', pe.reshape(2, -1), w1)[:, None, :]

    def page_spec(k):
        return pl.BlockSpec((PAGE_SIZE * 4 * NSA_KV_HEADS, hd),
                            lambda b, s, pt: (page0 + pt[b, jnp.minimum(s * CMP_PAGES + k, n_pages - 1)], 0))

    full = lambda shape: pl.BlockSpec(shape, lambda b, s, pt: (0,) * len(shape))
    return pl.pallas_call(
        functools.partial(_compress_paged_kernel, n_cmp=n_sub - 1),
        grid_spec=pltpu.PrefetchScalarGridSpec(
            num_scalar_prefetch=1,
            grid=(batch, n_pages // CMP_PAGES),
            in_specs=[page_spec(k) for k in range(CMP_PAGES + 1)] + [
                full((2, CMP_STRIDE, hd, 2 * CMP_HIDDEN)), full((2, CMP_HIDDEN, hd)), full((2, 1, CMP_HIDDEN))],
            out_specs=pl.BlockSpec((2, None, NSA_KV_HEADS, CMP_PAGES * sub_pp, hd), lambda b, s, pt: (0, b, 0, s, 0))),
        out_shape=jax.ShapeDtypeStruct((2, batch, NSA_KV_HEADS, n_sub, hd), BF16),
        compiler_params=pltpu.CompilerParams(dimension_semantics=("arbitrary", "arbitrary"),
                                             vmem_limit_bytes=VMEM_LIMIT_BYTES),
        name="compress_paged",
    )(page_table, *([cache.reshape(-1, hd)] * (CMP_PAGES + 1)), w1p, w2.astype(BF16), pe_h)


def _decode_select_kernel(q_ref, kc_ref, vc_ref, cover_ref, rsum_ref, ocmp_ref, idx_ref, *, n_cmp, n_past_blocks):
    nq = q_ref.shape[0]
    n_sub = kc_ref.shape[0]
    scale = NSA_HEAD_DIM ** -0.5
    pos = PAST_LEN + lax.broadcasted_iota(jnp.int32, (nq, 1), 0) // DEC_QROWS
    n = lax.broadcasted_iota(jnp.int32, (nq, n_sub), 1)
    ok = (n * CMP_STRIDE + (CMP_BLOCK - 1) <= pos) & (n < n_cmp)
    s = jnp.where(ok, _dot_nt(q_ref[...].astype(BF16), kc_ref[...]) * scale, NEG)
    e = jnp.where(ok, jnp.exp(s - jnp.max(s, axis=-1, keepdims=True)), 0.0)
    den = jnp.sum(e, axis=-1, keepdims=True)
    p = e / jnp.where(den > 0.0, den, 1.0)
    ocmp_ref[...] = jnp.dot(p.astype(BF16), vc_ref[...], preferred_element_type=F32)

    score = _dot_sel(_sel_dot(rsum_ref[...], p), cover_ref[...])
    post = PAST_LEN + lax.broadcasted_iota(jnp.int32, (score.shape[0], 1), 0)
    lane = lax.broadcasted_iota(jnp.int32, score.shape, 1)
    cur = post // SLC_BLOCK
    forced = (lane == 0) | (lane == cur) | (lane == cur - 1)
    score = jnp.where(forced, FORCE_SCORE, jnp.where(lane * SLC_BLOCK <= post, score, -1.0))
    rank = jnp.zeros(score.shape, jnp.int32)
    for i in range(n_past_blocks):
        ci = score[:, i:i + 1]
        rank = rank + ((ci > score) | ((ci == score) & (lane > i))).astype(jnp.int32)
    out_lane = lax.broadcasted_iota(jnp.int32, idx_ref.shape, 1)
    out = jnp.zeros(idx_ref.shape, jnp.int32)
    for k in range(N_SELECT - 1):
        idx_k = jnp.sum(jnp.where(rank == k, lane.astype(F32), 0.0), axis=-1, keepdims=True)
        out = jnp.where(out_lane == k, idx_k.astype(jnp.int32), out)
    idx_ref[...] = out


def _decode_select(q_pad, cmp, ls):
    batch = q_pad.shape[0]
    nq = ls * DEC_QROWS
    n_sub = cmp.shape[3]
    n_past_blocks = PAST_LEN // SLC_BLOCK
    assert PAST_LEN % SLC_BLOCK == 0 and ls <= min(SLC_BLOCK, 8) and n_past_blocks >= N_SELECT
    c0 = np.arange(n_sub)[:, None] * CMP_STRIDE
    s0 = np.arange(n_past_blocks)[None, :] * SLC_BLOCK
    cover = np.clip(np.minimum(c0 + CMP_BLOCK, s0 + SLC_BLOCK) - np.maximum(c0, s0), 0, None) / CMP_BLOCK
    cover = cover * (np.arange(n_sub)[:, None] < n_sub - 1)
    rows = np.arange(nq)[None, :]
    rsum = (rows // DEC_QROWS == np.arange(8)[:, None]) & (rows % DEC_QROWS < NSA_GROUP)
    hd = NSA_HEAD_DIM
    q_spec = pl.BlockSpec((None, None, nq, hd), lambda b, g: (b, g, 0, 0))
    return pl.pallas_call(
        functools.partial(_decode_select_kernel, n_cmp=n_sub - 1, n_past_blocks=n_past_blocks),
        grid=(batch, NSA_KV_HEADS),
        in_specs=[q_spec,
                  pl.BlockSpec((None, None, None, n_sub, hd), lambda b, g: (0, b, g, 0, 0)),
                  pl.BlockSpec((None, None, None, n_sub, hd), lambda b, g: (1, b, g, 0, 0)),
                  pl.BlockSpec((n_sub, n_past_blocks), lambda b, g: (0, 0)),
                  pl.BlockSpec((8, nq), lambda b, g: (0, 0))],
        out_specs=[q_spec, pl.BlockSpec((None, None, 8, LANE), lambda b, g: (b, g, 0, 0))],
        out_shape=[jax.ShapeDtypeStruct((batch, NSA_KV_HEADS, nq, hd), F32),
                   jax.ShapeDtypeStruct((batch, NSA_KV_HEADS, 8, LANE), jnp.int32)],
        compiler_params=pltpu.CompilerParams(dimension_semantics=("arbitrary", "arbitrary"),
                                             vmem_limit_bytes=VMEM_LIMIT_BYTES),
        name="decode_select",
    )(q_pad, cmp, cmp, jnp.asarray(cover, BF16), jnp.asarray(rsum, BF16))


def _decode_attend_kernel(pt_ref, idx_ref, q_ref, ocmp_ref, gate_ref, new_ref, kbuf_ref, vbuf_ref, cache_ref,
                          o_ref, kg_s, vg_s, sem, *, ls, page0):
    b = pl.program_id(0)
    g = pl.program_id(1)
    n_sel = N_SELECT - 1
    hd = NSA_HEAD_DIM
    scale = hd ** -0.5

    def gather_copies():
        for t in range(ls):
            for k in range(n_sel):
                j = idx_ref[((b * NSA_KV_HEADS + g) * ls + t) * n_sel + k]
                page = page0 + pt_ref[b, j // (PAGE_SIZE // SLC_BLOCK)]
                row0 = (j % (PAGE_SIZE // SLC_BLOCK)) * SLC_BLOCK
                for which, (slot, dst) in enumerate(((2, kg_s), (3, vg_s))):
                    yield pltpu.make_async_copy(
                        cache_ref.at[page, pl.ds(row0, SLC_BLOCK), slot * NSA_KV_HEADS + g, :],
                        dst.at[t, k], sem.at[which])

    for cp in gather_copies():
        cp.start()

    nq = ls * DEC_QROWS
    new = new_ref[...].astype(BF16)
    q = q_ref[...].astype(BF16)
    tok = lax.broadcasted_iota(jnp.int32, (nq, 1), 0) // DEC_QROWS
    wb = kbuf_ref.shape[0]
    i_buf = lax.broadcasted_iota(jnp.int32, (nq, wb), 1)
    i_new = lax.broadcasted_iota(jnp.int32, (nq, new.shape[1]), 1)
    m_new = (i_new <= tok) & (i_new < ls)
    m_buf = (wb - i_buf + tok) < WINDOW
    s_b = jnp.where(m_buf, _dot_nt(q, kbuf_ref[...].astype(BF16)) * scale, NEG)
    s_n = jnp.where(m_new, _dot_nt(q, new[2]) * scale, NEG)
    mx = jnp.maximum(jnp.max(s_b, axis=-1, keepdims=True), jnp.max(s_n, axis=-1, keepdims=True))
    e_b = jnp.exp(s_b - mx)
    e_n = jnp.exp(s_n - mx)
    den = jnp.sum(e_b, axis=-1, keepdims=True) + jnp.sum(e_n, axis=-1, keepdims=True)
    o_swa = (jnp.dot(e_b.astype(BF16), vbuf_ref[...].astype(BF16), preferred_element_type=F32)
             + jnp.dot(e_n.astype(BF16), new[3], preferred_element_type=F32)) / den

    for cp in gather_copies():
        cp.wait()

    gates = gate_ref[...]
    for t in range(ls):
        rows = slice(t * DEC_QROWS, (t + 1) * DEC_QROWS)
        qt = q[rows]
        s_p = _dot_nt(qt, kg_s[t].reshape(n_sel * SLC_BLOCK, hd).astype(BF16)) * scale
        s_n = jnp.where(m_new[rows], _dot_nt(qt, new[0]) * scale, NEG)
        mx = jnp.maximum(jnp.max(s_p, axis=-1, keepdims=True), jnp.max(s_n, axis=-1, keepdims=True))
        e_p = jnp.exp(s_p - mx)
        e_n = jnp.exp(s_n - mx)
        den = jnp.sum(e_p, axis=-1, keepdims=True) + jnp.sum(e_n, axis=-1, keepdims=True)
        o_slc = (jnp.dot(e_p.astype(BF16), vg_s[t].reshape(n_sel * SLC_BLOCK, hd).astype(BF16),
                         preferred_element_type=F32)
                 + jnp.dot(e_n.astype(BF16), new[1], preferred_element_type=F32)) / den
        gt = gates[rows]
        o_ref[rows, :] = gt[:, 0:1] * ocmp_ref[rows, :] + gt[:, 1:2] * o_slc + gt[:, 2:3] * o_swa[rows]


def _decode_attend(q_pad, o_cmp, gates, new_rows, swa_buf, cache, page0, page_table, idx, ls):
    batch = q_pad.shape[0]
    nq = ls * DEC_QROWS
    hd = NSA_HEAD_DIM
    wb = swa_buf.shape[1]
    assert wb == WINDOW and PAGE_SIZE % SLC_BLOCK == 0
    n_sel = N_SELECT - 1
    q_spec = pl.BlockSpec((None, None, nq, hd), lambda b, g, pt, ix: (b, g, 0, 0))

    def buf_spec(slot):
        return pl.BlockSpec((None, wb, hd), lambda b, g, pt, ix: (b, 0, slot * NSA_KV_HEADS + g))

    return pl.pallas_call(
        functools.partial(_decode_attend_kernel, ls=ls, page0=page0),
        grid_spec=pltpu.PrefetchScalarGridSpec(
            num_scalar_prefetch=2,
            grid=(batch, NSA_KV_HEADS),
            in_specs=[q_spec, q_spec,
                      pl.BlockSpec((None, None, nq, LANE), lambda b, g, pt, ix: (b, g, 0, 0)),
                      pl.BlockSpec((None, None, 4, 8, hd), lambda b, g, pt, ix: (b, g, 0, 0, 0)),
                      buf_spec(0), buf_spec(1),
                      pl.BlockSpec(memory_space=pl.ANY)],
            out_specs=q_spec,
            scratch_shapes=[pltpu.VMEM((ls, n_sel, SLC_BLOCK, hd), F32), pltpu.VMEM((ls, n_sel, SLC_BLOCK, hd), F32),
                            pltpu.SemaphoreType.DMA((2,))]),
        out_shape=jax.ShapeDtypeStruct((batch, NSA_KV_HEADS, nq, hd), F32),
        compiler_params=pltpu.CompilerParams(dimension_semantics=("arbitrary", "arbitrary"),
                                             vmem_limit_bytes=VMEM_LIMIT_BYTES),
        name="decode_attend",
    )(page_table, idx, q_pad, o_cmp, gates, new_rows,
      swa_buf.reshape(batch, wb, 2 * NSA_KV_HEADS * hd), swa_buf.reshape(batch, wb, 2 * NSA_KV_HEADS * hd),
      cache)


HG_SLAB = 256
HG_UNROLL = 8


class _ColWindow:
    def __init__(self, lo_ref, hi_ref, off):
        self.lo, self.hi, self.off = lo_ref, hi_ref, off

    def __getitem__(self, idx):
        a = self.lo[idx]
        if self.off == 0:
            return a
        shift = LANE - self.off
        lane = lax.broadcasted_iota(jnp.int32, a.shape, 1)
        return jnp.where(lane < shift, pltpu.roll(a, shift, 1), pltpu.roll(self.hi[idx], shift, 1))


def _hgrn_kernel(*refs, seq, off):
    ins = refs[:8]
    lb_ref, gn_ref, s0_ref, tri_ref, o_ref, s_ref, qs_s, el_s, o_s, u_s, st_s = refs[8:]
    hq_ref, hf_ref, hi_ref, hg_ref = (_ColWindow(ins[2 * i], ins[2 * i + 1], off) for i in range(4))
    c = HG_CHUNK
    per_slab = HG_SLAB // c
    lb = lb_ref[...]

    def prep(i, carry):
        rows = pl.ds(pl.multiple_of(i * HG_SLAB, HG_SLAB), HG_SLAB)
        hq = hq_ref[rows, :]
        q = hq * jax.nn.sigmoid(hq)
        f = lb + (1.0 - lb) * jax.nn.sigmoid(hf_ref[rows, :])
        k = 1.0 - f
        v = hi_ref[rows, :].astype(BF16)
        bc = _sel_dot(tri_ref[0], jnp.log(f))
        bc3 = bc.reshape(per_slab, c, LANE)
        mid = (c - 1) // 2
        bm = jnp.broadcast_to(bc3[:, mid:mid + 1, :], bc3.shape).reshape(HG_SLAB, LANE)
        bl = jnp.broadcast_to(bc3[:, c - 1:c, :], bc3.shape).reshape(HG_SLAB, LANE)
        a = _dot_nt((q * jnp.exp(bc - bm)).astype(BF16), (k * jnp.exp(bm - bc)).astype(BF16))
        a = jnp.where(tri_ref[0] > 0, a, 0.0)
        o_s[rows, :] = jnp.dot(a.astype(BF16), v, preferred_element_type=F32)
        qs_s[rows, :] = (q * jnp.exp(bc)).astype(qs_s.dtype)
        kd = (k * jnp.exp(bl - bc)).astype(BF16)
        el = jnp.exp(bl)
        for j in range(per_slab):
            ch = slice(j * c, (j + 1) * c)
            u_s[i * per_slab + j] = lax.dot_general(v[ch], kd[ch], (((0,), (0,)), ((), ())),
                                                    preferred_element_type=F32)
            el_s[pl.ds(i * per_slab + j, 1), :] = el[j * c:j * c + 1]
        return carry

    lax.fori_loop(0, seq // HG_SLAB, prep, 0)

    def scan(ci, st):
        st_s[ci] = st.astype(st_s.dtype)
        return st * el_s[pl.ds(ci, 1), :] + u_s[ci]

    st = lax.fori_loop(0, seq // c, scan, s0_ref[...].T, unroll=HG_UNROLL)
    s_ref[...] = st.T

    gn = gn_ref[...]

    def finish(i, carry):
        rows = pl.ds(pl.multiple_of(i * HG_SLAB, HG_SLAB), HG_SLAB)
        qs = qs_s[rows, :]
        inter = [_dot_nt(qs[j * c:(j + 1) * c], st_s[i * per_slab + j]) for j in range(per_slab)]
        o = o_s[rows, :] + jnp.concatenate(inter, axis=0)
        hg = hg_ref[rows, :]
        o = o * lax.rsqrt(jnp.mean(o * o, axis=-1, keepdims=True) + EPS)
        o_ref[rows, :] = (o * gn * (hg * jax.nn.sigmoid(hg))).astype(o_ref.dtype)
        return carry

    lax.fori_loop(0, seq // HG_SLAB, finish, 0)


def _hgrn_tri():
    i = np.arange(HG_SLAB)[:, None]
    j = np.arange(HG_SLAB)[None, :]
    cum = ((i // HG_CHUNK) == (j // HG_CHUNK)) & (j <= i)
    return jnp.asarray(cum[None], BF16)


def _hgrn_prompt(proj, col0, lb, g_norm, s0, batch, seq):
    assert seq % HG_SLAB == 0 and HG_DK == LANE and HG_DV == LANE
    c0 = col0 // LANE

    def col_specs(group):
        return [pl.BlockSpec((seq, LANE), lambda b, h, part=part: (b, c0 + group * HG_HEADS + h + part))
                for part in ((0, 1) if col0 % LANE else (0, 0))]

    vec_spec = pl.BlockSpec((1, LANE), lambda b, h: (0, h))
    st_spec = pl.BlockSpec((None, None, HG_DK, HG_DV), lambda b, h: (b, h, 0, 0))
    return pl.pallas_call(
        functools.partial(_hgrn_kernel, seq=seq, off=col0 % LANE),
        grid=(batch, HG_HEADS),
        in_specs=col_specs(0) + col_specs(1) + col_specs(2) + col_specs(3) + [
            vec_spec, vec_spec, st_spec, pl.BlockSpec((1, HG_SLAB, HG_SLAB), lambda b, h: (0, 0, 0))],
        out_specs=[pl.BlockSpec((seq, LANE), lambda b, h: (b, h)), st_spec],
        out_shape=[jax.ShapeDtypeStruct((batch * seq, HG_V_W), BF16),
                   jax.ShapeDtypeStruct((batch, HG_HEADS, HG_DK, HG_DV), F32)],
        scratch_shapes=[pltpu.VMEM((seq, LANE), BF16), pltpu.VMEM((seq // HG_CHUNK, LANE), F32),
                        pltpu.VMEM((seq, LANE), F32), pltpu.VMEM((seq // HG_CHUNK, HG_DV, HG_DK), F32),
                        pltpu.VMEM((seq // HG_CHUNK, HG_DV, HG_DK), BF16)],
        compiler_params=pltpu.CompilerParams(
            dimension_semantics=("arbitrary", "arbitrary"),
            vmem_limit_bytes=VMEM_LIMIT_BYTES),
        name="hgrn_prompt",
    )(*([proj] * 8), lb, g_norm, s0, _hgrn_tri())


SSD_TL = 256
SSD_GW = SSM_D_INNER // SSM_GROUPS
SSD_HPG = SSM_HEADS // SSM_GROUPS


def _split3(x):
    hi = x.astype(BF16)
    r1 = x - hi.astype(F32)
    mid = r1.astype(BF16)
    lo = (r1 - mid.astype(F32)).astype(BF16)
    return hi, mid, lo


def _sel_dot(sel, x):
    hi, mid, lo = _split3(x)
    d = functools.partial(jnp.dot, preferred_element_type=F32)
    return d(sel, hi) + d(sel, mid) + d(sel, lo)


def _dot_sel(x, sel):
    hi, mid, lo = _split3(x)
    d = functools.partial(jnp.dot, preferred_element_type=F32)
    return d(hi, sel) + d(mid, sel) + d(lo, sel)


def _causal_conv_silu(x, prev8, w, bias):
    row8 = lax.broadcasted_iota(jnp.int32, prev8.shape, 0)
    acc = bias
    for k in range(SSM_CONV - 1, 0, -1):
        r = pltpu.roll(x, k, 0)
        top = jnp.where(row8 < k, pltpu.roll(prev8, k, 0), r[:8])
        acc = acc + jnp.concatenate([top, r[8:]], axis=0) * w[SSM_CONV - 1 - k:SSM_CONV - k]
    acc = acc + x * w[SSM_CONV - 1:SSM_CONV]
    return acc * jax.nn.sigmoid(acc)


def _ssd_kernel(z_ref, x_ref, b_ref, c_ref, dt_ref, cx0_ref, cb0_ref, cc0_ref, s0_ref,
                wx_ref, wb_ref, wc_ref, bx_ref, bb_ref, bc_ref, dtb_ref, alog_ref, d_ref, gn_ref,
                tri_ref, hsel_ref, o_ref, s_ref,
                st_s, tx_s, tb_s, tc_s, xdt_s, xdd_s, ce_s, le_s, bm_s, cm_s, y_s):
    lt = pl.program_id(2)
    c = SSM_CHUNK
    tl = SSD_TL
    hp = SSM_HEAD_DIM

    @pl.when(lt == 0)
    def _():
        st_s[...] = s0_ref[...].T
        for tail, c0 in ((tx_s, cx0_ref), (tb_s, cb0_ref), (tc_s, cc0_ref)):
            tail[...] = jnp.zeros(tail.shape, F32)
            tail[8 - (SSM_CONV - 1):8, :] = c0[...]

    x_raw = x_ref[...]
    b_raw = b_ref[...]
    c_raw = c_ref[...]
    xs = _causal_conv_silu(x_raw, tx_s[...], wx_ref[...], bx_ref[...])
    bm_s[...] = _causal_conv_silu(b_raw, tb_s[...], wb_ref[...], bb_ref[...]).astype(bm_s.dtype)
    cm_s[...] = _causal_conv_silu(c_raw, tc_s[...], wc_ref[...], bc_ref[...]).astype(cm_s.dtype)
    tx_s[...] = x_raw[tl - 8:]
    tb_s[...] = b_raw[tl - 8:]
    tc_s[...] = c_raw[tl - 8:]

    dt = jax.nn.softplus(dt_ref[...] + dtb_ref[...])
    cum = _sel_dot(tri_ref[0], dt * -jnp.exp(alog_ref[...]))
    hsel = hsel_ref[...]
    dt_e = _dot_sel(dt, hsel)
    cum_e = _dot_sel(cum, hsel)
    last_e = _dot_sel(_sel_dot(tri_ref[1], cum), hsel)
    xdt = xs * dt_e
    xdt_s[...] = xdt.astype(xdt_s.dtype)
    xdd_s[...] = (xdt * jnp.exp(last_e - cum_e)).astype(xdd_s.dtype)
    ce_s[...] = cum_e
    le_s[...] = jnp.exp(last_e)

    trow = lax.broadcasted_iota(jnp.int32, (c, SSD_GW), 0)
    lane_g = lax.broadcasted_iota(jnp.int32, (c, SSD_GW), 1)
    diag = (lane_g % hp) == trow
    t2 = lax.broadcasted_iota(jnp.int32, (c, LANE), 0)
    l2 = lax.broadcasted_iota(jnp.int32, (c, LANE), 1)
    tril2 = (l2 % hp) <= t2
    r3 = lax.broadcasted_iota(jnp.int32, (2 * c, LANE), 0)
    l3 = lax.broadcasted_iota(jnp.int32, (2 * c, LANE), 1)
    blockdiag = (r3 // c) == (l3 // hp)

    for ci in range(tl // c):
        rows = pl.ds(ci * c, c)
        cm = cm_s[rows, :]
        bm = bm_s[rows, :]
        cum_c = ce_s[rows, :]
        cum_row = jnp.sum(jnp.where(diag, cum_c, 0.0), axis=0, keepdims=True)
        cb2 = _dot_nt(cm, jnp.concatenate([bm, bm], axis=0))
        xdt_c = xdt_s[rows, :]
        st = st_s[...]
        y_inter = jnp.dot(cm, st.astype(cm.dtype), preferred_element_type=F32) * jnp.exp(cum_c)
        for j in range(SSD_GW // LANE):
            cols = slice(j * LANE, (j + 1) * LANE)
            seg = cum_c[:, cols] - cum_row[:, cols]
            w = (cb2 * jnp.where(tril2, jnp.exp(seg), 0.0)).astype(xdt_c.dtype)
            xj = xdt_c[:, cols]
            xbd = jnp.where(blockdiag, jnp.concatenate([xj, xj], axis=0), jnp.zeros_like(xj[:1, :1]))
            y_s[rows, cols] = y_inter[:, cols] + jnp.dot(w, xbd, preferred_element_type=F32)
        upd = lax.dot_general(bm, xdd_s[rows, :], (((0,), (0,)), ((), ())), preferred_element_type=F32)
        st_s[...] = st * le_s[pl.ds(ci * c, 1), :] + upd

    z = z_ref[...]
    y = (y_s[...] + d_ref[...] * xs) * (z * jax.nn.sigmoid(z))
    y = y * lax.rsqrt(jnp.mean(y * y, axis=-1, keepdims=True) + EPS)
    o_ref[...] = (y * gn_ref[...]).astype(o_ref.dtype)

    @pl.when(lt == pl.num_programs(2) - 1)
    def _():
        s_ref[...] = st_s[...].T


def _ssd_consts():
    i = np.arange(SSD_TL)[:, None]
    j = np.arange(SSD_TL)[None, :]
    cum = ((i // SSM_CHUNK) == (j // SSM_CHUNK)) & (j <= i)
    last = j == (i // SSM_CHUNK) * SSM_CHUNK + SSM_CHUNK - 1
    tri = jnp.asarray(np.stack([cum, last]), BF16)
    h = np.arange(SSM_HEADS)[None, :, None]
    g = np.arange(SSM_GROUPS)[:, None, None]
    lane = np.arange(SSD_GW)[None, None, :]
    hsel = jnp.asarray(h == g * SSD_HPG + lane // SSM_HEAD_DIM, BF16)
    return tri, hsel


def _ssd_prompt(proj, conv0, s0, conv_w, conv_b, dt_bias, a_log, d_skip, norm_g, batch, seq):
    assert seq % SSD_TL == 0 and SSM_STATE == LANE and SSM_HEADS == LANE and SSM_CHUNK == SSM_HEAD_DIM
    nl = seq // SSD_TL
    gpl = SSD_GW // LANE
    xb = SSM_D_INNER // SSD_GW
    bb = 2 * SSM_D_INNER // LANE
    cb = bb + SSM_GROUPS
    db = cb + SSM_GROUPS
    tri, hsel = _ssd_consts()
    d_e = jnp.repeat(d_skip.astype(F32), SSM_HEAD_DIM)[None]
    row = lambda a: a.astype(F32)[None]

    def rows_spec(width, col_fn):
        return pl.BlockSpec((SSD_TL, width), lambda b, g, t: (b * nl + t, col_fn(g)))

    def conv0_spec(width, col_fn):
        return pl.BlockSpec((None, SSM_CONV - 1, width), lambda b, g, t: (b, 0, col_fn(g)))

    def vec_spec(rows_, width, col_fn):
        return pl.BlockSpec((rows_, width), lambda b, g, t: (0, col_fn(g)))

    xcol = lambda g: g
    bcol = lambda g: SSM_D_INNER // LANE + g
    ccol = lambda g: SSM_D_INNER // LANE + SSM_GROUPS + g
    zero = lambda g: 0
    st_spec = pl.BlockSpec((None, SSD_GW, SSM_STATE), lambda b, g, t: (b, g, 0))
    return pl.pallas_call(
        _ssd_kernel,
        grid=(batch, SSM_GROUPS, nl),
        in_specs=[rows_spec(SSD_GW, xcol), rows_spec(SSD_GW, lambda g: xb + g),
                  rows_spec(LANE, lambda g: bb + g), rows_spec(LANE, lambda g: cb + g),
                  rows_spec(LANE, lambda g: db),
                  conv0_spec(SSD_GW, xcol), conv0_spec(LANE, bcol), conv0_spec(LANE, ccol), st_spec,
                  vec_spec(SSM_CONV, SSD_GW, xcol), vec_spec(SSM_CONV, LANE, bcol), vec_spec(SSM_CONV, LANE, ccol),
                  vec_spec(1, SSD_GW, xcol), vec_spec(1, LANE, bcol), vec_spec(1, LANE, ccol),
                  vec_spec(1, LANE, zero), vec_spec(1, LANE, zero),
                  vec_spec(1, SSD_GW, xcol), vec_spec(1, SSD_GW, xcol),
                  pl.BlockSpec((2, SSD_TL, SSD_TL), lambda b, g, t: (0, 0, 0)),
                  pl.BlockSpec((None, SSM_HEADS, SSD_GW), lambda b, g, t: (g, 0, 0))],
        out_specs=[rows_spec(SSD_GW, xcol), st_spec],
        out_shape=[jax.ShapeDtypeStruct((batch * seq, SSM_D_INNER), BF16),
                   jax.ShapeDtypeStruct((batch, SSM_HEADS * SSM_HEAD_DIM, SSM_STATE), F32)],
        scratch_shapes=[pltpu.VMEM((SSM_STATE, SSD_GW), F32),
                        pltpu.VMEM((8, SSD_GW), F32), pltpu.VMEM((8, LANE), F32), pltpu.VMEM((8, LANE), F32),
                        pltpu.VMEM((SSD_TL, SSD_GW), BF16), pltpu.VMEM((SSD_TL, SSD_GW), BF16),
                        pltpu.VMEM((SSD_TL, SSD_GW), F32), pltpu.VMEM((SSD_TL, SSD_GW), F32),
                        pltpu.VMEM((SSD_TL, LANE), BF16), pltpu.VMEM((SSD_TL, LANE), BF16),
                        pltpu.VMEM((SSD_TL, SSD_GW), F32)],
        compiler_params=pltpu.CompilerParams(
            dimension_semantics=("arbitrary", "arbitrary", "arbitrary"),
            vmem_limit_bytes=VMEM_LIMIT_BYTES),
        name="ssd_prompt",
    )(proj, proj, proj, proj, proj, conv0, conv0, conv0, s0,
      conv_w, conv_w, conv_w, row(conv_b), row(conv_b), row(conv_b), row(dt_bias), row(a_log),
      d_e, row(norm_g), tri, hsel)


def split_cols(a, sizes):
    out, o = [], 0
    for s in sizes:
        out.append(a[..., o:o + s])
        o += s
    return out


def to_chunks(a, c):
    b, l = a.shape[:2]
    pad = (-l) % c
    a = jnp.pad(a.astype(F32), [(0, 0), (0, pad)] + [(0, 0)] * (a.ndim - 2))
    return jnp.moveaxis(a.reshape((b, (l + pad) // c, c) + a.shape[2:]), 1, 0)


def from_chunks(a, l):
    a = jnp.moveaxis(a, 0, 1)
    return a.reshape((a.shape[0], a.shape[1] * a.shape[2]) + a.shape[3:])[:, :l]


def gla_chunked(q, k, v, logf, s0):
    b, l = q.shape[:2]
    c = min(HG_CHUNK, l)
    mid = (c - 1) // 2
    tril = jnp.tril(jnp.ones((c, c), bool))

    def step(s, inp):
        qc, kc, vc, gc = inp
        bc = jnp.cumsum(gc, axis=1)
        bm = bc[:, mid:mid + 1]
        a = jnp.einsum('bthd,bshd->bhts', qc * jnp.exp(bc - bm), kc * jnp.exp(bm - bc))
        a = jnp.where(tril, a, 0.0)
        o = jnp.einsum('bhts,bshv->bthv', a, vc) + jnp.einsum('bthd,bhdv->bthv', qc * jnp.exp(bc), s)
        bl = bc[:, -1]
        s = jnp.exp(bl)[..., None] * s + jnp.einsum('bshd,bshv->bhdv', kc * jnp.exp(bl[:, None] - bc), vc)
        return s, o

    s, o = lax.scan(step, s0.astype(F32), (to_chunks(q, c), to_chunks(k, c), to_chunks(v, c), to_chunks(logf, c)))
    return from_chunks(o, l), s


def hgrn2(hq, hf, hi, hg, lb, s0, g_norm):
    b, l, _ = hq.shape
    q = jax.nn.silu(hq.astype(F32)).reshape(b, l, HG_HEADS, HG_DK)
    f = (lb + (1.0 - lb) * jax.nn.sigmoid(hf.astype(F32))).reshape(b, l, HG_HEADS, HG_DK)
    v = hi.astype(F32).reshape(b, l, HG_HEADS, HG_DV)
    o, s = gla_chunked(q, 1.0 - f, v, jnp.log(f), s0)
    o = o * lax.rsqrt(jnp.mean(o * o, axis=-1, keepdims=True) + EPS)
    o = o.reshape(b, l, HG_V_W) * g_norm.astype(F32) * jax.nn.silu(hg.astype(F32))
    return o, s


def nsa_prompt_branch(proj, b, l, w1, w2, pe):
    cmp = _compress_prompt(proj, w1, w2, pe, b, l)
    g = proj[:, AB_GATE_COL:AB_GATE_COL + NSA_GATE_W]
    gates = jax.nn.sigmoid(g).reshape(b * l, NSA_KV_HEADS, 3 * NSA_GROUP)
    gates = jnp.pad(gates, ((0, 0), (0, 0), (0, LANE - 3 * NSA_GROUP))).reshape(b * l, NSA_KV_HEADS * LANE)
    return _nsa_prompt(proj, cmp[0], cmp[1], gates, b, l)


AB_SIZES = (NSA_Q, NSA_KV_W, NSA_GATE_W, HG_QF_W, HG_QF_W, HG_V_W, HG_V_W)
AB_IN = sum(AB_SIZES)
AB_GATE_COL = NSA_Q + NSA_KV_W
AB_HG_COL = AB_GATE_COL + NSA_GATE_W


def ab_mixer_prompt(x, h, b, l, hg_state, lb, w_in, w1, w2, pe, hg_g, w_out):
    proj = _mm(h, w_in)
    o_nsa = nsa_prompt_branch(proj, b, l, w1, w2, pe)
    o_hg, s_new = _hgrn_prompt(proj, AB_HG_COL, lb[None], hg_g[None], hg_state, b, l)
    x = _mm([o_nsa, o_hg], w_out, res=x)
    slot_w = NSA_KV_HEADS * NSA_HEAD_DIM
    rows = proj[:, NSA_Q:NSA_Q + 4 * slot_w].reshape(b, l, 4, NSA_KV_HEADS, NSA_HEAD_DIM)
    wb = min(WINDOW, l)
    buf = proj.reshape(b, l, AB_IN)[:, l - wb:, NSA_Q + 4 * slot_w:NSA_Q + 6 * slot_w]
    return x, rows, buf.reshape(b, wb, 2, NSA_KV_HEADS, NSA_HEAD_DIM), s_new


def nsa_decode_branch(q, kv, g, cache, page0, page_table, swa_buf, w1, w2, pe):
    b, ls = q.shape[:2]
    assert page_table.shape[1] * PAGE_SIZE == PAST_LEN
    hd = NSA_HEAD_DIM
    row_pad = DEC_QROWS - NSA_GROUP
    q_pad = q.reshape(b, ls, NSA_KV_HEADS, NSA_GROUP, hd).transpose(0, 2, 1, 3, 4)
    q_pad = jnp.pad(q_pad, ((0, 0), (0, 0), (0, 0), (0, row_pad), (0, 0))).reshape(b, NSA_KV_HEADS, ls * DEC_QROWS, hd)
    gates = jax.nn.sigmoid(g).reshape(b, ls, NSA_KV_HEADS, NSA_GROUP, 3).transpose(0, 2, 1, 3, 4)
    gates = jnp.pad(gates, ((0, 0), (0, 0), (0, 0), (0, row_pad), (0, LANE - 3)))
    gates = gates.reshape(b, NSA_KV_HEADS, ls * DEC_QROWS, LANE)
    new_rows = jnp.pad(kv[:, :, 2:6].transpose(0, 3, 2, 1, 4), ((0, 0), (0, 0), (0, 0), (0, 8 - ls), (0, 0)))
    cmp = _compress_paged(cache, page0, page_table, w1, w2, pe)
    o_cmp, idx = _decode_select(q_pad, cmp, ls)
    o = _decode_attend(q_pad, o_cmp, gates, new_rows, swa_buf, cache, page0, page_table,
                       idx[:, :, :ls, :N_SELECT - 1].reshape(-1), ls)
    o = o.reshape(b, NSA_KV_HEADS, ls, DEC_QROWS, hd)[:, :, :, :NSA_GROUP]
    return o.transpose(0, 2, 1, 3, 4).reshape(b * ls, NSA_Q)


def ab_mixer_sample(x, h, b, l, cache, page0, page_table, swa_buf, hg_state, lb, w_in, w1, w2, pe, hg_g, w_out):
    q, kv, g, hq, hf, hi, hgate = split_cols(_mm(h, w_in).reshape(b, l, AB_IN), AB_SIZES)
    kv = kv.reshape(b, l, 6, NSA_KV_HEADS, NSA_HEAD_DIM)
    o_nsa = nsa_decode_branch(q, kv, g, cache, page0, page_table, swa_buf, w1, w2, pe)
    new_buf = jnp.concatenate([swa_buf.astype(kv.dtype), kv[:, :, 4:]], axis=1)[:, l:]
    o_hg, s_new = hgrn2(hq, hf, hi, hgate, lb, hg_state, hg_g)
    mix = jnp.concatenate([o_nsa, o_hg.reshape(b * l, HG_V_W)], axis=-1).astype(BF16)
    return _mm(mix, w_out, res=x), kv[:, :, :4], new_buf, s_new


def ssd_chunked(x, dt, a, bm, cm, s0):
    b, l, nh, p = x.shape
    g, n = bm.shape[2], bm.shape[3]
    r = nh // g
    c = min(SSM_CHUNK, l)
    tril = jnp.tril(jnp.ones((c, c), bool))

    def step(s, inp):
        xc, dtc, bc, cc = inp
        cum = jnp.cumsum(dtc * a, axis=1)
        seg = cum[:, :, None, :] - cum[:, None, :, :]
        lm = jnp.exp(jnp.where(tril[None, :, :, None], seg, -jnp.inf)).reshape(b, c, c, g, r)
        xdt = (xc * dtc[..., None]).reshape(b, c, g, r, p)
        cb = jnp.einsum('btgn,bsgn->btsg', cc, bc)
        sg = s.reshape(b, g, r, p, n)
        y = jnp.einsum('btsg,btsgr,bsgrp->btgrp', cb, lm, xdt)
        y = y + jnp.einsum('btgn,bgrpn->btgrp', cc, sg) * jnp.exp(cum).reshape(b, c, g, r)[..., None]
        dec = jnp.exp(cum[:, -1:] - cum).reshape(b, c, g, r)
        sg = jnp.exp(cum[:, -1]).reshape(b, g, r)[..., None, None] * sg + jnp.einsum('bsgn,bsgrp->bgrpn', bc, xdt * dec[..., None])
        return sg.reshape(b, nh, p, n), y.reshape(b, c, nh, p)

    s, y = lax.scan(step, s0.astype(F32), (to_chunks(x, c), to_chunks(dt, c), to_chunks(bm, c), to_chunks(cm, c)))
    return from_chunks(y, l), s


def mamba_prompt(x, h, b, l, w_in, conv_w, conv_b, dt_bias, a_log, d_skip, norm_g, w_out):
    proj = _mm(h, w_in)
    y, s = _ssd_prompt(proj, jnp.zeros((b, SSM_CONV - 1, SSM_CONV_DIM), F32),
                       jnp.zeros((b, SSM_HEADS * SSM_HEAD_DIM, SSM_STATE), F32),
                       conv_w, conv_b, dt_bias, a_log, d_skip, norm_g, b, l)
    assert l >= SSM_CONV - 1
    new_conv = proj.reshape(b, l, -1)[:, l - (SSM_CONV - 1):, SSM_D_INNER:SSM_D_INNER + SSM_CONV_DIM]
    return _mm(y, w_out, res=x), new_conv, s.reshape(b, SSM_HEADS, SSM_HEAD_DIM, SSM_STATE)


def mamba_sample(x, h, b, l, conv_state, ssm_state, w_in, conv_w, conv_b, dt_bias, a_log, d_skip, norm_g, w_out):
    z, xbc, dt = split_cols(_mm(h, w_in).reshape(b, l, -1), (SSM_D_INNER, SSM_CONV_DIM, SSM_HEADS))
    xpad = jnp.concatenate([conv_state.astype(xbc.dtype), xbc], axis=1)
    acc = conv_b.astype(F32)
    for j in range(SSM_CONV):
        acc = acc + xpad[:, j:j + l].astype(F32) * conv_w[j].astype(F32)
    xbc = jax.nn.silu(acc)
    new_conv = xpad[:, xpad.shape[1] - (SSM_CONV - 1):]
    xs, bm, cm = split_cols(xbc, (SSM_D_INNER, SSM_GROUPS * SSM_STATE, SSM_GROUPS * SSM_STATE))
    xs = xs.reshape(b, l, SSM_HEADS, SSM_HEAD_DIM)
    bm = bm.reshape(b, l, SSM_GROUPS, SSM_STATE)
    cm = cm.reshape(b, l, SSM_GROUPS, SSM_STATE)
    dt = jax.nn.softplus(dt.astype(F32) + dt_bias.astype(F32))
    a = -jnp.exp(a_log.astype(F32))
    y, s = ssd_chunked(xs, dt, a, bm, cm, ssm_state)
    y = y + d_skip.astype(F32)[:, None] * xs
    y = (y.reshape(b, l, SSM_D_INNER) * jax.nn.silu(z.astype(F32))).reshape(b, l, SSM_GROUPS, SSM_D_INNER // SSM_GROUPS)
    y = (y * lax.rsqrt(jnp.mean(y * y, axis=-1, keepdims=True) + EPS)).reshape(b, l, SSM_D_INNER) * norm_g.astype(F32)
    return _mm(y.reshape(b * l, SSM_D_INNER).astype(BF16), w_out, res=x), new_conv, s


def ffn_ple(x, p, g_ffn, w_ffn_in, w_ffn_out, g_ple, w_ple_gate, w_ple_up):
    x = _mm(_mm_swiglu(_rms(x, g_ffn, BF16), w_ffn_in), w_ffn_out, res=x)
    return _mm_ple(_rms(x, g_ple, BF16), w_ple_gate, p, w_ple_up, x)


def kernel(x_prompt, x_sample, cache_nsa_kv, cache_swa_kv, state_hgrn, state_ssm, cache_conv, page_table, p_prompt, p_sample, norm_mix, norm_ffn, w_ab_in, w_cmp1, w_cmp2, cmp_pe, hg_lb_logits, hg_norm, w_ab_out, w_ssm_in, ssm_conv_w, ssm_conv_b, ssm_dt_bias, ssm_a_log, ssm_d, ssm_norm, w_ssm_out, w_ffn_in, w_ffn_out, w_ple_up, w_ple_gate, norm_ple, norm_final):
    depth = norm_mix.shape[0]
    bp, lp, d = x_prompt.shape
    bs, ls, _ = x_sample.shape
    n_pool = cache_nsa_kv.shape[1]
    cache_rows = cache_nsa_kv.reshape(cache_nsa_kv.shape[0] * n_pool, PAGE_SIZE, 4 * NSA_KV_HEADS, NSA_HEAD_DIM)
    lb_all = jnp.cumsum(jax.nn.softmax(hg_lb_logits.astype(F32), axis=0), axis=0)
    w_ffn_out_b = w_ffn_out.astype(BF16)
    w_ssm_out_b = w_ssm_out.astype(BF16)
    xp = x_prompt.reshape(bp * lp, d)
    xs = x_sample.reshape(bs * ls, d)
    nsa_p, nsa_s, swa_p, swa_s, hg_p, hg_s, ssm_p, ssm_s, cv_p, cv_s = [], [], [], [], [], [], [], [], [], []
    for i in range(depth):
        hp = _rms(xp, norm_mix[i], BF16)
        hs = _rms(xs, norm_mix[i], BF16)
        if i % 2 == 0:
            a = i // 2
            wa = (lb_all[a], Stacked(w_ab_in, a), w_cmp1[a], w_cmp2[a], cmp_pe[a], hg_norm[a],
                  Stacked(w_ab_out, a))
            xp, r_p, b_p, s_p = ab_mixer_prompt(xp, hp, bp, lp, jnp.zeros((bp, HG_HEADS, HG_DK, HG_DV), F32), *wa)
            xs, r_s, b_s, s_s = ab_mixer_sample(xs, hs, bs, ls, cache_rows, a * n_pool, page_table,
                                                cache_swa_kv[a], state_hgrn[a], *wa)
            nsa_p.append(r_p)
            nsa_s.append(r_s)
            swa_p.append(b_p)
            swa_s.append(b_s)
            hg_p.append(s_p)
            hg_s.append(s_s)
        else:
            c = i // 2
            wc = (Stacked(w_ssm_in, c), ssm_conv_w[c], ssm_conv_b[c], ssm_dt_bias[c], ssm_a_log[c], ssm_d[c],
                  ssm_norm[c], Stacked(w_ssm_out_b, c))
            xp, c_p, t_p = mamba_prompt(xp, hp, bp, lp, *wc)
            xs, c_s, t_s = mamba_sample(xs, hs, bs, ls, cache_conv[c], state_ssm[c], *wc)
            cv_p.append(c_p)
            cv_s.append(c_s)
            ssm_p.append(t_p)
            ssm_s.append(t_s)
        wf = (norm_ffn[i], Stacked(w_ffn_in, i), Stacked(w_ffn_out_b, i), norm_ple[i],
              Stacked(w_ple_gate, i), Stacked(w_ple_up, i))
        xp = ffn_ple(xp, p_prompt[i].reshape(bp * lp, PLE_DIM).astype(BF16), *wf)
        xs = ffn_ple(xs, p_sample[i].reshape(bs * ls, PLE_DIM).astype(BF16), *wf)
    y_prompt = _rms(xp, norm_final, F32).reshape(bp, lp, d)
    y_sample = _rms(xs, norm_final, F32).reshape(bs, ls, d)
    return (y_prompt, y_sample, jnp.stack(nsa_p), jnp.stack(nsa_s), jnp.stack(swa_p), jnp.stack(swa_s),
            jnp.stack(hg_p), jnp.stack(hg_s), jnp.stack(ssm_p), jnp.stack(ssm_s), jnp.stack(cv_p), jnp.stack(cv_s))
```

```python
import functools
from typing import NamedTuple

import jax
import jax.numpy as jnp
import numpy as np
from jax import lax
from jax.experimental import pallas as pl
from jax.experimental.pallas import tpu as pltpu

D_MODEL = 4096
PAST_LEN = 16384
PAGE_SIZE = 128
PLE_DIM = 256
NSA_HEADS = 16
NSA_KV_HEADS = 4
NSA_GROUP = NSA_HEADS // NSA_KV_HEADS
NSA_HEAD_DIM = 128
NSA_Q = NSA_HEADS * NSA_HEAD_DIM
NSA_KV_W = 6 * NSA_KV_HEADS * NSA_HEAD_DIM
NSA_GATE_W = 3 * NSA_HEADS
CMP_BLOCK = 32
CMP_STRIDE = 16
CMP_HIDDEN = 256
SLC_BLOCK = 64
N_SELECT = 16
WINDOW = 512
FORCE_SCORE = 1e4
HG_HEADS = 16
HG_DK = 128
HG_DV = (D_MODEL // 2) // HG_HEADS
HG_QF_W = HG_HEADS * HG_DK
HG_V_W = HG_HEADS * HG_DV
HG_CHUNK = 32
SSM_D_INNER = 2 * D_MODEL
SSM_HEAD_DIM = 64
SSM_HEADS = SSM_D_INNER // SSM_HEAD_DIM
SSM_GROUPS = 8
SSM_STATE = 128
SSM_CONV = 4
SSM_CONV_DIM = SSM_D_INNER + 2 * SSM_GROUPS * SSM_STATE
SSM_CHUNK = 64
EPS = 1e-6
NEG = -1e30

LANE = 128
VMEM_LIMIT_BYTES = 56 * 1024 * 1024
BF16 = jnp.bfloat16
F32 = jnp.float32


MAX_ROW_TILE = 1024
NORM_ROW_TILE = 512


def _rms_kernel(x_ref, g_ref, o_ref):
    x = x_ref[...]
    y = x * lax.rsqrt(jnp.mean(x * x, axis=-1, keepdims=True) + EPS)
    o_ref[...] = (y * g_ref[...]).astype(o_ref.dtype)


def _rms(x, g, out_dtype):
    m, d = x.shape
    tm = min(m, NORM_ROW_TILE)
    assert m % tm == 0
    return pl.pallas_call(
        _rms_kernel,
        grid=(m // tm,),
        in_specs=[pl.BlockSpec((tm, d), lambda i: (i, 0)), pl.BlockSpec((1, d), lambda i: (0, 0))],
        out_specs=pl.BlockSpec((tm, d), lambda i: (i, 0)),
        out_shape=jax.ShapeDtypeStruct((m, d), out_dtype),
        compiler_params=pltpu.CompilerParams(dimension_semantics=("arbitrary",),
                                             vmem_limit_bytes=VMEM_LIMIT_BYTES),
        name="rms",
    )(x, g.astype(F32)[None])


def _mm_kernel(*refs, n_lhs, residual):
    x_refs = refs[:n_lhs]
    w_ref = refs[n_lhs]
    o_ref = refs[-1]
    k = pl.program_id(2)

    def first(x_ref):
        acc = jnp.dot(x_ref[...], w_ref[...].astype(BF16), preferred_element_type=F32)
        o_ref[...] = acc + refs[n_lhs + 1][...] if residual else acc

    def later(x_ref):
        o_ref[...] += jnp.dot(x_ref[...], w_ref[...].astype(BF16), preferred_element_type=F32)

    pl.when(k == 0)(functools.partial(first, x_refs[0]))
    if n_lhs == 1:
        pl.when(k > 0)(functools.partial(later, x_refs[0]))
    else:
        for p in range(1, n_lhs):
            pl.when(k == p)(functools.partial(later, x_refs[p]))


def _w_tile_bytes(w):
    return 2 * w.dtype.itemsize + (2 if w.dtype != BF16 else 0)


def _k_tile(k, x_bytes_per_k, w_bytes_per_k, budget):
    units = k // LANE
    for parts in range(1, units + 1):
        if units % parts == 0 and (x_bytes_per_k + w_bytes_per_k) * (units // parts) * LANE <= budget:
            return (units // parts) * LANE
    raise ValueError(f"no K tile for {k}")


class Stacked(NamedTuple):
    w: jax.Array
    layer: int


def _unstack(w):
    return (w.w, w.layer) if isinstance(w, Stacked) else (w, None)


def _w_spec(block, index_map, layer):
    if layer is None:
        return pl.BlockSpec(block, index_map)
    return pl.BlockSpec((None,) + block, lambda *g: (layer,) + index_map(*g))


def _mm(xs, w, res=None):
    xs = list(xs) if isinstance(xs, (list, tuple)) else [xs]
    m = xs[0].shape[0]
    w, layer = _unstack(w)
    k, n = w.shape[-2:]
    tm = min(m, MAX_ROW_TILE)
    n_out_bufs = 4 if res is not None else 2
    widest = 1024 if w.dtype == BF16 else 512
    for tn in (min(n, widest), min(n, widest // 2)):
        budget = VMEM_LIMIT_BYTES - n_out_bufs * tm * tn * 4 - (4 << 20)
        x_per_k = 2 * 2 * tm * len(xs)
        w_per_k = _w_tile_bytes(w) * tn
        tk = _k_tile(k, x_per_k, w_per_k, budget) if len(xs) == 1 else k // len(xs)
        if tk == k or tk >= 2048:
            break
    if len(xs) == 1:
        x_specs = [pl.BlockSpec((tm, tk), lambda i, j, kk: (i, kk))]
    else:
        assert all(x.shape[1] == tk for x in xs) and (x_per_k + w_per_k) * tk <= budget
        x_specs = [pl.BlockSpec((tm, tk), lambda i, j, kk: (i, 0)) for _ in xs]
    assert m % tm == 0 and k % tk == 0 and tk % LANE == 0
    tile = pl.BlockSpec((tm, tn), lambda i, j, kk: (i, j))
    return pl.pallas_call(
        functools.partial(_mm_kernel, n_lhs=len(xs), residual=res is not None),
        grid=(m // tm, pl.cdiv(n, tn), k // tk),
        in_specs=x_specs + [_w_spec((tk, tn), lambda i, j, kk: (kk, j), layer)] + ([tile] if res is not None else []),
        out_specs=tile,
        out_shape=jax.ShapeDtypeStruct((m, n), F32),
        compiler_params=pltpu.CompilerParams(
            dimension_semantics=("arbitrary", "arbitrary", "arbitrary"),
            vmem_limit_bytes=VMEM_LIMIT_BYTES),
        name="mm",
    )(*xs, w, *([res] if res is not None else []))


FFN_TILE = 256


def _swiglu_kernel(x_ref, wg_ref, wu_ref, o_ref):
    x = x_ref[...]
    gate = jnp.dot(x, wg_ref[...].astype(BF16), preferred_element_type=F32)
    up = jnp.dot(x, wu_ref[...].astype(BF16), preferred_element_type=F32)
    o_ref[...] = (gate * jax.nn.sigmoid(gate) * up).astype(o_ref.dtype)


def _mm_swiglu(x, w):
    m, k = x.shape
    w, layer = w
    hidden = w.shape[2] // 2
    tm = min(m, MAX_ROW_TILE)
    nj = hidden // FFN_TILE
    assert m % tm == 0 and hidden % FFN_TILE == 0 and w.shape[1] == k
    return pl.pallas_call(
        _swiglu_kernel,
        grid=(m // tm, nj),
        in_specs=[pl.BlockSpec((tm, k), lambda i, j: (i, 0)),
                  _w_spec((k, FFN_TILE), lambda i, j: (0, j), layer),
                  _w_spec((k, FFN_TILE), lambda i, j: (0, nj + j), layer)],
        out_specs=pl.BlockSpec((tm, FFN_TILE), lambda i, j: (i, j)),
        out_shape=jax.ShapeDtypeStruct((m, hidden), BF16),
        compiler_params=pltpu.CompilerParams(dimension_semantics=("arbitrary", "arbitrary"),
                                             vmem_limit_bytes=VMEM_LIMIT_BYTES),
        name="mm_swiglu",
    )(x, w, w)


PLE_COL_TILE = 512


def _ple_kernel(t_ref, wg_ref, p_ref, wu_ref, res_ref, o_ref):
    gate = jnp.dot(t_ref[...], wg_ref[...].astype(BF16), preferred_element_type=F32)
    up = jnp.dot(p_ref[...], wu_ref[...].astype(BF16), preferred_element_type=F32)
    o_ref[...] = res_ref[...] + up * jax.nn.sigmoid(gate)


def _mm_ple(t, w_gate, p, w_up, res):
    m, k = t.shape
    w_gate, layer = w_gate
    w_up, layer_up = w_up
    assert layer == layer_up
    n = w_gate.shape[2]
    kp = p.shape[1]
    tm = min(m, MAX_ROW_TILE)
    tn = PLE_COL_TILE
    assert m % tm == 0 and n % tn == 0
    tile = pl.BlockSpec((tm, tn), lambda i, j: (i, j))
    return pl.pallas_call(
        _ple_kernel,
        grid=(m // tm, n // tn),
        in_specs=[pl.BlockSpec((tm, k), lambda i, j: (i, 0)), _w_spec((k, tn), lambda i, j: (0, j), layer),
                  pl.BlockSpec((tm, kp), lambda i, j: (i, 0)), _w_spec((kp, tn), lambda i, j: (0, j), layer), tile],
        out_specs=tile,
        out_shape=jax.ShapeDtypeStruct((m, n), F32),
        compiler_params=pltpu.CompilerParams(dimension_semantics=("arbitrary", "arbitrary"),
                                             vmem_limit_bytes=VMEM_LIMIT_BYTES),
        name="mm_ple",
    )(t, w_gate, p, w_up, res)


NSA_TQ = 256
NSA_KEY_BUCKET = 512


def _dot_nt(a, b):
    return lax.dot_general(a, b, (((1,), (1,)), ((), ())), preferred_element_type=F32)


def _group_attend(q4, k, v, mask, scale, tq):
    s4 = _dot_nt(q4, k) * scale
    es, dens = [], []
    for r in range(NSA_GROUP):
        s = jnp.where(mask, s4[r * tq:(r + 1) * tq], NEG)
        e = jnp.exp(s - jnp.max(s, axis=-1, keepdims=True))
        dens.append(jnp.sum(e, axis=-1, keepdims=True))
        es.append(e.astype(BF16))
    o4 = jnp.dot(jnp.concatenate(es, axis=0), v, preferred_element_type=F32)
    return [o4[r * tq:(r + 1) * tq] / dens[r] for r in range(NSA_GROUP)]


def _nsa_prompt_kernel(q_ref, kc_ref, vc_ref, ks_ref, vs_ref, kw_ref, vw_ref, gate_ref, cover_ref, expand_ref,
                       o_ref, slc_s, *, seq):
    tq = NSA_TQ
    hd = NSA_HEAD_DIM
    n_cmp = seq // CMP_STRIDE - CMP_BLOCK // CMP_STRIDE + 1
    n_slc = seq // SLC_BLOCK
    span = WINDOW + tq
    scale = hd ** -0.5
    q0 = pl.program_id(2) * tq
    pos = q0 + lax.broadcasted_iota(jnp.int32, (tq, 1), 0)
    lane = lax.broadcasted_iota(jnp.int32, (tq, LANE), 1)
    q4 = jnp.concatenate([q_ref[:, r * hd:(r + 1) * hd].astype(BF16) for r in range(NSA_GROUP)], axis=0)

    cmp_ok = (lane * CMP_STRIDE + (CMP_BLOCK - 1) <= pos) & (lane < n_cmp)
    s4 = _dot_nt(q4, kc_ref[...]) * scale
    psum = jnp.zeros((tq, LANE), F32)
    ps = []
    for r in range(NSA_GROUP):
        s = jnp.where(cmp_ok, s4[r * tq:(r + 1) * tq], NEG)
        e = jnp.where(cmp_ok, jnp.exp(s - jnp.max(s, axis=-1, keepdims=True)), 0.0)
        den = jnp.sum(e, axis=-1, keepdims=True)
        p = e / jnp.where(den > 0.0, den, 1.0)
        psum = psum + p
        ps.append(p.astype(BF16))
    o_cmp4 = jnp.dot(jnp.concatenate(ps, axis=0), vc_ref[...], preferred_element_type=F32)

    score = jnp.dot(psum, cover_ref[...], preferred_element_type=F32, precision=lax.Precision.HIGHEST)
    cur = pos // SLC_BLOCK
    forced = (lane == 0) | (lane == cur) | (lane == cur - 1)
    causal = lane * SLC_BLOCK <= pos
    score = jnp.where(forced, FORCE_SCORE, jnp.where(causal, score, -1.0))
    score = jnp.where(lane < n_slc, score, -2.0)
    rank = jnp.zeros((tq, LANE), jnp.int32)
    for i in range(n_slc):
        ci = score[:, i:i + 1]
        beats = (ci > score) | ((ci == score) & (lane > i))
        rank = rank + beats.astype(jnp.int32)
    sel = ((rank < min(N_SELECT, n_slc)) & (lane < n_slc)).astype(BF16)

    def selected(nk):
        sel_keys = jnp.dot(sel, expand_ref[:, :nk], preferred_element_type=F32)
        kpos = lax.broadcasted_iota(jnp.int32, (tq, nk), 1)
        outs = _group_attend(q4, ks_ref[:nk, :].astype(BF16), vs_ref[:nk, :].astype(BF16),
                             (sel_keys > 0.5) & (kpos <= pos), scale, tq)
        for r in range(NSA_GROUP):
            slc_s[r] = outs[r]

    n_bucket = (q0 + tq + NSA_KEY_BUCKET - 1) // NSA_KEY_BUCKET
    for nb in range(1, seq // NSA_KEY_BUCKET + 1):
        pl.when(n_bucket == nb)(functools.partial(selected, nb * NSA_KEY_BUCKET))

    w0 = pl.multiple_of(jnp.maximum(q0 - WINDOW, 0), tq)
    wpos = w0 + lax.broadcasted_iota(jnp.int32, (tq, span), 1)
    o_swa = _group_attend(q4, kw_ref[pl.ds(w0, span), :].astype(BF16), vw_ref[pl.ds(w0, span), :].astype(BF16),
                          (wpos <= pos) & (pos - wpos < WINDOW), scale, tq)

    for r in range(NSA_GROUP):
        g = gate_ref[:, 3 * r:3 * r + 3]
        o = g[:, 0:1] * o_cmp4[r * tq:(r + 1) * tq] + g[:, 1:2] * slc_s[r] + g[:, 2:3] * o_swa[r]
        o_ref[:, r * hd:(r + 1) * hd] = o.astype(o_ref.dtype)


def _nsa_prompt(proj, kc, vc, gates, batch, seq):
    assert seq % NSA_TQ == 0 and seq // SLC_BLOCK <= LANE and seq // CMP_STRIDE <= LANE + 1
    assert seq % NSA_KEY_BUCKET == 0 and seq >= WINDOW + NSA_TQ and WINDOW % NSA_TQ == 0
    nt = seq // NSA_TQ
    n_cmp = seq // CMP_STRIDE - CMP_BLOCK // CMP_STRIDE + 1
    n_slc = seq // SLC_BLOCK
    c0 = np.arange(LANE)[:, None] * CMP_STRIDE
    s0 = np.arange(LANE)[None, :] * SLC_BLOCK
    cover = np.clip(np.minimum(c0 + CMP_BLOCK, s0 + SLC_BLOCK) - np.maximum(c0, s0), 0, None) / CMP_BLOCK
    cover = cover * (np.arange(LANE)[:, None] < n_cmp) * (np.arange(LANE)[None, :] < n_slc)
    expand = (np.arange(seq)[None, :] // SLC_BLOCK == np.arange(LANE)[:, None])
    hd = NSA_HEAD_DIM
    gw = NSA_GROUP * hd

    def kv_spec(slot):
        return pl.BlockSpec((seq, hd), lambda b, g, t: (b, NSA_Q // hd + slot * NSA_KV_HEADS + g))

    cmp_spec = pl.BlockSpec((None, None, LANE, hd), lambda b, g, t: (b, g, 0, 0))
    return pl.pallas_call(
        functools.partial(_nsa_prompt_kernel, seq=seq),
        grid=(batch, NSA_KV_HEADS, nt),
        in_specs=[pl.BlockSpec((NSA_TQ, gw), lambda b, g, t: (b * nt + t, g)),
                  cmp_spec, cmp_spec, kv_spec(2), kv_spec(3), kv_spec(4), kv_spec(5),
                  pl.BlockSpec((NSA_TQ, LANE), lambda b, g, t: (b * nt + t, g)),
                  pl.BlockSpec((LANE, LANE), lambda b, g, t: (0, 0)),
                  pl.BlockSpec((LANE, seq), lambda b, g, t: (0, 0))],
        out_specs=pl.BlockSpec((NSA_TQ, gw), lambda b, g, t: (b * nt + t, g)),
        out_shape=jax.ShapeDtypeStruct((batch * seq, NSA_Q), BF16),
        scratch_shapes=[pltpu.VMEM((NSA_GROUP, NSA_TQ, hd), F32)],
        compiler_params=pltpu.CompilerParams(
            dimension_semantics=("arbitrary", "arbitrary", "arbitrary"),
            vmem_limit_bytes=VMEM_LIMIT_BYTES),
        name="nsa_prompt",
    )(proj, kc, vc, proj, proj, proj, proj, gates, jnp.asarray(cover, F32), jnp.asarray(expand, BF16))


def _compress_kernel(x_ref, w1_ref, w2_ref, pe_ref, o_ref, *, seq):
    n_sub = seq // CMP_STRIDE
    n_cmp = n_sub - CMP_BLOCK // CMP_STRIDE + 1
    acc = jnp.zeros((n_sub, 2 * CMP_HIDDEN), F32)
    for r in range(CMP_STRIDE):
        xr = x_ref[pl.ds(r, n_sub, stride=CMP_STRIDE), :].astype(BF16)
        acc = acc + jnp.dot(xr, w1_ref[r], preferred_element_type=F32)
    hid = acc[:, :CMP_HIDDEN] + pltpu.roll(acc[:, CMP_HIDDEN:], n_sub - 1, 0) + pe_ref[...]
    c = jnp.dot((hid * jax.nn.sigmoid(hid)).astype(BF16), w2_ref[...], preferred_element_type=F32)
    row = lax.broadcasted_iota(jnp.int32, c.shape, 0)
    o_ref[...] = jnp.where(row < n_cmp, c, 0.0).astype(o_ref.dtype)


def _compress_prompt(proj, w1, w2, pe, batch, seq):
    assert CMP_BLOCK == 2 * CMP_STRIDE and seq // CMP_STRIDE == LANE and NSA_HEAD_DIM == LANE
    hd = NSA_HEAD_DIM
    w1p = w1.reshape(2, 2, CMP_STRIDE, hd, CMP_HIDDEN).transpose(0, 2, 3, 1, 4)
    w1p = w1p.reshape(2, CMP_STRIDE, hd, 2 * CMP_HIDDEN).astype(BF16)
    pe_h = jnp.einsum('sc,sch->sh', pe.reshape(2, -1), w1)[:, None, :]
    return pl.pallas_call(
        functools.partial(_compress_kernel, seq=seq),
        grid=(batch, 2, NSA_KV_HEADS),
        in_specs=[pl.BlockSpec((seq, hd), lambda b, s, g: (b, NSA_Q // hd + s * NSA_KV_HEADS + g)),
                  pl.BlockSpec((None, CMP_STRIDE, hd, 2 * CMP_HIDDEN), lambda b, s, g: (s, 0, 0, 0)),
                  pl.BlockSpec((None, CMP_HIDDEN, hd), lambda b, s, g: (s, 0, 0)),
                  pl.BlockSpec((None, 1, CMP_HIDDEN), lambda b, s, g: (s, 0, 0))],
        out_specs=pl.BlockSpec((None, None, None, LANE, hd), lambda b, s, g: (s, b, g, 0, 0)),
        out_shape=jax.ShapeDtypeStruct((2, batch, NSA_KV_HEADS, LANE, hd), BF16),
        compiler_params=pltpu.CompilerParams(dimension_semantics=("arbitrary", "arbitrary", "arbitrary"),
                                             vmem_limit_bytes=VMEM_LIMIT_BYTES),
        name="compress_prompt",
    )(proj, w1p, w2.astype(BF16), pe_h)


CMP_PAGES = 16
DEC_QROWS = 8


def _compress_paged_kernel(pt_ref, *refs, n_cmp):
    pages = refs[:CMP_PAGES + 1]
    w1_ref, w2_ref, pe_ref, o_ref = refs[CMP_PAGES + 1:]
    per_row = 4 * NSA_KV_HEADS
    sub_pp = PAGE_SIZE // CMP_STRIDE
    own = CMP_PAGES * sub_pp
    tot = own + sub_pp
    first = pl.program_id(1) * own
    for slot in range(2):
        acc = jnp.zeros((NSA_KV_HEADS * tot, 2 * CMP_HIDDEN), F32)
        for r in range(CMP_STRIDE):
            xr = jnp.concatenate(
                [page[pl.ds(r * per_row + slot * NSA_KV_HEADS + g, sub_pp, stride=CMP_STRIDE * per_row), :]
                 for g in range(NSA_KV_HEADS) for page in pages], axis=0).astype(BF16)
            acc = acc + jnp.dot(xr, w1_ref[slot, r], preferred_element_type=F32)
        for g in range(NSA_KV_HEADS):
            blk = acc[g * tot:(g + 1) * tot]
            hid = (blk[:, :CMP_HIDDEN] + pltpu.roll(blk[:, CMP_HIDDEN:], tot - 1, 0))[:own] + pe_ref[slot]
            c = jnp.dot((hid * jax.nn.sigmoid(hid)).astype(BF16), w2_ref[slot], preferred_element_type=F32)
            sub = first + lax.broadcasted_iota(jnp.int32, c.shape, 0)
            o_ref[slot, g] = jnp.where(sub < n_cmp, c, 0.0).astype(o_ref.dtype)


def _compress_paged(cache, page0, page_table, w1, w2, pe):
    batch, n_pages = page_table.shape
    hd = NSA_HEAD_DIM
    sub_pp = PAGE_SIZE // CMP_STRIDE
    n_sub = n_pages * sub_pp
    assert n_pages % CMP_PAGES == 0 and PAGE_SIZE % CMP_STRIDE == 0 and CMP_BLOCK == 2 * CMP_STRIDE
    w1p = w1.reshape(2, 2, CMP_STRIDE, hd, CMP_HIDDEN).transpose(0, 2, 3, 1, 4)
    w1p = w1p.reshape(2, CMP_STRIDE, hd, 2 * CMP_HIDDEN).astype(BF16)
    pe_h = jnp.einsum('sc,sch->sh', pe.reshape(2, -1), w1)[:, None, :]

    def page_spec(k):
        return pl.BlockSpec((PAGE_SIZE * 4 * NSA_KV_HEADS, hd),
                            lambda b, s, pt: (page0 + pt[b, jnp.minimum(s * CMP_PAGES + k, n_pages - 1)], 0))

    full = lambda shape: pl.BlockSpec(shape, lambda b, s, pt: (0,) * len(shape))
    return pl.pallas_call(
        functools.partial(_compress_paged_kernel, n_cmp=n_sub - 1),
        grid_spec=pltpu.PrefetchScalarGridSpec(
            num_scalar_prefetch=1,
            grid=(batch, n_pages // CMP_PAGES),
            in_specs=[page_spec(k) for k in range(CMP_PAGES + 1)] + [
                full((2, CMP_STRIDE, hd, 2 * CMP_HIDDEN)), full((2, CMP_HIDDEN, hd)), full((2, 1, CMP_HIDDEN))],
            out_specs=pl.BlockSpec((2, None, NSA_KV_HEADS, CMP_PAGES * sub_pp, hd), lambda b, s, pt: (0, b, 0, s, 0))),
        out_shape=jax.ShapeDtypeStruct((2, batch, NSA_KV_HEADS, n_sub, hd), BF16),
        compiler_params=pltpu.CompilerParams(dimension_semantics=("arbitrary", "arbitrary"),
                                             vmem_limit_bytes=VMEM_LIMIT_BYTES),
        name="compress_paged",
    )(page_table, *([cache.reshape(-1, hd)] * (CMP_PAGES + 1)), w1p, w2.astype(BF16), pe_h)


def _decode_select_kernel(q_ref, kc_ref, vc_ref, cover_ref, rsum_ref, ocmp_ref, idx_ref, *, n_cmp, n_past_blocks):
    nq = q_ref.shape[0]
    n_sub = kc_ref.shape[0]
    scale = NSA_HEAD_DIM ** -0.5
    pos = PAST_LEN + lax.broadcasted_iota(jnp.int32, (nq, 1), 0) // DEC_QROWS
    n = lax.broadcasted_iota(jnp.int32, (nq, n_sub), 1)
    ok = (n * CMP_STRIDE + (CMP_BLOCK - 1) <= pos) & (n < n_cmp)
    s = jnp.where(ok, _dot_nt(q_ref[...].astype(BF16), kc_ref[...]) * scale, NEG)
    e = jnp.where(ok, jnp.exp(s - jnp.max(s, axis=-1, keepdims=True)), 0.0)
    den = jnp.sum(e, axis=-1, keepdims=True)
    p = e / jnp.where(den > 0.0, den, 1.0)
    ocmp_ref[...] = jnp.dot(p.astype(BF16), vc_ref[...], preferred_element_type=F32)

    score = _dot_sel(_sel_dot(rsum_ref[...], p), cover_ref[...])
    post = PAST_LEN + lax.broadcasted_iota(jnp.int32, (score.shape[0], 1), 0)
    lane = lax.broadcasted_iota(jnp.int32, score.shape, 1)
    cur = post // SLC_BLOCK
    forced = (lane == 0) | (lane == cur) | (lane == cur - 1)
    score = jnp.where(forced, FORCE_SCORE, jnp.where(lane * SLC_BLOCK <= post, score, -1.0))
    rank = jnp.zeros(score.shape, jnp.int32)
    for i in range(n_past_blocks):
        ci = score[:, i:i + 1]
        rank = rank + ((ci > score) | ((ci == score) & (lane > i))).astype(jnp.int32)
    out_lane = lax.broadcasted_iota(jnp.int32, idx_ref.shape, 1)
    out = jnp.zeros(idx_ref.shape, jnp.int32)
    for k in range(N_SELECT - 1):
        idx_k = jnp.sum(jnp.where(rank == k, lane.astype(F32), 0.0), axis=-1, keepdims=True)
        out = jnp.where(out_lane == k, idx_k.astype(jnp.int32), out)
    idx_ref[...] = out


def _decode_select(q_pad, cmp, ls):
    batch = q_pad.shape[0]
    nq = ls * DEC_QROWS
    n_sub = cmp.shape[3]
    n_past_blocks = PAST_LEN // SLC_BLOCK
    assert PAST_LEN % SLC_BLOCK == 0 and ls <= min(SLC_BLOCK, 8) and n_past_blocks >= N_SELECT
    c0 = np.arange(n_sub)[:, None] * CMP_STRIDE
    s0 = np.arange(n_past_blocks)[None, :] * SLC_BLOCK
    cover = np.clip(np.minimum(c0 + CMP_BLOCK, s0 + SLC_BLOCK) - np.maximum(c0, s0), 0, None) / CMP_BLOCK
    cover = cover * (np.arange(n_sub)[:, None] < n_sub - 1)
    rows = np.arange(nq)[None, :]
    rsum = (rows // DEC_QROWS == np.arange(8)[:, None]) & (rows % DEC_QROWS < NSA_GROUP)
    hd = NSA_HEAD_DIM
    q_spec = pl.BlockSpec((None, None, nq, hd), lambda b, g: (b, g, 0, 0))
    return pl.pallas_call(
        functools.partial(_decode_select_kernel, n_cmp=n_sub - 1, n_past_blocks=n_past_blocks),
        grid=(batch, NSA_KV_HEADS),
        in_specs=[q_spec,
                  pl.BlockSpec((None, None, None, n_sub, hd), lambda b, g: (0, b, g, 0, 0)),
                  pl.BlockSpec((None, None, None, n_sub, hd), lambda b, g: (1, b, g, 0, 0)),
                  pl.BlockSpec((n_sub, n_past_blocks), lambda b, g: (0, 0)),
                  pl.BlockSpec((8, nq), lambda b, g: (0, 0))],
        out_specs=[q_spec, pl.BlockSpec((None, None, 8, LANE), lambda b, g: (b, g, 0, 0))],
        out_shape=[jax.ShapeDtypeStruct((batch, NSA_KV_HEADS, nq, hd), F32),
                   jax.ShapeDtypeStruct((batch, NSA_KV_HEADS, 8, LANE), jnp.int32)],
        compiler_params=pltpu.CompilerParams(dimension_semantics=("arbitrary", "arbitrary"),
                                             vmem_limit_bytes=VMEM_LIMIT_BYTES),
        name="decode_select",
    )(q_pad, cmp, cmp, jnp.asarray(cover, BF16), jnp.asarray(rsum, BF16))


def _decode_attend_kernel(pt_ref, idx_ref, q_ref, ocmp_ref, gate_ref, new_ref, kbuf_ref, vbuf_ref, cache_ref,
                          o_ref, kg_s, vg_s, sem, *, ls, page0):
    b = pl.program_id(0)
    g = pl.program_id(1)
    n_sel = N_SELECT - 1
    hd = NSA_HEAD_DIM
    scale = hd ** -0.5

    def gather_copies():
        for t in range(ls):
            for k in range(n_sel):
                j = idx_ref[((b * NSA_KV_HEADS + g) * ls + t) * n_sel + k]
                page = page0 + pt_ref[b, j // (PAGE_SIZE // SLC_BLOCK)]
                row0 = (j % (PAGE_SIZE // SLC_BLOCK)) * SLC_BLOCK
                for which, (slot, dst) in enumerate(((2, kg_s), (3, vg_s))):
                    yield pltpu.make_async_copy(
                        cache_ref.at[page, pl.ds(row0, SLC_BLOCK), slot * NSA_KV_HEADS + g, :],
                        dst.at[t, k], sem.at[which])

    for cp in gather_copies():
        cp.start()

    nq = ls * DEC_QROWS
    new = new_ref[...].astype(BF16)
    q = q_ref[...].astype(BF16)
    tok = lax.broadcasted_iota(jnp.int32, (nq, 1), 0) // DEC_QROWS
    wb = kbuf_ref.shape[0]
    i_buf = lax.broadcasted_iota(jnp.int32, (nq, wb), 1)
    i_new = lax.broadcasted_iota(jnp.int32, (nq, new.shape[1]), 1)
    m_new = (i_new <= tok) & (i_new < ls)
    m_buf = (wb - i_buf + tok) < WINDOW
    s_b = jnp.where(m_buf, _dot_nt(q, kbuf_ref[...].astype(BF16)) * scale, NEG)
    s_n = jnp.where(m_new, _dot_nt(q, new[2]) * scale, NEG)
    mx = jnp.maximum(jnp.max(s_b, axis=-1, keepdims=True), jnp.max(s_n, axis=-1, keepdims=True))
    e_b = jnp.exp(s_b - mx)
    e_n = jnp.exp(s_n - mx)
    den = jnp.sum(e_b, axis=-1, keepdims=True) + jnp.sum(e_n, axis=-1, keepdims=True)
    o_swa = (jnp.dot(e_b.astype(BF16), vbuf_ref[...].astype(BF16), preferred_element_type=F32)
             + jnp.dot(e_n.astype(BF16), new[3], preferred_element_type=F32)) / den

    for cp in gather_copies():
        cp.wait()

    gates = gate_ref[...]
    for t in range(ls):
        rows = slice(t * DEC_QROWS, (t + 1) * DEC_QROWS)
        qt = q[rows]
        s_p = _dot_nt(qt, kg_s[t].reshape(n_sel * SLC_BLOCK, hd).astype(BF16)) * scale
        s_n = jnp.where(m_new[rows], _dot_nt(qt, new[0]) * scale, NEG)
        mx = jnp.maximum(jnp.max(s_p, axis=-1, keepdims=True), jnp.max(s_n, axis=-1, keepdims=True))
        e_p = jnp.exp(s_p - mx)
        e_n = jnp.exp(s_n - mx)
        den = jnp.sum(e_p, axis=-1, keepdims=True) + jnp.sum(e_n, axis=-1, keepdims=True)
        o_slc = (jnp.dot(e_p.astype(BF16), vg_s[t].reshape(n_sel * SLC_BLOCK, hd).astype(BF16),
                         preferred_element_type=F32)
                 + jnp.dot(e_n.astype(BF16), new[1], preferred_element_type=F32)) / den
        gt = gates[rows]
        o_ref[rows, :] = gt[:, 0:1] * ocmp_ref[rows, :] + gt[:, 1:2] * o_slc + gt[:, 2:3] * o_swa[rows]


def _decode_attend(q_pad, o_cmp, gates, new_rows, swa_buf, cache, page0, page_table, idx, ls):
    batch = q_pad.shape[0]
    nq = ls * DEC_QROWS
    hd = NSA_HEAD_DIM
    wb = swa_buf.shape[1]
    assert wb == WINDOW and PAGE_SIZE % SLC_BLOCK == 0
    n_sel = N_SELECT - 1
    q_spec = pl.BlockSpec((None, None, nq, hd), lambda b, g, pt, ix: (b, g, 0, 0))

    def buf_spec(slot):
        return pl.BlockSpec((None, wb, hd), lambda b, g, pt, ix: (b, 0, slot * NSA_KV_HEADS + g))

    return pl.pallas_call(
        functools.partial(_decode_attend_kernel, ls=ls, page0=page0),
        grid_spec=pltpu.PrefetchScalarGridSpec(
            num_scalar_prefetch=2,
            grid=(batch, NSA_KV_HEADS),
            in_specs=[q_spec, q_spec,
                      pl.BlockSpec((None, None, nq, LANE), lambda b, g, pt, ix: (b, g, 0, 0)),
                      pl.BlockSpec((None, None, 4, 8, hd), lambda b, g, pt, ix: (b, g, 0, 0, 0)),
                      buf_spec(0), buf_spec(1),
                      pl.BlockSpec(memory_space=pl.ANY)],
            out_specs=q_spec,
            scratch_shapes=[pltpu.VMEM((ls, n_sel, SLC_BLOCK, hd), F32), pltpu.VMEM((ls, n_sel, SLC_BLOCK, hd), F32),
                            pltpu.SemaphoreType.DMA((2,))]),
        out_shape=jax.ShapeDtypeStruct((batch, NSA_KV_HEADS, nq, hd), F32),
        compiler_params=pltpu.CompilerParams(dimension_semantics=("arbitrary", "arbitrary"),
                                             vmem_limit_bytes=VMEM_LIMIT_BYTES),
        name="decode_attend",
    )(page_table, idx, q_pad, o_cmp, gates, new_rows,
      swa_buf.reshape(batch, wb, 2 * NSA_KV_HEADS * hd), swa_buf.reshape(batch, wb, 2 * NSA_KV_HEADS * hd),
      cache)


HG_SLAB = 256
HG_UNROLL = 8


class _ColWindow:
    def __init__(self, lo_ref, hi_ref, off):
        self.lo, self.hi, self.off = lo_ref, hi_ref, off

    def __getitem__(self, idx):
        a = self.lo[idx]
        if self.off == 0:
            return a
        shift = LANE - self.off
        lane = lax.broadcasted_iota(jnp.int32, a.shape, 1)
        return jnp.where(lane < shift, pltpu.roll(a, shift, 1), pltpu.roll(self.hi[idx], shift, 1))


def _hgrn_kernel(*refs, seq, off):
    ins = refs[:8]
    lb_ref, gn_ref, s0_ref, tri_ref, o_ref, s_ref, qs_s, el_s, o_s, u_s, st_s = refs[8:]
    hq_ref, hf_ref, hi_ref, hg_ref = (_ColWindow(ins[2 * i], ins[2 * i + 1], off) for i in range(4))
    c = HG_CHUNK
    per_slab = HG_SLAB // c
    lb = lb_ref[...]

    def prep(i, carry):
        rows = pl.ds(pl.multiple_of(i * HG_SLAB, HG_SLAB), HG_SLAB)
        hq = hq_ref[rows, :]
        q = hq * jax.nn.sigmoid(hq)
        f = lb + (1.0 - lb) * jax.nn.sigmoid(hf_ref[rows, :])
        k = 1.0 - f
        v = hi_ref[rows, :].astype(BF16)
        bc = _sel_dot(tri_ref[0], jnp.log(f))
        bc3 = bc.reshape(per_slab, c, LANE)
        mid = (c - 1) // 2
        bm = jnp.broadcast_to(bc3[:, mid:mid + 1, :], bc3.shape).reshape(HG_SLAB, LANE)
        bl = jnp.broadcast_to(bc3[:, c - 1:c, :], bc3.shape).reshape(HG_SLAB, LANE)
        a = _dot_nt((q * jnp.exp(bc - bm)).astype(BF16), (k * jnp.exp(bm - bc)).astype(BF16))
        a = jnp.where(tri_ref[0] > 0, a, 0.0)
        o_s[rows, :] = jnp.dot(a.astype(BF16), v, preferred_element_type=F32)
        qs_s[rows, :] = (q * jnp.exp(bc)).astype(qs_s.dtype)
        kd = (k * jnp.exp(bl - bc)).astype(BF16)
        el = jnp.exp(bl)
        for j in range(per_slab):
            ch = slice(j * c, (j + 1) * c)
            u_s[i * per_slab + j] = lax.dot_general(v[ch], kd[ch], (((0,), (0,)), ((), ())),
                                                    preferred_element_type=F32)
            el_s[pl.ds(i * per_slab + j, 1), :] = el[j * c:j * c + 1]
        return carry

    lax.fori_loop(0, seq // HG_SLAB, prep, 0)

    def scan(ci, st):
        st_s[ci] = st.astype(st_s.dtype)
        return st * el_s[pl.ds(ci, 1), :] + u_s[ci]

    st = lax.fori_loop(0, seq // c, scan, s0_ref[...].T, unroll=HG_UNROLL)
    s_ref[...] = st.T

    gn = gn_ref[...]

    def finish(i, carry):
        rows = pl.ds(pl.multiple_of(i * HG_SLAB, HG_SLAB), HG_SLAB)
        qs = qs_s[rows, :]
        inter = [_dot_nt(qs[j * c:(j + 1) * c], st_s[i * per_slab + j]) for j in range(per_slab)]
        o = o_s[rows, :] + jnp.concatenate(inter, axis=0)
        hg = hg_ref[rows, :]
        o = o * lax.rsqrt(jnp.mean(o * o, axis=-1, keepdims=True) + EPS)
        o_ref[rows, :] = (o * gn * (hg * jax.nn.sigmoid(hg))).astype(o_ref.dtype)
        return carry

    lax.fori_loop(0, seq // HG_SLAB, finish, 0)


def _hgrn_tri():
    i = np.arange(HG_SLAB)[:, None]
    j = np.arange(HG_SLAB)[None, :]
    cum = ((i // HG_CHUNK) == (j // HG_CHUNK)) & (j <= i)
    return jnp.asarray(cum[None], BF16)


def _hgrn_prompt(proj, col0, lb, g_norm, s0, batch, seq):
    assert seq % HG_SLAB == 0 and HG_DK == LANE and HG_DV == LANE
    c0 = col0 // LANE

    def col_specs(group):
        return [pl.BlockSpec((seq, LANE), lambda b, h, part=part: (b, c0 + group * HG_HEADS + h + part))
                for part in ((0, 1) if col0 % LANE else (0, 0))]

    vec_spec = pl.BlockSpec((1, LANE), lambda b, h: (0, h))
    st_spec = pl.BlockSpec((None, None, HG_DK, HG_DV), lambda b, h: (b, h, 0, 0))
    return pl.pallas_call(
        functools.partial(_hgrn_kernel, seq=seq, off=col0 % LANE),
        grid=(batch, HG_HEADS),
        in_specs=col_specs(0) + col_specs(1) + col_specs(2) + col_specs(3) + [
            vec_spec, vec_spec, st_spec, pl.BlockSpec((1, HG_SLAB, HG_SLAB), lambda b, h: (0, 0, 0))],
        out_specs=[pl.BlockSpec((seq, LANE), lambda b, h: (b, h)), st_spec],
        out_shape=[jax.ShapeDtypeStruct((batch * seq, HG_V_W), BF16),
                   jax.ShapeDtypeStruct((batch, HG_HEADS, HG_DK, HG_DV), F32)],
        scratch_shapes=[pltpu.VMEM((seq, LANE), BF16), pltpu.VMEM((seq // HG_CHUNK, LANE), F32),
                        pltpu.VMEM((seq, LANE), F32), pltpu.VMEM((seq // HG_CHUNK, HG_DV, HG_DK), F32),
                        pltpu.VMEM((seq // HG_CHUNK, HG_DV, HG_DK), BF16)],
        compiler_params=pltpu.CompilerParams(
            dimension_semantics=("arbitrary", "arbitrary"),
            vmem_limit_bytes=VMEM_LIMIT_BYTES),
        name="hgrn_prompt",
    )(*([proj] * 8), lb, g_norm, s0, _hgrn_tri())


SSD_TL = 256
SSD_GW = SSM_D_INNER // SSM_GROUPS
SSD_HPG = SSM_HEADS // SSM_GROUPS


def _split3(x):
    hi = x.astype(BF16)
    r1 = x - hi.astype(F32)
    mid = r1.astype(BF16)
    lo = (r1 - mid.astype(F32)).astype(BF16)
    return hi, mid, lo


def _sel_dot(sel, x):
    hi, mid, lo = _split3(x)
    d = functools.partial(jnp.dot, preferred_element_type=F32)
    return d(sel, hi) + d(sel, mid) + d(sel, lo)


def _dot_sel(x, sel):
    hi, mid, lo = _split3(x)
    d = functools.partial(jnp.dot, preferred_element_type=F32)
    return d(hi, sel) + d(mid, sel) + d(lo, sel)


def _causal_conv_silu(x, prev8, w, bias):
    row8 = lax.broadcasted_iota(jnp.int32, prev8.shape, 0)
    acc = bias
    for k in range(SSM_CONV - 1, 0, -1):
        r = pltpu.roll(x, k, 0)
        top = jnp.where(row8 < k, pltpu.roll(prev8, k, 0), r[:8])
        acc = acc + jnp.concatenate([top, r[8:]], axis=0) * w[SSM_CONV - 1 - k:SSM_CONV - k]
    acc = acc + x * w[SSM_CONV - 1:SSM_CONV]
    return acc * jax.nn.sigmoid(acc)


def _ssd_kernel(z_ref, x_ref, b_ref, c_ref, dt_ref, cx0_ref, cb0_ref, cc0_ref, s0_ref,
                wx_ref, wb_ref, wc_ref, bx_ref, bb_ref, bc_ref, dtb_ref, alog_ref, d_ref, gn_ref,
                tri_ref, hsel_ref, o_ref, s_ref,
                st_s, tx_s, tb_s, tc_s, xdt_s, xdd_s, ce_s, le_s, bm_s, cm_s, y_s):
    lt = pl.program_id(2)
    c = SSM_CHUNK
    tl = SSD_TL
    hp = SSM_HEAD_DIM

    @pl.when(lt == 0)
    def _():
        st_s[...] = s0_ref[...].T
        for tail, c0 in ((tx_s, cx0_ref), (tb_s, cb0_ref), (tc_s, cc0_ref)):
            tail[...] = jnp.zeros(tail.shape, F32)
            tail[8 - (SSM_CONV - 1):8, :] = c0[...]

    x_raw = x_ref[...]
    b_raw = b_ref[...]
    c_raw = c_ref[...]
    xs = _causal_conv_silu(x_raw, tx_s[...], wx_ref[...], bx_ref[...])
    bm_s[...] = _causal_conv_silu(b_raw, tb_s[...], wb_ref[...], bb_ref[...]).astype(bm_s.dtype)
    cm_s[...] = _causal_conv_silu(c_raw, tc_s[...], wc_ref[...], bc_ref[...]).astype(cm_s.dtype)
    tx_s[...] = x_raw[tl - 8:]
    tb_s[...] = b_raw[tl - 8:]
    tc_s[...] = c_raw[tl - 8:]

    dt = jax.nn.softplus(dt_ref[...] + dtb_ref[...])
    cum = _sel_dot(tri_ref[0], dt * -jnp.exp(alog_ref[...]))
    hsel = hsel_ref[...]
    dt_e = _dot_sel(dt, hsel)
    cum_e = _dot_sel(cum, hsel)
    last_e = _dot_sel(_sel_dot(tri_ref[1], cum), hsel)
    xdt = xs * dt_e
    xdt_s[...] = xdt.astype(xdt_s.dtype)
    xdd_s[...] = (xdt * jnp.exp(last_e - cum_e)).astype(xdd_s.dtype)
    ce_s[...] = cum_e
    le_s[...] = jnp.exp(last_e)

    trow = lax.broadcasted_iota(jnp.int32, (c, SSD_GW), 0)
    lane_g = lax.broadcasted_iota(jnp.int32, (c, SSD_GW), 1)
    diag = (lane_g % hp) == trow
    t2 = lax.broadcasted_iota(jnp.int32, (c, LANE), 0)
    l2 = lax.broadcasted_iota(jnp.int32, (c, LANE), 1)
    tril2 = (l2 % hp) <= t2
    r3 = lax.broadcasted_iota(jnp.int32, (2 * c, LANE), 0)
    l3 = lax.broadcasted_iota(jnp.int32, (2 * c, LANE), 1)
    blockdiag = (r3 // c) == (l3 // hp)

    for ci in range(tl // c):
        rows = pl.ds(ci * c, c)
        cm = cm_s[rows, :]
        bm = bm_s[rows, :]
        cum_c = ce_s[rows, :]
        cum_row = jnp.sum(jnp.where(diag, cum_c, 0.0), axis=0, keepdims=True)
        cb2 = _dot_nt(cm, jnp.concatenate([bm, bm], axis=0))
        xdt_c = xdt_s[rows, :]
        st = st_s[...]
        y_inter = jnp.dot(cm, st.astype(cm.dtype), preferred_element_type=F32) * jnp.exp(cum_c)
        for j in range(SSD_GW // LANE):
            cols = slice(j * LANE, (j + 1) * LANE)
            seg = cum_c[:, cols] - cum_row[:, cols]
            w = (cb2 * jnp.where(tril2, jnp.exp(seg), 0.0)).astype(xdt_c.dtype)
            xj = xdt_c[:, cols]
            xbd = jnp.where(blockdiag, jnp.concatenate([xj, xj], axis=0), jnp.zeros_like(xj[:1, :1]))
            y_s[rows, cols] = y_inter[:, cols] + jnp.dot(w, xbd, preferred_element_type=F32)
        upd = lax.dot_general(bm, xdd_s[rows, :], (((0,), (0,)), ((), ())), preferred_element_type=F32)
        st_s[...] = st * le_s[pl.ds(ci * c, 1), :] + upd

    z = z_ref[...]
    y = (y_s[...] + d_ref[...] * xs) * (z * jax.nn.sigmoid(z))
    y = y * lax.rsqrt(jnp.mean(y * y, axis=-1, keepdims=True) + EPS)
    o_ref[...] = (y * gn_ref[...]).astype(o_ref.dtype)

    @pl.when(lt == pl.num_programs(2) - 1)
    def _():
        s_ref[...] = st_s[...].T


def _ssd_consts():
    i = np.arange(SSD_TL)[:, None]
    j = np.arange(SSD_TL)[None, :]
    cum = ((i // SSM_CHUNK) == (j // SSM_CHUNK)) & (j <= i)
    last = j == (i // SSM_CHUNK) * SSM_CHUNK + SSM_CHUNK - 1
    tri = jnp.asarray(np.stack([cum, last]), BF16)
    h = np.arange(SSM_HEADS)[None, :, None]
    g = np.arange(SSM_GROUPS)[:, None, None]
    lane = np.arange(SSD_GW)[None, None, :]
    hsel = jnp.asarray(h == g * SSD_HPG + lane // SSM_HEAD_DIM, BF16)
    return tri, hsel


def _ssd_prompt(proj, conv0, s0, conv_w, conv_b, dt_bias, a_log, d_skip, norm_g, batch, seq):
    assert seq % SSD_TL == 0 and SSM_STATE == LANE and SSM_HEADS == LANE and SSM_CHUNK == SSM_HEAD_DIM
    nl = seq // SSD_TL
    gpl = SSD_GW // LANE
    xb = SSM_D_INNER // SSD_GW
    bb = 2 * SSM_D_INNER // LANE
    cb = bb + SSM_GROUPS
    db = cb + SSM_GROUPS
    tri, hsel = _ssd_consts()
    d_e = jnp.repeat(d_skip.astype(F32), SSM_HEAD_DIM)[None]
    row = lambda a: a.astype(F32)[None]

    def rows_spec(width, col_fn):
        return pl.BlockSpec((SSD_TL, width), lambda b, g, t: (b * nl + t, col_fn(g)))

    def conv0_spec(width, col_fn):
        return pl.BlockSpec((None, SSM_CONV - 1, width), lambda b, g, t: (b, 0, col_fn(g)))

    def vec_spec(rows_, width, col_fn):
        return pl.BlockSpec((rows_, width), lambda b, g, t: (0, col_fn(g)))

    xcol = lambda g: g
    bcol = lambda g: SSM_D_INNER // LANE + g
    ccol = lambda g: SSM_D_INNER // LANE + SSM_GROUPS + g
    zero = lambda g: 0
    st_spec = pl.BlockSpec((None, SSD_GW, SSM_STATE), lambda b, g, t: (b, g, 0))
    return pl.pallas_call(
        _ssd_kernel,
        grid=(batch, SSM_GROUPS, nl),
        in_specs=[rows_spec(SSD_GW, xcol), rows_spec(SSD_GW, lambda g: xb + g),
                  rows_spec(LANE, lambda g: bb + g), rows_spec(LANE, lambda g: cb + g),
                  rows_spec(LANE, lambda g: db),
                  conv0_spec(SSD_GW, xcol), conv0_spec(LANE, bcol), conv0_spec(LANE, ccol), st_spec,
                  vec_spec(SSM_CONV, SSD_GW, xcol), vec_spec(SSM_CONV, LANE, bcol), vec_spec(SSM_CONV, LANE, ccol),
                  vec_spec(1, SSD_GW, xcol), vec_spec(1, LANE, bcol), vec_spec(1, LANE, ccol),
                  vec_spec(1, LANE, zero), vec_spec(1, LANE, zero),
                  vec_spec(1, SSD_GW, xcol), vec_spec(1, SSD_GW, xcol),
                  pl.BlockSpec((2, SSD_TL, SSD_TL), lambda b, g, t: (0, 0, 0)),
                  pl.BlockSpec((None, SSM_HEADS, SSD_GW), lambda b, g, t: (g, 0, 0))],
        out_specs=[rows_spec(SSD_GW, xcol), st_spec],
        out_shape=[jax.ShapeDtypeStruct((batch * seq, SSM_D_INNER), BF16),
                   jax.ShapeDtypeStruct((batch, SSM_HEADS * SSM_HEAD_DIM, SSM_STATE), F32)],
        scratch_shapes=[pltpu.VMEM((SSM_STATE, SSD_GW), F32),
                        pltpu.VMEM((8, SSD_GW), F32), pltpu.VMEM((8, LANE), F32), pltpu.VMEM((8, LANE), F32),
                        pltpu.VMEM((SSD_TL, SSD_GW), BF16), pltpu.VMEM((SSD_TL, SSD_GW), BF16),
                        pltpu.VMEM((SSD_TL, SSD_GW), F32), pltpu.VMEM((SSD_TL, SSD_GW), F32),
                        pltpu.VMEM((SSD_TL, LANE), BF16), pltpu.VMEM((SSD_TL, LANE), BF16),
                        pltpu.VMEM((SSD_TL, SSD_GW), F32)],
        compiler_params=pltpu.CompilerParams(
            dimension_semantics=("arbitrary", "arbitrary", "arbitrary"),
            vmem_limit_bytes=VMEM_LIMIT_BYTES),
        name="ssd_prompt",
    )(proj, proj, proj, proj, proj, conv0, conv0, conv0, s0,
      conv_w, conv_w, conv_w, row(conv_b), row(conv_b), row(conv_b), row(dt_bias), row(a_log),
      d_e, row(norm_g), tri, hsel)


def split_cols(a, sizes):
    out, o = [], 0
    for s in sizes:
        out.append(a[..., o:o + s])
        o += s
    return out


def to_chunks(a, c):
    b, l = a.shape[:2]
    pad = (-l) % c
    a = jnp.pad(a.astype(F32), [(0, 0), (0, pad)] + [(0, 0)] * (a.ndim - 2))
    return jnp.moveaxis(a.reshape((b, (l + pad) // c, c) + a.shape[2:]), 1, 0)


def from_chunks(a, l):
    a = jnp.moveaxis(a, 0, 1)
    return a.reshape((a.shape[0], a.shape[1] * a.shape[2]) + a.shape[3:])[:, :l]


def gla_chunked(q, k, v, logf, s0):
    b, l = q.shape[:2]
    c = min(HG_CHUNK, l)
    mid = (c - 1) // 2
    tril = jnp.tril(jnp.ones((c, c), bool))

    def step(s, inp):
        qc, kc, vc, gc = inp
        bc = jnp.cumsum(gc, axis=1)
        bm = bc[:, mid:mid + 1]
        a = jnp.einsum('bthd,bshd->bhts', qc * jnp.exp(bc - bm), kc * jnp.exp(bm - bc))
        a = jnp.where(tril, a, 0.0)
        o = jnp.einsum('bhts,bshv->bthv', a, vc) + jnp.einsum('bthd,bhdv->bthv', qc * jnp.exp(bc), s)
        bl = bc[:, -1]
        s = jnp.exp(bl)[..., None] * s + jnp.einsum('bshd,bshv->bhdv', kc * jnp.exp(bl[:, None] - bc), vc)
        return s, o

    s, o = lax.scan(step, s0.astype(F32), (to_chunks(q, c), to_chunks(k, c), to_chunks(v, c), to_chunks(logf, c)))
    return from_chunks(o, l), s


def hgrn2(hq, hf, hi, hg, lb, s0, g_norm):
    b, l, _ = hq.shape
    q = jax.nn.silu(hq.astype(F32)).reshape(b, l, HG_HEADS, HG_DK)
    f = (lb + (1.0 - lb) * jax.nn.sigmoid(hf.astype(F32))).reshape(b, l, HG_HEADS, HG_DK)
    v = hi.astype(F32).reshape(b, l, HG_HEADS, HG_DV)
    o, s = gla_chunked(q, 1.0 - f, v, jnp.log(f), s0)
    o = o * lax.rsqrt(jnp.mean(o * o, axis=-1, keepdims=True) + EPS)
    o = o.reshape(b, l, HG_V_W) * g_norm.astype(F32) * jax.nn.silu(hg.astype(F32))
    return o, s


def nsa_prompt_branch(proj, b, l, w1, w2, pe):
    cmp = _compress_prompt(proj, w1, w2, pe, b, l)
    g = proj[:, AB_GATE_COL:AB_GATE_COL + NSA_GATE_W]
    gates = jax.nn.sigmoid(g).reshape(b * l, NSA_KV_HEADS, 3 * NSA_GROUP)
    gates = jnp.pad(gates, ((0, 0), (0, 0), (0, LANE - 3 * NSA_GROUP))).reshape(b * l, NSA_KV_HEADS * LANE)
    return _nsa_prompt(proj, cmp[0], cmp[1], gates, b, l)


AB_SIZES = (NSA_Q, NSA_KV_W, NSA_GATE_W, HG_QF_W, HG_QF_W, HG_V_W, HG_V_W)
AB_IN = sum(AB_SIZES)
AB_GATE_COL = NSA_Q + NSA_KV_W
AB_HG_COL = AB_GATE_COL + NSA_GATE_W


def ab_mixer_prompt(x, h, b, l, hg_state, lb, w_in, w1, w2, pe, hg_g, w_out):
    proj = _mm(h, w_in)
    o_nsa = nsa_prompt_branch(proj, b, l, w1, w2, pe)
    o_hg, s_new = _hgrn_prompt(proj, AB_HG_COL, lb[None], hg_g[None], hg_state, b, l)
    x = _mm([o_nsa, o_hg], w_out, res=x)
    slot_w = NSA_KV_HEADS * NSA_HEAD_DIM
    rows = proj[:, NSA_Q:NSA_Q + 4 * slot_w].reshape(b, l, 4, NSA_KV_HEADS, NSA_HEAD_DIM)
    wb = min(WINDOW, l)
    buf = proj.reshape(b, l, AB_IN)[:, l - wb:, NSA_Q + 4 * slot_w:NSA_Q + 6 * slot_w]
    return x, rows, buf.reshape(b, wb, 2, NSA_KV_HEADS, NSA_HEAD_DIM), s_new


def nsa_decode_branch(q, kv, g, cache, page0, page_table, swa_buf, w1, w2, pe):
    b, ls = q.shape[:2]
    assert page_table.shape[1] * PAGE_SIZE == PAST_LEN
    hd = NSA_HEAD_DIM
    row_pad = DEC_QROWS - NSA_GROUP
    q_pad = q.reshape(b, ls, NSA_KV_HEADS, NSA_GROUP, hd).transpose(0, 2, 1, 3, 4)
    q_pad = jnp.pad(q_pad, ((0, 0), (0, 0), (0, 0), (0, row_pad), (0, 0))).reshape(b, NSA_KV_HEADS, ls * DEC_QROWS, hd)
    gates = jax.nn.sigmoid(g).reshape(b, ls, NSA_KV_HEADS, NSA_GROUP, 3).transpose(0, 2, 1, 3, 4)
    gates = jnp.pad(gates, ((0, 0), (0, 0), (0, 0), (0, row_pad), (0, LANE - 3)))
    gates = gates.reshape(b, NSA_KV_HEADS, ls * DEC_QROWS, LANE)
    new_rows = jnp.pad(kv[:, :, 2:6].transpose(0, 3, 2, 1, 4), ((0, 0), (0, 0), (0, 0), (0, 8 - ls), (0, 0)))
    cmp = _compress_paged(cache, page0, page_table, w1, w2, pe)
    o_cmp, idx = _decode_select(q_pad, cmp, ls)
    o = _decode_attend(q_pad, o_cmp, gates, new_rows, swa_buf, cache, page0, page_table,
                       idx[:, :, :ls, :N_SELECT - 1].reshape(-1), ls)
    o = o.reshape(b, NSA_KV_HEADS, ls, DEC_QROWS, hd)[:, :, :, :NSA_GROUP]
    return o.transpose(0, 2, 1, 3, 4).reshape(b * ls, NSA_Q)


def ab_mixer_sample(x, h, b, l, cache, page0, page_table, swa_buf, hg_state, lb, w_in, w1, w2, pe, hg_g, w_out):
    q, kv, g, hq, hf, hi, hgate = split_cols(_mm(h, w_in).reshape(b, l, AB_IN), AB_SIZES)
    kv = kv.reshape(b, l, 6, NSA_KV_HEADS, NSA_HEAD_DIM)
    o_nsa = nsa_decode_branch(q, kv, g, cache, page0, page_table, swa_buf, w1, w2, pe)
    new_buf = jnp.concatenate([swa_buf.astype(kv.dtype), kv[:, :, 4:]], axis=1)[:, l:]
    o_hg, s_new = hgrn2(hq, hf, hi, hgate, lb, hg_state, hg_g)
    mix = jnp.concatenate([o_nsa, o_hg.reshape(b * l, HG_V_W)], axis=-1).astype(BF16)
    return _mm(mix, w_out, res=x), kv[:, :, :4], new_buf, s_new


def ssd_chunked(x, dt, a, bm, cm, s0):
    b, l, nh, p = x.shape
    g, n = bm.shape[2], bm.shape[3]
    r = nh // g
    c = min(SSM_CHUNK, l)
    tril = jnp.tril(jnp.ones((c, c), bool))

    def step(s, inp):
        xc, dtc, bc, cc = inp
        cum = jnp.cumsum(dtc * a, axis=1)
        seg = cum[:, :, None, :] - cum[:, None, :, :]
        lm = jnp.exp(jnp.where(tril[None, :, :, None], seg, -jnp.inf)).reshape(b, c, c, g, r)
        xdt = (xc * dtc[..., None]).reshape(b, c, g, r, p)
        cb = jnp.einsum('btgn,bsgn->btsg', cc, bc)
        sg = s.reshape(b, g, r, p, n)
        y = jnp.einsum('btsg,btsgr,bsgrp->btgrp', cb, lm, xdt)
        y = y + jnp.einsum('btgn,bgrpn->btgrp', cc, sg) * jnp.exp(cum).reshape(b, c, g, r)[..., None]
        dec = jnp.exp(cum[:, -1:] - cum).reshape(b, c, g, r)
        sg = jnp.exp(cum[:, -1]).reshape(b, g, r)[..., None, None] * sg + jnp.einsum('bsgn,bsgrp->bgrpn', bc, xdt * dec[..., None])
        return sg.reshape(b, nh, p, n), y.reshape(b, c, nh, p)

    s, y = lax.scan(step, s0.astype(F32), (to_chunks(x, c), to_chunks(dt, c), to_chunks(bm, c), to_chunks(cm, c)))
    return from_chunks(y, l), s


def mamba_prompt(x, h, b, l, w_in, conv_w, conv_b, dt_bias, a_log, d_skip, norm_g, w_out):
    proj = _mm(h, w_in)
    y, s = _ssd_prompt(proj, jnp.zeros((b, SSM_CONV - 1, SSM_CONV_DIM), F32),
                       jnp.zeros((b, SSM_HEADS * SSM_HEAD_DIM, SSM_STATE), F32),
                       conv_w, conv_b, dt_bias, a_log, d_skip, norm_g, b, l)
    assert l >= SSM_CONV - 1
    new_conv = proj.reshape(b, l, -1)[:, l - (SSM_CONV - 1):, SSM_D_INNER:SSM_D_INNER + SSM_CONV_DIM]
    return _mm(y, w_out, res=x), new_conv, s.reshape(b, SSM_HEADS, SSM_HEAD_DIM, SSM_STATE)


def mamba_sample(x, h, b, l, conv_state, ssm_state, w_in, conv_w, conv_b, dt_bias, a_log, d_skip, norm_g, w_out):
    z, xbc, dt = split_cols(_mm(h, w_in).reshape(b, l, -1), (SSM_D_INNER, SSM_CONV_DIM, SSM_HEADS))
    xpad = jnp.concatenate([conv_state.astype(xbc.dtype), xbc], axis=1)
    acc = conv_b.astype(F32)
    for j in range(SSM_CONV):
        acc = acc + xpad[:, j:j + l].astype(F32) * conv_w[j].astype(F32)
    xbc = jax.nn.silu(acc)
    new_conv = xpad[:, xpad.shape[1] - (SSM_CONV - 1):]
    xs, bm, cm = split_cols(xbc, (SSM_D_INNER, SSM_GROUPS * SSM_STATE, SSM_GROUPS * SSM_STATE))
    xs = xs.reshape(b, l, SSM_HEADS, SSM_HEAD_DIM)
    bm = bm.reshape(b, l, SSM_GROUPS, SSM_STATE)
    cm = cm.reshape(b, l, SSM_GROUPS, SSM_STATE)
    dt = jax.nn.softplus(dt.astype(F32) + dt_bias.astype(F32))
    a = -jnp.exp(a_log.astype(F32))
    y, s = ssd_chunked(xs, dt, a, bm, cm, ssm_state)
    y = y + d_skip.astype(F32)[:, None] * xs
    y = (y.reshape(b, l, SSM_D_INNER) * jax.nn.silu(z.astype(F32))).reshape(b, l, SSM_GROUPS, SSM_D_INNER // SSM_GROUPS)
    y = (y * lax.rsqrt(jnp.mean(y * y, axis=-1, keepdims=True) + EPS)).reshape(b, l, SSM_D_INNER) * norm_g.astype(F32)
    return _mm(y.reshape(b * l, SSM_D_INNER).astype(BF16), w_out, res=x), new_conv, s


def ffn_ple(x, p, g_ffn, w_ffn_in, w_ffn_out, g_ple, w_ple_gate, w_ple_up):
    x = _mm(_mm_swiglu(_rms(x, g_ffn, BF16), w_ffn_in), w_ffn_out, res=x)
    return _mm_ple(_rms(x, g_ple, BF16), w_ple_gate, p, w_ple_up, x)


def kernel(x_prompt, x_sample, cache_nsa_kv, cache_swa_kv, state_hgrn, state_ssm, cache_conv, page_table, p_prompt, p_sample, norm_mix, norm_ffn, w_ab_in, w_cmp1, w_cmp2, cmp_pe, hg_lb_logits, hg_norm, w_ab_out, w_ssm_in, ssm_conv_w, ssm_conv_b, ssm_dt_bias, ssm_a_log, ssm_d, ssm_norm, w_ssm_out, w_ffn_in, w_ffn_out, w_ple_up, w_ple_gate, norm_ple, norm_final):
    depth = norm_mix.shape[0]
    bp, lp, d = x_prompt.shape
    bs, ls, _ = x_sample.shape
    n_pool = cache_nsa_kv.shape[1]
    cache_rows = cache_nsa_kv.reshape(cache_nsa_kv.shape[0] * n_pool, PAGE_SIZE, 4 * NSA_KV_HEADS, NSA_HEAD_DIM)
    lb_all = jnp.cumsum(jax.nn.softmax(hg_lb_logits.astype(F32), axis=0), axis=0)
    w_ffn_out_b = w_ffn_out.astype(BF16)
    w_ssm_out_b = w_ssm_out.astype(BF16)
    w_ab_in_b = w_ab_in.astype(BF16)
    xp = x_prompt.reshape(bp * lp, d)
    xs = x_sample.reshape(bs * ls, d)
    nsa_p, nsa_s, swa_p, swa_s, hg_p, hg_s, ssm_p, ssm_s, cv_p, cv_s = [], [], [], [], [], [], [], [], [], []
    for i in range(depth):
        hp = _rms(xp, norm_mix[i], BF16)
        hs = _rms(xs, norm_mix[i], BF16)
        if i % 2 == 0:
            a = i // 2
            wa = (lb_all[a], Stacked(w_ab_in_b, a), w_cmp1[a], w_cmp2[a], cmp_pe[a], hg_norm[a],
                  Stacked(w_ab_out, a))
            xp, r_p, b_p, s_p = ab_mixer_prompt(xp, hp, bp, lp, jnp.zeros((bp, HG_HEADS, HG_DK, HG_DV), F32), *wa)
            xs, r_s, b_s, s_s = ab_mixer_sample(xs, hs, bs, ls, cache_rows, a * n_pool, page_table,
                                                cache_swa_kv[a], state_hgrn[a], *wa)
            nsa_p.append(r_p)
            nsa_s.append(r_s)
            swa_p.append(b_p)
            swa_s.append(b_s)
            hg_p.append(s_p)
            hg_s.append(s_s)
        else:
            c = i // 2
            wc = (Stacked(w_ssm_in, c), ssm_conv_w[c], ssm_conv_b[c], ssm_dt_bias[c], ssm_a_log[c], ssm_d[c],
                  ssm_norm[c], Stacked(w_ssm_out_b, c))
            xp, c_p, t_p = mamba_prompt(xp, hp, bp, lp, *wc)
            xs, c_s, t_s = mamba_sample(xs, hs, bs, ls, cache_conv[c], state_ssm[c], *wc)
            cv_p.append(c_p)
            cv_s.append(c_s)
            ssm_p.append(t_p)
            ssm_s.append(t_s)
        wf = (norm_ffn[i], Stacked(w_ffn_in, i), Stacked(w_ffn_out_b, i), norm_ple[i],
              Stacked(w_ple_gate, i), Stacked(w_ple_up, i))
        xp = ffn_ple(xp, p_prompt[i].reshape(bp * lp, PLE_DIM).astype(BF16), *wf)
        xs = ffn_ple(xs, p_sample[i].reshape(bs * ls, PLE_DIM).astype(BF16), *wf)
    y_prompt = _rms(xp, norm_final, F32).reshape(bp, lp, d)
    y_sample = _rms(xs, norm_final, F32).reshape(bs, ls, d)
    return (y_prompt, y_sample, jnp.stack(nsa_p), jnp.stack(nsa_s), jnp.stack(swa_p), jnp.stack(swa_s),
            jnp.stack(hg_p), jnp.stack(hg_s), jnp.stack(ssm_p), jnp.stack(ssm_s), jnp.stack(cv_p), jnp.stack(cv_s))
```

```python
import functools
from typing import NamedTuple

import jax
import jax.numpy as jnp
import numpy as np
from jax import lax
from jax.experimental import pallas as pl
from jax.experimental.pallas import tpu as pltpu

D_MODEL = 4096
PAST_LEN = 16384
PAGE_SIZE = 128
PLE_DIM = 256
NSA_HEADS = 16
NSA_KV_HEADS = 4
NSA_GROUP = NSA_HEADS // NSA_KV_HEADS
NSA_HEAD_DIM = 128
NSA_Q = NSA_HEADS * NSA_HEAD_DIM
NSA_KV_W = 6 * NSA_KV_HEADS * NSA_HEAD_DIM
NSA_GATE_W = 3 * NSA_HEADS
CMP_BLOCK = 32
CMP_STRIDE = 16
CMP_HIDDEN = 256
SLC_BLOCK = 64
N_SELECT = 16
WINDOW = 512
FORCE_SCORE = 1e4
HG_HEADS = 16
HG_DK = 128
HG_DV = (D_MODEL // 2) // HG_HEADS
HG_QF_W = HG_HEADS * HG_DK
HG_V_W = HG_HEADS * HG_DV
HG_CHUNK = 32
SSM_D_INNER = 2 * D_MODEL
SSM_HEAD_DIM = 64
SSM_HEADS = SSM_D_INNER // SSM_HEAD_DIM
SSM_GROUPS = 8
SSM_STATE = 128
SSM_CONV = 4
SSM_CONV_DIM = SSM_D_INNER + 2 * SSM_GROUPS * SSM_STATE
SSM_CHUNK = 64
EPS = 1e-6
NEG = -1e30

LANE = 128
VMEM_LIMIT_BYTES = 56 * 1024 * 1024
BF16 = jnp.bfloat16
F32 = jnp.float32


MAX_ROW_TILE = 1024
NORM_ROW_TILE = 512


def _rms_kernel(x_ref, g_ref, o_ref):
    x = x_ref[...]
    y = x * lax.rsqrt(jnp.mean(x * x, axis=-1, keepdims=True) + EPS)
    o_ref[...] = (y * g_ref[...]).astype(o_ref.dtype)


def _rms(x, g, out_dtype):
    m, d = x.shape
    tm = min(m, NORM_ROW_TILE)
    assert m % tm == 0
    return pl.pallas_call(
        _rms_kernel,
        grid=(m // tm,),
        in_specs=[pl.BlockSpec((tm, d), lambda i: (i, 0)), pl.BlockSpec((1, d), lambda i: (0, 0))],
        out_specs=pl.BlockSpec((tm, d), lambda i: (i, 0)),
        out_shape=jax.ShapeDtypeStruct((m, d), out_dtype),
        compiler_params=pltpu.CompilerParams(dimension_semantics=("arbitrary",),
                                             vmem_limit_bytes=VMEM_LIMIT_BYTES),
        name="rms",
    )(x, g.astype(F32)[None])


def _mm_kernel(*refs, n_lhs, residual):
    x_refs = refs[:n_lhs]
    w_ref = refs[n_lhs]
    o_ref = refs[-1]
    k = pl.program_id(2)

    def first(x_ref):
        acc = jnp.dot(x_ref[...], w_ref[...].astype(BF16), preferred_element_type=F32)
        o_ref[...] = acc + refs[n_lhs + 1][...] if residual else acc

    def later(x_ref):
        o_ref[...] += jnp.dot(x_ref[...], w_ref[...].astype(BF16), preferred_element_type=F32)

    pl.when(k == 0)(functools.partial(first, x_refs[0]))
    if n_lhs == 1:
        pl.when(k > 0)(functools.partial(later, x_refs[0]))
    else:
        for p in range(1, n_lhs):
            pl.when(k == p)(functools.partial(later, x_refs[p]))


def _w_tile_bytes(w):
    return 2 * w.dtype.itemsize + (2 if w.dtype != BF16 else 0)


def _k_tile(k, x_bytes_per_k, w_bytes_per_k, budget):
    units = k // LANE
    for parts in range(1, units + 1):
        if units % parts == 0 and (x_bytes_per_k + w_bytes_per_k) * (units // parts) * LANE <= budget:
            return (units // parts) * LANE
    raise ValueError(f"no K tile for {k}")


class Stacked(NamedTuple):
    w: jax.Array
    layer: int


def _unstack(w):
    return (w.w, w.layer) if isinstance(w, Stacked) else (w, None)


def _w_spec(block, index_map, layer):
    if layer is None:
        return pl.BlockSpec(block, index_map)
    return pl.BlockSpec((None,) + block, lambda *g: (layer,) + index_map(*g))


def _mm(xs, w, res=None):
    xs = list(xs) if isinstance(xs, (list, tuple)) else [xs]
    m = xs[0].shape[0]
    w, layer = _unstack(w)
    k, n = w.shape[-2:]
    tm = min(m, MAX_ROW_TILE)
    n_out_bufs = 4 if res is not None else 2
    widest = 1024 if w.dtype == BF16 else 512
    for tn in (min(n, widest), min(n, widest // 2)):
        budget = VMEM_LIMIT_BYTES - n_out_bufs * tm * tn * 4 - (4 << 20)
        x_per_k = 2 * 2 * tm * len(xs)
        w_per_k = _w_tile_bytes(w) * tn
        tk = _k_tile(k, x_per_k, w_per_k, budget) if len(xs) == 1 else k // len(xs)
        if tk == k or tk >= 2048:
            break
    if len(xs) == 1:
        x_specs = [pl.BlockSpec((tm, tk), lambda i, j, kk: (i, kk))]
    else:
        assert all(x.shape[1] == tk for x in xs) and (x_per_k + w_per_k) * tk <= budget
        x_specs = [pl.BlockSpec((tm, tk), lambda i, j, kk: (i, 0)) for _ in xs]
    assert m % tm == 0 and k % tk == 0 and tk % LANE == 0
    tile = pl.BlockSpec((tm, tn), lambda i, j, kk: (i, j))
    return pl.pallas_call(
        functools.partial(_mm_kernel, n_lhs=len(xs), residual=res is not None),
        grid=(m // tm, pl.cdiv(n, tn), k // tk),
        in_specs=x_specs + [_w_spec((tk, tn), lambda i, j, kk: (kk, j), layer)] + ([tile] if res is not None else []),
        out_specs=tile,
        out_shape=jax.ShapeDtypeStruct((m, n), F32),
        compiler_params=pltpu.CompilerParams(
            dimension_semantics=("arbitrary", "arbitrary", "arbitrary"),
            vmem_limit_bytes=VMEM_LIMIT_BYTES),
        name="mm",
    )(*xs, w, *([res] if res is not None else []))


FFN_TILE = 256


def _swiglu_kernel(x_ref, wg_ref, wu_ref, o_ref):
    x = x_ref[...]
    gate = jnp.dot(x, wg_ref[...].astype(BF16), preferred_element_type=F32)
    up = jnp.dot(x, wu_ref[...].astype(BF16), preferred_element_type=F32)
    o_ref[...] = (gate * jax.nn.sigmoid(gate) * up).astype(o_ref.dtype)


def _mm_swiglu(x, w):
    m, k = x.shape
    w, layer = w
    hidden = w.shape[2] // 2
    tm = min(m, MAX_ROW_TILE)
    nj = hidden // FFN_TILE
    assert m % tm == 0 and hidden % FFN_TILE == 0 and w.shape[1] == k
    return pl.pallas_call(
        _swiglu_kernel,
        grid=(m // tm, nj),
        in_specs=[pl.BlockSpec((tm, k), lambda i, j: (i, 0)),
                  _w_spec((k, FFN_TILE), lambda i, j: (0, j), layer),
                  _w_spec((k, FFN_TILE), lambda i, j: (0, nj + j), layer)],
        out_specs=pl.BlockSpec((tm, FFN_TILE), lambda i, j: (i, j)),
        out_shape=jax.ShapeDtypeStruct((m, hidden), BF16),
        compiler_params=pltpu.CompilerParams(dimension_semantics=("arbitrary", "arbitrary"),
                                             vmem_limit_bytes=VMEM_LIMIT_BYTES),
        name="mm_swiglu",
    )(x, w, w)


PLE_COL_TILE = 512


def _ple_kernel(t_ref, wg_ref, p_ref, wu_ref, res_ref, o_ref):
    gate = jnp.dot(t_ref[...], wg_ref[...].astype(BF16), preferred_element_type=F32)
    up = jnp.dot(p_ref[...], wu_ref[...].astype(BF16), preferred_element_type=F32)
    o_ref[...] = res_ref[...] + up * jax.nn.sigmoid(gate)


def _mm_ple(t, w_gate, p, w_up, res):
    m, k = t.shape
    w_gate, layer = w_gate
    w_up, layer_up = w_up
    assert layer == layer_up
    n = w_gate.shape[2]
    kp = p.shape[1]
    tm = min(m, MAX_ROW_TILE)
    tn = PLE_COL_TILE
    assert m % tm == 0 and n % tn == 0
    tile = pl.BlockSpec((tm, tn), lambda i, j: (i, j))
    return pl.pallas_call(
        _ple_kernel,
        grid=(m // tm, n // tn),
        in_specs=[pl.BlockSpec((tm, k), lambda i, j: (i, 0)), _w_spec((k, tn), lambda i, j: (0, j), layer),
                  pl.BlockSpec((tm, kp), lambda i, j: (i, 0)), _w_spec((kp, tn), lambda i, j: (0, j), layer), tile],
        out_specs=tile,
        out_shape=jax.ShapeDtypeStruct((m, n), F32),
        compiler_params=pltpu.CompilerParams(dimension_semantics=("arbitrary", "arbitrary"),
                                             vmem_limit_bytes=VMEM_LIMIT_BYTES),
        name="mm_ple",
    )(t, w_gate, p, w_up, res)


NSA_TQ = 256
NSA_KEY_BUCKET = 512


def _dot_nt(a, b):
    return lax.dot_general(a, b, (((1,), (1,)), ((), ())), preferred_element_type=F32)


def _group_attend(q4, k, v, mask, scale, tq):
    s4 = _dot_nt(q4, k) * scale
    es, dens = [], []
    for r in range(NSA_GROUP):
        s = jnp.where(mask, s4[r * tq:(r + 1) * tq], NEG)
        e = jnp.exp(s - jnp.max(s, axis=-1, keepdims=True))
        dens.append(jnp.sum(e, axis=-1, keepdims=True))
        es.append(e.astype(BF16))
    o4 = jnp.dot(jnp.concatenate(es, axis=0), v, preferred_element_type=F32)
    return [o4[r * tq:(r + 1) * tq] / dens[r] for r in range(NSA_GROUP)]


def _nsa_prompt_kernel(q_ref, kc_ref, vc_ref, ks_ref, vs_ref, kw_ref, vw_ref, gate_ref, cover_ref, expand_ref,
                       o_ref, slc_s, *, seq):
    tq = NSA_TQ
    hd = NSA_HEAD_DIM
    n_cmp = seq // CMP_STRIDE - CMP_BLOCK // CMP_STRIDE + 1
    n_slc = seq // SLC_BLOCK
    span = WINDOW + tq
    scale = hd ** -0.5
    q0 = pl.program_id(2) * tq
    pos = q0 + lax.broadcasted_iota(jnp.int32, (tq, 1), 0)
    lane = lax.broadcasted_iota(jnp.int32, (tq, LANE), 1)
    q4 = jnp.concatenate([q_ref[:, r * hd:(r + 1) * hd].astype(BF16) for r in range(NSA_GROUP)], axis=0)

    cmp_ok = (lane * CMP_STRIDE + (CMP_BLOCK - 1) <= pos) & (lane < n_cmp)
    s4 = _dot_nt(q4, kc_ref[...]) * scale
    psum = jnp.zeros((tq, LANE), F32)
    ps = []
    for r in range(NSA_GROUP):
        s = jnp.where(cmp_ok, s4[r * tq:(r + 1) * tq], NEG)
        e = jnp.where(cmp_ok, jnp.exp(s - jnp.max(s, axis=-1, keepdims=True)), 0.0)
        den = jnp.sum(e, axis=-1, keepdims=True)
        p = e / jnp.where(den > 0.0, den, 1.0)
        psum = psum + p
        ps.append(p.astype(BF16))
    o_cmp4 = jnp.dot(jnp.concatenate(ps, axis=0), vc_ref[...], preferred_element_type=F32)

    score = jnp.dot(psum, cover_ref[...], preferred_element_type=F32, precision=lax.Precision.HIGHEST)
    cur = pos // SLC_BLOCK
    forced = (lane == 0) | (lane == cur) | (lane == cur - 1)
    causal = lane * SLC_BLOCK <= pos
    score = jnp.where(forced, FORCE_SCORE, jnp.where(causal, score, -1.0))
    score = jnp.where(lane < n_slc, score, -2.0)
    rank = jnp.zeros((tq, LANE), jnp.int32)
    for i in range(n_slc):
        ci = score[:, i:i + 1]
        beats = (ci > score) | ((ci == score) & (lane > i))
        rank = rank + beats.astype(jnp.int32)
    sel = ((rank < min(N_SELECT, n_slc)) & (lane < n_slc)).astype(BF16)

    def selected(nk):
        sel_keys = jnp.dot(sel, expand_ref[:, :nk], preferred_element_type=F32)
        kpos = lax.broadcasted_iota(jnp.int32, (tq, nk), 1)
        outs = _group_attend(q4, ks_ref[:nk, :].astype(BF16), vs_ref[:nk, :].astype(BF16),
                             (sel_keys > 0.5) & (kpos <= pos), scale, tq)
        for r in range(NSA_GROUP):
            slc_s[r] = outs[r]

    n_bucket = (q0 + tq + NSA_KEY_BUCKET - 1) // NSA_KEY_BUCKET
    for nb in range(1, seq // NSA_KEY_BUCKET + 1):
        pl.when(n_bucket == nb)(functools.partial(selected, nb * NSA_KEY_BUCKET))

    w0 = pl.multiple_of(jnp.maximum(q0 - WINDOW, 0), tq)
    wpos = w0 + lax.broadcasted_iota(jnp.int32, (tq, span), 1)
    o_swa = _group_attend(q4, kw_ref[pl.ds(w0, span), :].astype(BF16), vw_ref[pl.ds(w0, span), :].astype(BF16),
                          (wpos <= pos) & (pos - wpos < WINDOW), scale, tq)

    for r in range(NSA_GROUP):
        g = gate_ref[:, 3 * r:3 * r + 3]
        o = g[:, 0:1] * o_cmp4[r * tq:(r + 1) * tq] + g[:, 1:2] * slc_s[r] + g[:, 2:3] * o_swa[r]
        o_ref[:, r * hd:(r + 1) * hd] = o.astype(o_ref.dtype)


def _nsa_prompt(proj, kc, vc, gates, batch, seq):
    assert seq % NSA_TQ == 0 and seq // SLC_BLOCK <= LANE and seq // CMP_STRIDE <= LANE + 1
    assert seq % NSA_KEY_BUCKET == 0 and seq >= WINDOW + NSA_TQ and WINDOW % NSA_TQ == 0
    nt = seq // NSA_TQ
    n_cmp = seq // CMP_STRIDE - CMP_BLOCK // CMP_STRIDE + 1
    n_slc = seq // SLC_BLOCK
    c0 = np.arange(LANE)[:, None] * CMP_STRIDE
    s0 = np.arange(LANE)[None, :] * SLC_BLOCK
    cover = np.clip(np.minimum(c0 + CMP_BLOCK, s0 + SLC_BLOCK) - np.maximum(c0, s0), 0, None) / CMP_BLOCK
    cover = cover * (np.arange(LANE)[:, None] < n_cmp) * (np.arange(LANE)[None, :] < n_slc)
    expand = (np.arange(seq)[None, :] // SLC_BLOCK == np.arange(LANE)[:, None])
    hd = NSA_HEAD_DIM
    gw = NSA_GROUP * hd

    def kv_spec(slot):
        return pl.BlockSpec((seq, hd), lambda b, g, t: (b, NSA_Q // hd + slot * NSA_KV_HEADS + g))

    cmp_spec = pl.BlockSpec((None, None, LANE, hd), lambda b, g, t: (b, g, 0, 0))
    return pl.pallas_call(
        functools.partial(_nsa_prompt_kernel, seq=seq),
        grid=(batch, NSA_KV_HEADS, nt),
        in_specs=[pl.BlockSpec((NSA_TQ, gw), lambda b, g, t: (b * nt + t, g)),
                  cmp_spec, cmp_spec, kv_spec(2), kv_spec(3), kv_spec(4), kv_spec(5),
                  pl.BlockSpec((NSA_TQ, LANE), lambda b, g, t: (b * nt + t, g)),
                  pl.BlockSpec((LANE, LANE), lambda b, g, t: (0, 0)),
                  pl.BlockSpec((LANE, seq), lambda b, g, t: (0, 0))],
        out_specs=pl.BlockSpec((NSA_TQ, gw), lambda b, g, t: (b * nt + t, g)),
        out_shape=jax.ShapeDtypeStruct((batch * seq, NSA_Q), BF16),
        scratch_shapes=[pltpu.VMEM((NSA_GROUP, NSA_TQ, hd), F32)],
        compiler_params=pltpu.CompilerParams(
            dimension_semantics=("arbitrary", "arbitrary", "arbitrary"),
            vmem_limit_bytes=VMEM_LIMIT_BYTES),
        name="nsa_prompt",
    )(proj, kc, vc, proj, proj, proj, proj, gates, jnp.asarray(cover, F32), jnp.asarray(expand, BF16))


def _compress_kernel(x_ref, w1_ref, w2_ref, pe_ref, o_ref, *, seq):
    n_sub = seq // CMP_STRIDE
    n_cmp = n_sub - CMP_BLOCK // CMP_STRIDE + 1
    acc = jnp.zeros((n_sub, 2 * CMP_HIDDEN), F32)
    for r in range(CMP_STRIDE):
        xr = x_ref[pl.ds(r, n_sub, stride=CMP_STRIDE), :].astype(BF16)
        acc = acc + jnp.dot(xr, w1_ref[r], preferred_element_type=F32)
    hid = acc[:, :CMP_HIDDEN] + pltpu.roll(acc[:, CMP_HIDDEN:], n_sub - 1, 0) + pe_ref[...]
    c = jnp.dot((hid * jax.nn.sigmoid(hid)).astype(BF16), w2_ref[...], preferred_element_type=F32)
    row = lax.broadcasted_iota(jnp.int32, c.shape, 0)
    o_ref[...] = jnp.where(row < n_cmp, c, 0.0).astype(o_ref.dtype)


def _compress_prompt(proj, w1, w2, pe, batch, seq):
    assert CMP_BLOCK == 2 * CMP_STRIDE and seq // CMP_STRIDE == LANE and NSA_HEAD_DIM == LANE
    hd = NSA_HEAD_DIM
    w1p = w1.reshape(2, 2, CMP_STRIDE, hd, CMP_HIDDEN).transpose(0, 2, 3, 1, 4)
    w1p = w1p.reshape(2, CMP_STRIDE, hd, 2 * CMP_HIDDEN).astype(BF16)
    pe_h = jnp.einsum('sc,sch->sh', pe.reshape(2, -1), w1)[:, None, :]
    return pl.pallas_call(
        functools.partial(_compress_kernel, seq=seq),
        grid=(batch, 2, NSA_KV_HEADS),
        in_specs=[pl.BlockSpec((seq, hd), lambda b, s, g: (b, NSA_Q // hd + s * NSA_KV_HEADS + g)),
                  pl.BlockSpec((None, CMP_STRIDE, hd, 2 * CMP_HIDDEN), lambda b, s, g: (s, 0, 0, 0)),
                  pl.BlockSpec((None, CMP_HIDDEN, hd), lambda b, s, g: (s, 0, 0)),
                  pl.BlockSpec((None, 1, CMP_HIDDEN), lambda b, s, g: (s, 0, 0))],
        out_specs=pl.BlockSpec((None, None, None, LANE, hd), lambda b, s, g: (s, b, g, 0, 0)),
        out_shape=jax.ShapeDtypeStruct((2, batch, NSA_KV_HEADS, LANE, hd), BF16),
        compiler_params=pltpu.CompilerParams(dimension_semantics=("arbitrary", "arbitrary", "arbitrary"),
                                             vmem_limit_bytes=VMEM_LIMIT_BYTES),
        name="compress_prompt",
    )(proj, w1p, w2.astype(BF16), pe_h)


CMP_PAGES = 16
DEC_QROWS = 8


def _compress_paged_kernel(pt_ref, *refs, n_cmp):
    pages = refs[:CMP_PAGES + 1]
    w1_ref, w2_ref, pe_ref, o_ref = refs[CMP_PAGES + 1:]
    per_row = 4 * NSA_KV_HEADS
    sub_pp = PAGE_SIZE // CMP_STRIDE
    own = CMP_PAGES * sub_pp
    tot = own + sub_pp
    first = pl.program_id(1) * own
    for slot in range(2):
        acc = jnp.zeros((NSA_KV_HEADS * tot, 2 * CMP_HIDDEN), F32)
        for r in range(CMP_STRIDE):
            xr = jnp.concatenate(
                [page[pl.ds(r * per_row + slot * NSA_KV_HEADS + g, sub_pp, stride=CMP_STRIDE * per_row), :]
                 for g in range(NSA_KV_HEADS) for page in pages], axis=0).astype(BF16)
            acc = acc + jnp.dot(xr, w1_ref[slot, r], preferred_element_type=F32)
        for g in range(NSA_KV_HEADS):
            blk = acc[g * tot:(g + 1) * tot]
            hid = (blk[:, :CMP_HIDDEN] + pltpu.roll(blk[:, CMP_HIDDEN:], tot - 1, 0))[:own] + pe_ref[slot]
            c = jnp.dot((hid * jax.nn.sigmoid(hid)).astype(BF16), w2_ref[slot], preferred_element_type=F32)
            sub = first + lax.broadcasted_iota(jnp.int32, c.shape, 0)
            o_ref[slot, g] = jnp.where(sub < n_cmp, c, 0.0).astype(o_ref.dtype)


def _compress_paged(cache, page0, page_table, w1, w2, pe):
    batch, n_pages = page_table.shape
    hd = NSA_HEAD_DIM
    sub_pp = PAGE_SIZE // CMP_STRIDE
    n_sub = n_pages * sub_pp
    assert n_pages % CMP_PAGES == 0 and PAGE_SIZE % CMP_STRIDE == 0 and CMP_BLOCK == 2 * CMP_STRIDE
    w1p = w1.reshape(2, 2, CMP_STRIDE, hd, CMP_HIDDEN).transpose(0, 2, 3, 1, 4)
    w1p = w1p.reshape(2, CMP_STRIDE, hd, 2 * CMP_HIDDEN).astype(BF16)
    pe_h = jnp.einsum('sc,sch->sh', pe.reshape(2, -1), w1)[:, None, :]

    def page_spec(k):
        return pl.BlockSpec((PAGE_SIZE * 4 * NSA_KV_HEADS, hd),
                            lambda b, s, pt: (page0 + pt[b, jnp.minimum(s * CMP_PAGES + k, n_pages - 1)], 0))

    full = lambda shape: pl.BlockSpec(shape, lambda b, s, pt: (0,) * len(shape))
    return pl.pallas_call(
        functools.partial(_compress_paged_kernel, n_cmp=n_sub - 1),
        grid_spec=pltpu.PrefetchScalarGridSpec(
            num_scalar_prefetch=1,
            grid=(batch, n_pages // CMP_PAGES),
            in_specs=[page_spec(k) for k in range(CMP_PAGES + 1)] + [
                full((2, CMP_STRIDE, hd, 2 * CMP_HIDDEN)), full((2, CMP_HIDDEN, hd)), full((2, 1, CMP_HIDDEN))],
            out_specs=pl.BlockSpec((2, None, NSA_KV_HEADS, CMP_PAGES * sub_pp, hd), lambda b, s, pt: (0, b, 0, s, 0))),
        out_shape=jax.ShapeDtypeStruct((2, batch, NSA_KV_HEADS, n_sub, hd), BF16),
        compiler_params=pltpu.CompilerParams(dimension_semantics=("arbitrary", "arbitrary"),
                                             vmem_limit_bytes=VMEM_LIMIT_BYTES),
        name="compress_paged",
    )(page_table, *([cache.reshape(-1, hd)] * (CMP_PAGES + 1)), w1p, w2.astype(BF16), pe_h)


def _decode_select_kernel(q_ref, kc_ref, vc_ref, cover_ref, rsum_ref, ocmp_ref, idx_ref, *, n_cmp, n_past_blocks):
    nq = q_ref.shape[0]
    n_sub = kc_ref.shape[0]
    scale = NSA_HEAD_DIM ** -0.5
    pos = PAST_LEN + lax.broadcasted_iota(jnp.int32, (nq, 1), 0) // DEC_QROWS
    n = lax.broadcasted_iota(jnp.int32, (nq, n_sub), 1)
    ok = (n * CMP_STRIDE + (CMP_BLOCK - 1) <= pos) & (n < n_cmp)
    s = jnp.where(ok, _dot_nt(q_ref[...].astype(BF16), kc_ref[...]) * scale, NEG)
    e = jnp.where(ok, jnp.exp(s - jnp.max(s, axis=-1, keepdims=True)), 0.0)
    den = jnp.sum(e, axis=-1, keepdims=True)
    p = e / jnp.where(den > 0.0, den, 1.0)
    ocmp_ref[...] = jnp.dot(p.astype(BF16), vc_ref[...], preferred_element_type=F32)

    score = _dot_sel(_sel_dot(rsum_ref[...], p), cover_ref[...])
    post = PAST_LEN + lax.broadcasted_iota(jnp.int32, (score.shape[0], 1), 0)
    lane = lax.broadcasted_iota(jnp.int32, score.shape, 1)
    cur = post // SLC_BLOCK
    forced = (lane == 0) | (lane == cur) | (lane == cur - 1)
    score = jnp.where(forced, FORCE_SCORE, jnp.where(lane * SLC_BLOCK <= post, score, -1.0))
    rank = jnp.zeros(score.shape, jnp.int32)
    for i in range(n_past_blocks):
        ci = score[:, i:i + 1]
        rank = rank + ((ci > score) | ((ci == score) & (lane > i))).astype(jnp.int32)
    out_lane = lax.broadcasted_iota(jnp.int32, idx_ref.shape, 1)
    out = jnp.zeros(idx_ref.shape, jnp.int32)
    for k in range(N_SELECT - 1):
        idx_k = jnp.sum(jnp.where(rank == k, lane.astype(F32), 0.0), axis=-1, keepdims=True)
        out = jnp.where(out_lane == k, idx_k.astype(jnp.int32), out)
    idx_ref[...] = out


def _decode_select(q_pad, cmp, ls):
    batch = q_pad.shape[0]
    nq = ls * DEC_QROWS
    n_sub = cmp.shape[3]
    n_past_blocks = PAST_LEN // SLC_BLOCK
    assert PAST_LEN % SLC_BLOCK == 0 and ls <= min(SLC_BLOCK, 8) and n_past_blocks >= N_SELECT
    c0 = np.arange(n_sub)[:, None] * CMP_STRIDE
    s0 = np.arange(n_past_blocks)[None, :] * SLC_BLOCK
    cover = np.clip(np.minimum(c0 + CMP_BLOCK, s0 + SLC_BLOCK) - np.maximum(c0, s0), 0, None) / CMP_BLOCK
    cover = cover * (np.arange(n_sub)[:, None] < n_sub - 1)
    rows = np.arange(nq)[None, :]
    rsum = (rows // DEC_QROWS == np.arange(8)[:, None]) & (rows % DEC_QROWS < NSA_GROUP)
    hd = NSA_HEAD_DIM
    q_spec = pl.BlockSpec((None, None, nq, hd), lambda b, g: (b, g, 0, 0))
    return pl.pallas_call(
        functools.partial(_decode_select_kernel, n_cmp=n_sub - 1, n_past_blocks=n_past_blocks),
        grid=(batch, NSA_KV_HEADS),
        in_specs=[q_spec,
                  pl.BlockSpec((None, None, None, n_sub, hd), lambda b, g: (0, b, g, 0, 0)),
                  pl.BlockSpec((None, None, None, n_sub, hd), lambda b, g: (1, b, g, 0, 0)),
                  pl.BlockSpec((n_sub, n_past_blocks), lambda b, g: (0, 0)),
                  pl.BlockSpec((8, nq), lambda b, g: (0, 0))],
        out_specs=[q_spec, pl.BlockSpec((None, None, 8, LANE), lambda b, g: (b, g, 0, 0))],
        out_shape=[jax.ShapeDtypeStruct((batch, NSA_KV_HEADS, nq, hd), F32),
                   jax.ShapeDtypeStruct((batch, NSA_KV_HEADS, 8, LANE), jnp.int32)],
        compiler_params=pltpu.CompilerParams(dimension_semantics=("arbitrary", "arbitrary"),
                                             vmem_limit_bytes=VMEM_LIMIT_BYTES),
        name="decode_select",
    )(q_pad, cmp, cmp, jnp.asarray(cover, BF16), jnp.asarray(rsum, BF16))


def _decode_attend_kernel(pt_ref, idx_ref, q_ref, ocmp_ref, gate_ref, new_ref, kbuf_ref, vbuf_ref, cache_ref,
                          o_ref, kg_s, vg_s, sem, *, ls, page0):
    b = pl.program_id(0)
    g = pl.program_id(1)
    n_sel = N_SELECT - 1
    hd = NSA_HEAD_DIM
    scale = hd ** -0.5

    def gather_copies():
        for t in range(ls):
            for k in range(n_sel):
                j = idx_ref[((b * NSA_KV_HEADS + g) * ls + t) * n_sel + k]
                page = page0 + pt_ref[b, j // (PAGE_SIZE // SLC_BLOCK)]
                row0 = (j % (PAGE_SIZE // SLC_BLOCK)) * SLC_BLOCK
                for which, (slot, dst) in enumerate(((2, kg_s), (3, vg_s))):
                    yield pltpu.make_async_copy(
                        cache_ref.at[page, pl.ds(row0, SLC_BLOCK), slot * NSA_KV_HEADS + g, :],
                        dst.at[t, k], sem.at[which])

    for cp in gather_copies():
        cp.start()

    nq = ls * DEC_QROWS
    new = new_ref[...].astype(BF16)
    q = q_ref[...].astype(BF16)
    tok = lax.broadcasted_iota(jnp.int32, (nq, 1), 0) // DEC_QROWS
    wb = kbuf_ref.shape[0]
    i_buf = lax.broadcasted_iota(jnp.int32, (nq, wb), 1)
    i_new = lax.broadcasted_iota(jnp.int32, (nq, new.shape[1]), 1)
    m_new = (i_new <= tok) & (i_new < ls)
    m_buf = (wb - i_buf + tok) < WINDOW
    s_b = jnp.where(m_buf, _dot_nt(q, kbuf_ref[...].astype(BF16)) * scale, NEG)
    s_n = jnp.where(m_new, _dot_nt(q, new[2]) * scale, NEG)
    mx = jnp.maximum(jnp.max(s_b, axis=-1, keepdims=True), jnp.max(s_n, axis=-1, keepdims=True))
    e_b = jnp.exp(s_b - mx)
    e_n = jnp.exp(s_n - mx)
    den = jnp.sum(e_b, axis=-1, keepdims=True) + jnp.sum(e_n, axis=-1, keepdims=True)
    o_swa = (jnp.dot(e_b.astype(BF16), vbuf_ref[...].astype(BF16), preferred_element_type=F32)
             + jnp.dot(e_n.astype(BF16), new[3], preferred_element_type=F32)) / den

    for cp in gather_copies():
        cp.wait()

    gates = gate_ref[...]
    for t in range(ls):
        rows = slice(t * DEC_QROWS, (t + 1) * DEC_QROWS)
        qt = q[rows]
        s_p = _dot_nt(qt, kg_s[t].reshape(n_sel * SLC_BLOCK, hd).astype(BF16)) * scale
        s_n = jnp.where(m_new[rows], _dot_nt(qt, new[0]) * scale, NEG)
        mx = jnp.maximum(jnp.max(s_p, axis=-1, keepdims=True), jnp.max(s_n, axis=-1, keepdims=True))
        e_p = jnp.exp(s_p - mx)
        e_n = jnp.exp(s_n - mx)
        den = jnp.sum(e_p, axis=-1, keepdims=True) + jnp.sum(e_n, axis=-1, keepdims=True)
        o_slc = (jnp.dot(e_p.astype(BF16), vg_s[t].reshape(n_sel * SLC_BLOCK, hd).astype(BF16),
                         preferred_element_type=F32)
                 + jnp.dot(e_n.astype(BF16), new[1], preferred_element_type=F32)) / den
        gt = gates[rows]
        o_ref[rows, :] = gt[:, 0:1] * ocmp_ref[rows, :] + gt[:, 1:2] * o_slc + gt[:, 2:3] * o_swa[rows]


def _decode_attend(q_pad, o_cmp, gates, new_rows, swa_buf, cache, page0, page_table, idx, ls):
    batch = q_pad.shape[0]
    nq = ls * DEC_QROWS
    hd = NSA_HEAD_DIM
    wb = swa_buf.shape[1]
    assert wb == WINDOW and PAGE_SIZE % SLC_BLOCK == 0
    n_sel = N_SELECT - 1
    q_spec = pl.BlockSpec((None, None, nq, hd), lambda b, g, pt, ix: (b, g, 0, 0))

    def buf_spec(slot):
        return pl.BlockSpec((None, wb, hd), lambda b, g, pt, ix: (b, 0, slot * NSA_KV_HEADS + g))

    return pl.pallas_call(
        functools.partial(_decode_attend_kernel, ls=ls, page0=page0),
        grid_spec=pltpu.PrefetchScalarGridSpec(
            num_scalar_prefetch=2,
            grid=(batch, NSA_KV_HEADS),
            in_specs=[q_spec, q_spec,
                      pl.BlockSpec((None, None, nq, LANE), lambda b, g, pt, ix: (b, g, 0, 0)),
                      pl.BlockSpec((None, None, 4, 8, hd), lambda b, g, pt, ix: (b, g, 0, 0, 0)),
                      buf_spec(0), buf_spec(1),
                      pl.BlockSpec(memory_space=pl.ANY)],
            out_specs=q_spec,
            scratch_shapes=[pltpu.VMEM((ls, n_sel, SLC_BLOCK, hd), F32), pltpu.VMEM((ls, n_sel, SLC_BLOCK, hd), F32),
                            pltpu.SemaphoreType.DMA((2,))]),
        out_shape=jax.ShapeDtypeStruct((batch, NSA_KV_HEADS, nq, hd), F32),
        compiler_params=pltpu.CompilerParams(dimension_semantics=("arbitrary", "arbitrary"),
                                             vmem_limit_bytes=VMEM_LIMIT_BYTES),
        name="decode_attend",
    )(page_table, idx, q_pad, o_cmp, gates, new_rows,
      swa_buf.reshape(batch, wb, 2 * NSA_KV_HEADS * hd), swa_buf.reshape(batch, wb, 2 * NSA_KV_HEADS * hd),
      cache)


HG_SLAB = 256
HG_UNROLL = 8


class _ColWindow:
    def __init__(self, lo_ref, hi_ref, off):
        self.lo, self.hi, self.off = lo_ref, hi_ref, off

    def __getitem__(self, idx):
        a = self.lo[idx]
        if self.off == 0:
            return a
        shift = LANE - self.off
        lane = lax.broadcasted_iota(jnp.int32, a.shape, 1)
        return jnp.where(lane < shift, pltpu.roll(a, shift, 1), pltpu.roll(self.hi[idx], shift, 1))


def _hgrn_kernel(*refs, seq, off):
    ins = refs[:8]
    lb_ref, gn_ref, s0_ref, tri_ref, o_ref, s_ref, qs_s, el_s, o_s, u_s, st_s = refs[8:]
    hq_ref, hf_ref, hi_ref, hg_ref = (_ColWindow(ins[2 * i], ins[2 * i + 1], off) for i in range(4))
    c = HG_CHUNK
    per_slab = HG_SLAB // c
    lb = lb_ref[...]

    def prep(i, carry):
        rows = pl.ds(pl.multiple_of(i * HG_SLAB, HG_SLAB), HG_SLAB)
        hq = hq_ref[rows, :]
        q = hq * jax.nn.sigmoid(hq)
        f = lb + (1.0 - lb) * jax.nn.sigmoid(hf_ref[rows, :])
        k = 1.0 - f
        v = hi_ref[rows, :].astype(BF16)
        bc = _sel_dot(tri_ref[0], jnp.log(f))
        bc3 = bc.reshape(per_slab, c, LANE)
        mid = (c - 1) // 2
        bm = jnp.broadcast_to(bc3[:, mid:mid + 1, :], bc3.shape).reshape(HG_SLAB, LANE)
        bl = jnp.broadcast_to(bc3[:, c - 1:c, :], bc3.shape).reshape(HG_SLAB, LANE)
        a = _dot_nt((q * jnp.exp(bc - bm)).astype(BF16), (k * jnp.exp(bm - bc)).astype(BF16))
        a = jnp.where(tri_ref[0] > 0, a, 0.0)
        o_s[rows, :] = jnp.dot(a.astype(BF16), v, preferred_element_type=F32)
        qs_s[rows, :] = (q * jnp.exp(bc)).astype(qs_s.dtype)
        kd = (k * jnp.exp(bl - bc)).astype(BF16)
        el = jnp.exp(bl)
        for j in range(per_slab):
            ch = slice(j * c, (j + 1) * c)
            u_s[i * per_slab + j] = lax.dot_general(v[ch], kd[ch], (((0,), (0,)), ((), ())),
                                                    preferred_element_type=F32)
            el_s[pl.ds(i * per_slab + j, 1), :] = el[j * c:j * c + 1]
        return carry

    lax.fori_loop(0, seq // HG_SLAB, prep, 0)

    def scan(ci, st):
        st_s[ci] = st.astype(st_s.dtype)
        return st * el_s[pl.ds(ci, 1), :] + u_s[ci]

    st = lax.fori_loop(0, seq // c, scan, s0_ref[...].T, unroll=HG_UNROLL)
    s_ref[...] = st.T

    gn = gn_ref[...]

    def finish(i, carry):
        rows = pl.ds(pl.multiple_of(i * HG_SLAB, HG_SLAB), HG_SLAB)
        qs = qs_s[rows, :]
        inter = [_dot_nt(qs[j * c:(j + 1) * c], st_s[i * per_slab + j]) for j in range(per_slab)]
        o = o_s[rows, :] + jnp.concatenate(inter, axis=0)
        hg = hg_ref[rows, :]
        o = o * lax.rsqrt(jnp.mean(o * o, axis=-1, keepdims=True) + EPS)
        o_ref[rows, :] = (o * gn * (hg * jax.nn.sigmoid(hg))).astype(o_ref.dtype)
        return carry

    lax.fori_loop(0, seq // HG_SLAB, finish, 0)


def _hgrn_tri():
    i = np.arange(HG_SLAB)[:, None]
    j = np.arange(HG_SLAB)[None, :]
    cum = ((i // HG_CHUNK) == (j // HG_CHUNK)) & (j <= i)
    return jnp.asarray(cum[None], BF16)


def _hgrn_prompt(proj, col0, lb, g_norm, s0, batch, seq):
    assert seq % HG_SLAB == 0 and HG_DK == LANE and HG_DV == LANE
    c0 = col0 // LANE

    def col_specs(group):
        return [pl.BlockSpec((seq, LANE), lambda b, h, part=part: (b, c0 + group * HG_HEADS + h + part))
                for part in ((0, 1) if col0 % LANE else (0, 0))]

    vec_spec = pl.BlockSpec((1, LANE), lambda b, h: (0, h))
    st_spec = pl.BlockSpec((None, None, HG_DK, HG_DV), lambda b, h: (b, h, 0, 0))
    return pl.pallas_call(
        functools.partial(_hgrn_kernel, seq=seq, off=col0 % LANE),
        grid=(batch, HG_HEADS),
        in_specs=col_specs(0) + col_specs(1) + col_specs(2) + col_specs(3) + [
            vec_spec, vec_spec, st_spec, pl.BlockSpec((1, HG_SLAB, HG_SLAB), lambda b, h: (0, 0, 0))],
        out_specs=[pl.BlockSpec((seq, LANE), lambda b, h: (b, h)), st_spec],
        out_shape=[jax.ShapeDtypeStruct((batch * seq, HG_V_W), BF16),
                   jax.ShapeDtypeStruct((batch, HG_HEADS, HG_DK, HG_DV), F32)],
        scratch_shapes=[pltpu.VMEM((seq, LANE), BF16), pltpu.VMEM((seq // HG_CHUNK, LANE), F32),
                        pltpu.VMEM((seq, LANE), F32), pltpu.VMEM((seq // HG_CHUNK, HG_DV, HG_DK), F32),
                        pltpu.VMEM((seq // HG_CHUNK, HG_DV, HG_DK), BF16)],
        compiler_params=pltpu.CompilerParams(
            dimension_semantics=("arbitrary", "arbitrary"),
            vmem_limit_bytes=VMEM_LIMIT_BYTES),
        name="hgrn_prompt",
    )(*([proj] * 8), lb, g_norm, s0, _hgrn_tri())


SSD_TL = 512
SSD_GW = SSM_D_INNER // SSM_GROUPS
SSD_HPG = SSM_HEADS // SSM_GROUPS


def _split3(x):
    hi = x.astype(BF16)
    r1 = x - hi.astype(F32)
    mid = r1.astype(BF16)
    lo = (r1 - mid.astype(F32)).astype(BF16)
    return hi, mid, lo


def _sel_dot(sel, x):
    hi, mid, lo = _split3(x)
    d = functools.partial(jnp.dot, preferred_element_type=F32)
    return d(sel, hi) + d(sel, mid) + d(sel, lo)


def _dot_sel(x, sel):
    hi, mid, lo = _split3(x)
    d = functools.partial(jnp.dot, preferred_element_type=F32)
    return d(hi, sel) + d(mid, sel) + d(lo, sel)


def _causal_conv_silu(x, prev8, w, bias):
    row8 = lax.broadcasted_iota(jnp.int32, prev8.shape, 0)
    acc = bias
    for k in range(SSM_CONV - 1, 0, -1):
        r = pltpu.roll(x, k, 0)
        top = jnp.where(row8 < k, pltpu.roll(prev8, k, 0), r[:8])
        acc = acc + jnp.concatenate([top, r[8:]], axis=0) * w[SSM_CONV - 1 - k:SSM_CONV - k]
    acc = acc + x * w[SSM_CONV - 1:SSM_CONV]
    return acc * jax.nn.sigmoid(acc)


def _ssd_kernel(z_ref, x_ref, b_ref, c_ref, dt_ref, cx0_ref, cb0_ref, cc0_ref, s0_ref,
                wx_ref, wb_ref, wc_ref, bx_ref, bb_ref, bc_ref, dtb_ref, alog_ref, d_ref, gn_ref,
                tri_ref, hsel_ref, o_ref, s_ref,
                st_s, tx_s, tb_s, tc_s, xdt_s, xdd_s, ce_s, le_s, bm_s, cm_s, y_s):
    lt = pl.program_id(2)
    c = SSM_CHUNK
    tl = SSD_TL
    hp = SSM_HEAD_DIM

    @pl.when(lt == 0)
    def _():
        st_s[...] = s0_ref[...].T
        for tail, c0 in ((tx_s, cx0_ref), (tb_s, cb0_ref), (tc_s, cc0_ref)):
            tail[...] = jnp.zeros(tail.shape, F32)
            tail[8 - (SSM_CONV - 1):8, :] = c0[...]

    x_raw = x_ref[...]
    b_raw = b_ref[...]
    c_raw = c_ref[...]
    xs = _causal_conv_silu(x_raw, tx_s[...], wx_ref[...], bx_ref[...])
    bm_s[...] = _causal_conv_silu(b_raw, tb_s[...], wb_ref[...], bb_ref[...]).astype(bm_s.dtype)
    cm_s[...] = _causal_conv_silu(c_raw, tc_s[...], wc_ref[...], bc_ref[...]).astype(cm_s.dtype)
    tx_s[...] = x_raw[tl - 8:]
    tb_s[...] = b_raw[tl - 8:]
    tc_s[...] = c_raw[tl - 8:]

    dt = jax.nn.softplus(dt_ref[...] + dtb_ref[...])
    cum = _sel_dot(tri_ref[0], dt * -jnp.exp(alog_ref[...]))
    hsel = hsel_ref[...]
    dt_e = _dot_sel(dt, hsel)
    cum_e = _dot_sel(cum, hsel)
    last_e = _dot_sel(_sel_dot(tri_ref[1], cum), hsel)
    xdt = xs * dt_e
    xdt_s[...] = xdt.astype(xdt_s.dtype)
    xdd_s[...] = (xdt * jnp.exp(last_e - cum_e)).astype(xdd_s.dtype)
    ce_s[...] = cum_e
    le_s[...] = jnp.exp(last_e)

    trow = lax.broadcasted_iota(jnp.int32, (c, SSD_GW), 0)
    lane_g = lax.broadcasted_iota(jnp.int32, (c, SSD_GW), 1)
    diag = (lane_g % hp) == trow
    t2 = lax.broadcasted_iota(jnp.int32, (c, LANE), 0)
    l2 = lax.broadcasted_iota(jnp.int32, (c, LANE), 1)
    tril2 = (l2 % hp) <= t2
    r3 = lax.broadcasted_iota(jnp.int32, (2 * c, LANE), 0)
    l3 = lax.broadcasted_iota(jnp.int32, (2 * c, LANE), 1)
    blockdiag = (r3 // c) == (l3 // hp)

    for ci in range(tl // c):
        rows = pl.ds(ci * c, c)
        cm = cm_s[rows, :]
        bm = bm_s[rows, :]
        cum_c = ce_s[rows, :]
        cum_row = jnp.sum(jnp.where(diag, cum_c, 0.0), axis=0, keepdims=True)
        cb2 = _dot_nt(cm, jnp.concatenate([bm, bm], axis=0))
        xdt_c = xdt_s[rows, :]
        st = st_s[...]
        y_inter = jnp.dot(cm, st.astype(cm.dtype), preferred_element_type=F32) * jnp.exp(cum_c)
        for j in range(SSD_GW // LANE):
            cols = slice(j * LANE, (j + 1) * LANE)
            seg = cum_c[:, cols] - cum_row[:, cols]
            w = (cb2 * jnp.where(tril2, jnp.exp(seg), 0.0)).astype(xdt_c.dtype)
            xj = xdt_c[:, cols]
            xbd = jnp.where(blockdiag, jnp.concatenate([xj, xj], axis=0), jnp.zeros_like(xj[:1, :1]))
            y_s[rows, cols] = y_inter[:, cols] + jnp.dot(w, xbd, preferred_element_type=F32)
        upd = lax.dot_general(bm, xdd_s[rows, :], (((0,), (0,)), ((), ())), preferred_element_type=F32)
        st_s[...] = st * le_s[pl.ds(ci * c, 1), :] + upd

    z = z_ref[...]
    y = (y_s[...] + d_ref[...] * xs) * (z * jax.nn.sigmoid(z))
    y = y * lax.rsqrt(jnp.mean(y * y, axis=-1, keepdims=True) + EPS)
    o_ref[...] = (y * gn_ref[...]).astype(o_ref.dtype)

    @pl.when(lt == pl.num_programs(2) - 1)
    def _():
        s_ref[...] = st_s[...].T


def _ssd_consts():
    i = np.arange(SSD_TL)[:, None]
    j = np.arange(SSD_TL)[None, :]
    cum = ((i // SSM_CHUNK) == (j // SSM_CHUNK)) & (j <= i)
    last = j == (i // SSM_CHUNK) * SSM_CHUNK + SSM_CHUNK - 1
    tri = jnp.asarray(np.stack([cum, last]), BF16)
    h = np.arange(SSM_HEADS)[None, :, None]
    g = np.arange(SSM_GROUPS)[:, None, None]
    lane = np.arange(SSD_GW)[None, None, :]
    hsel = jnp.asarray(h == g * SSD_HPG + lane // SSM_HEAD_DIM, BF16)
    return tri, hsel


def _ssd_prompt(proj, conv0, s0, conv_w, conv_b, dt_bias, a_log, d_skip, norm_g, batch, seq):
    assert seq % SSD_TL == 0 and SSM_STATE == LANE and SSM_HEADS == LANE and SSM_CHUNK == SSM_HEAD_DIM
    nl = seq // SSD_TL
    gpl = SSD_GW // LANE
    xb = SSM_D_INNER // SSD_GW
    bb = 2 * SSM_D_INNER // LANE
    cb = bb + SSM_GROUPS
    db = cb + SSM_GROUPS
    tri, hsel = _ssd_consts()
    d_e = jnp.repeat(d_skip.astype(F32), SSM_HEAD_DIM)[None]
    row = lambda a: a.astype(F32)[None]

    def rows_spec(width, col_fn):
        return pl.BlockSpec((SSD_TL, width), lambda b, g, t: (b * nl + t, col_fn(g)))

    def conv0_spec(width, col_fn):
        return pl.BlockSpec((None, SSM_CONV - 1, width), lambda b, g, t: (b, 0, col_fn(g)))

    def vec_spec(rows_, width, col_fn):
        return pl.BlockSpec((rows_, width), lambda b, g, t: (0, col_fn(g)))

    xcol = lambda g: g
    bcol = lambda g: SSM_D_INNER // LANE + g
    ccol = lambda g: SSM_D_INNER // LANE + SSM_GROUPS + g
    zero = lambda g: 0
    st_spec = pl.BlockSpec((None, SSD_GW, SSM_STATE), lambda b, g, t: (b, g, 0))
    return pl.pallas_call(
        _ssd_kernel,
        grid=(batch, SSM_GROUPS, nl),
        in_specs=[rows_spec(SSD_GW, xcol), rows_spec(SSD_GW, lambda g: xb + g),
                  rows_spec(LANE, lambda g: bb + g), rows_spec(LANE, lambda g: cb + g),
                  rows_spec(LANE, lambda g: db),
                  conv0_spec(SSD_GW, xcol), conv0_spec(LANE, bcol), conv0_spec(LANE, ccol), st_spec,
                  vec_spec(SSM_CONV, SSD_GW, xcol), vec_spec(SSM_CONV, LANE, bcol), vec_spec(SSM_CONV, LANE, ccol),
                  vec_spec(1, SSD_GW, xcol), vec_spec(1, LANE, bcol), vec_spec(1, LANE, ccol),
                  vec_spec(1, LANE, zero), vec_spec(1, LANE, zero),
                  vec_spec(1, SSD_GW, xcol), vec_spec(1, SSD_GW, xcol),
                  pl.BlockSpec((2, SSD_TL, SSD_TL), lambda b, g, t: (0, 0, 0)),
                  pl.BlockSpec((None, SSM_HEADS, SSD_GW), lambda b, g, t: (g, 0, 0))],
        out_specs=[rows_spec(SSD_GW, xcol), st_spec],
        out_shape=[jax.ShapeDtypeStruct((batch * seq, SSM_D_INNER), BF16),
                   jax.ShapeDtypeStruct((batch, SSM_HEADS * SSM_HEAD_DIM, SSM_STATE), F32)],
        scratch_shapes=[pltpu.VMEM((SSM_STATE, SSD_GW), F32),
                        pltpu.VMEM((8, SSD_GW), F32), pltpu.VMEM((8, LANE), F32), pltpu.VMEM((8, LANE), F32),
                        pltpu.VMEM((SSD_TL, SSD_GW), BF16), pltpu.VMEM((SSD_TL, SSD_GW), BF16),
                        pltpu.VMEM((SSD_TL, SSD_GW), F32), pltpu.VMEM((SSD_TL, SSD_GW), F32),
                        pltpu.VMEM((SSD_TL, LANE), BF16), pltpu.VMEM((SSD_TL, LANE), BF16),
                        pltpu.VMEM((SSD_TL, SSD_GW), F32)],
        compiler_params=pltpu.CompilerParams(
            dimension_semantics=("arbitrary", "arbitrary", "arbitrary"),
            vmem_limit_bytes=VMEM_LIMIT_BYTES),
        name="ssd_prompt",
    )(proj, proj, proj, proj, proj, conv0, conv0, conv0, s0,
      conv_w, conv_w, conv_w, row(conv_b), row(conv_b), row(conv_b), row(dt_bias), row(a_log),
      d_e, row(norm_g), tri, hsel)


def split_cols(a, sizes):
    out, o = [], 0
    for s in sizes:
        out.append(a[..., o:o + s])
        o += s
    return out


def to_chunks(a, c):
    b, l = a.shape[:2]
    pad = (-l) % c
    a = jnp.pad(a.astype(F32), [(0, 0), (0, pad)] + [(0, 0)] * (a.ndim - 2))
    return jnp.moveaxis(a.reshape((b, (l + pad) // c, c) + a.shape[2:]), 1, 0)


def from_chunks(a, l):
    a = jnp.moveaxis(a, 0, 1)
    return a.reshape((a.shape[0], a.shape[1] * a.shape[2]) + a.shape[3:])[:, :l]


def gla_chunked(q, k, v, logf, s0):
    b, l = q.shape[:2]
    c = min(HG_CHUNK, l)
    mid = (c - 1) // 2
    tril = jnp.tril(jnp.ones((c, c), bool))

    def step(s, inp):
        qc, kc, vc, gc = inp
        bc = jnp.cumsum(gc, axis=1)
        bm = bc[:, mid:mid + 1]
        a = jnp.einsum('bthd,bshd->bhts', qc * jnp.exp(bc - bm), kc * jnp.exp(bm - bc))
        a = jnp.where(tril, a, 0.0)
        o = jnp.einsum('bhts,bshv->bthv', a, vc) + jnp.einsum('bthd,bhdv->bthv', qc * jnp.exp(bc), s)
        bl = bc[:, -1]
        s = jnp.exp(bl)[..., None] * s + jnp.einsum('bshd,bshv->bhdv', kc * jnp.exp(bl[:, None] - bc), vc)
        return s, o

    s, o = lax.scan(step, s0.astype(F32), (to_chunks(q, c), to_chunks(k, c), to_chunks(v, c), to_chunks(logf, c)))
    return from_chunks(o, l), s


def hgrn2(hq, hf, hi, hg, lb, s0, g_norm):
    b, l, _ = hq.shape
    q = jax.nn.silu(hq.astype(F32)).reshape(b, l, HG_HEADS, HG_DK)
    f = (lb + (1.0 - lb) * jax.nn.sigmoid(hf.astype(F32))).reshape(b, l, HG_HEADS, HG_DK)
    v = hi.astype(F32).reshape(b, l, HG_HEADS, HG_DV)
    o, s = gla_chunked(q, 1.0 - f, v, jnp.log(f), s0)
    o = o * lax.rsqrt(jnp.mean(o * o, axis=-1, keepdims=True) + EPS)
    o = o.reshape(b, l, HG_V_W) * g_norm.astype(F32) * jax.nn.silu(hg.astype(F32))
    return o, s


def nsa_prompt_branch(proj, b, l, w1, w2, pe):
    cmp = _compress_prompt(proj, w1, w2, pe, b, l)
    g = proj[:, AB_GATE_COL:AB_GATE_COL + NSA_GATE_W]
    gates = jax.nn.sigmoid(g).reshape(b * l, NSA_KV_HEADS, 3 * NSA_GROUP)
    gates = jnp.pad(gates, ((0, 0), (0, 0), (0, LANE - 3 * NSA_GROUP))).reshape(b * l, NSA_KV_HEADS * LANE)
    return _nsa_prompt(proj, cmp[0], cmp[1], gates, b, l)


AB_SIZES = (NSA_Q, NSA_KV_W, NSA_GATE_W, HG_QF_W, HG_QF_W, HG_V_W, HG_V_W)
AB_IN = sum(AB_SIZES)
AB_GATE_COL = NSA_Q + NSA_KV_W
AB_HG_COL = AB_GATE_COL + NSA_GATE_W


def ab_mixer_prompt(x, h, b, l, hg_state, lb, w_in, w1, w2, pe, hg_g, w_out):
    proj = _mm(h, w_in)
    o_nsa = nsa_prompt_branch(proj, b, l, w1, w2, pe)
    o_hg, s_new = _hgrn_prompt(proj, AB_HG_COL, lb[None], hg_g[None], hg_state, b, l)
    x = _mm([o_nsa, o_hg], w_out, res=x)
    slot_w = NSA_KV_HEADS * NSA_HEAD_DIM
    rows = proj[:, NSA_Q:NSA_Q + 4 * slot_w].reshape(b, l, 4, NSA_KV_HEADS, NSA_HEAD_DIM)
    wb = min(WINDOW, l)
    buf = proj.reshape(b, l, AB_IN)[:, l - wb:, NSA_Q + 4 * slot_w:NSA_Q + 6 * slot_w]
    return x, rows, buf.reshape(b, wb, 2, NSA_KV_HEADS, NSA_HEAD_DIM), s_new


def nsa_decode_branch(q, kv, g, cache, page0, page_table, swa_buf, w1, w2, pe):
    b, ls = q.shape[:2]
    assert page_table.shape[1] * PAGE_SIZE == PAST_LEN
    hd = NSA_HEAD_DIM
    row_pad = DEC_QROWS - NSA_GROUP
    q_pad = q.reshape(b, ls, NSA_KV_HEADS, NSA_GROUP, hd).transpose(0, 2, 1, 3, 4)
    q_pad = jnp.pad(q_pad, ((0, 0), (0, 0), (0, 0), (0, row_pad), (0, 0))).reshape(b, NSA_KV_HEADS, ls * DEC_QROWS, hd)
    gates = jax.nn.sigmoid(g).reshape(b, ls, NSA_KV_HEADS, NSA_GROUP, 3).transpose(0, 2, 1, 3, 4)
    gates = jnp.pad(gates, ((0, 0), (0, 0), (0, 0), (0, row_pad), (0, LANE - 3)))
    gates = gates.reshape(b, NSA_KV_HEADS, ls * DEC_QROWS, LANE)
    new_rows = jnp.pad(kv[:, :, 2:6].transpose(0, 3, 2, 1, 4), ((0, 0), (0, 0), (0, 0), (0, 8 - ls), (0, 0)))
    cmp = _compress_paged(cache, page0, page_table, w1, w2, pe)
    o_cmp, idx = _decode_select(q_pad, cmp, ls)
    o = _decode_attend(q_pad, o_cmp, gates, new_rows, swa_buf, cache, page0, page_table,
                       idx[:, :, :ls, :N_SELECT - 1].reshape(-1), ls)
    o = o.reshape(b, NSA_KV_HEADS, ls, DEC_QROWS, hd)[:, :, :, :NSA_GROUP]
    return o.transpose(0, 2, 1, 3, 4).reshape(b * ls, NSA_Q)


def ab_mixer_sample(x, h, b, l, cache, page0, page_table, swa_buf, hg_state, lb, w_in, w1, w2, pe, hg_g, w_out):
    q, kv, g, hq, hf, hi, hgate = split_cols(_mm(h, w_in).reshape(b, l, AB_IN), AB_SIZES)
    kv = kv.reshape(b, l, 6, NSA_KV_HEADS, NSA_HEAD_DIM)
    o_nsa = nsa_decode_branch(q, kv, g, cache, page0, page_table, swa_buf, w1, w2, pe)
    new_buf = jnp.concatenate([swa_buf.astype(kv.dtype), kv[:, :, 4:]], axis=1)[:, l:]
    o_hg, s_new = hgrn2(hq, hf, hi, hgate, lb, hg_state, hg_g)
    mix = jnp.concatenate([o_nsa, o_hg.reshape(b * l, HG_V_W)], axis=-1).astype(BF16)
    return _mm(mix, w_out, res=x), kv[:, :, :4], new_buf, s_new


def ssd_chunked(x, dt, a, bm, cm, s0):
    b, l, nh, p = x.shape
    g, n = bm.shape[2], bm.shape[3]
    r = nh // g
    c = min(SSM_CHUNK, l)
    tril = jnp.tril(jnp.ones((c, c), bool))

    def step(s, inp):
        xc, dtc, bc, cc = inp
        cum = jnp.cumsum(dtc * a, axis=1)
        seg = cum[:, :, None, :] - cum[:, None, :, :]
        lm = jnp.exp(jnp.where(tril[None, :, :, None], seg, -jnp.inf)).reshape(b, c, c, g, r)
        xdt = (xc * dtc[..., None]).reshape(b, c, g, r, p)
        cb = jnp.einsum('btgn,bsgn->btsg', cc, bc)
        sg = s.reshape(b, g, r, p, n)
        y = jnp.einsum('btsg,btsgr,bsgrp->btgrp', cb, lm, xdt)
        y = y + jnp.einsum('btgn,bgrpn->btgrp', cc, sg) * jnp.exp(cum).reshape(b, c, g, r)[..., None]
        dec = jnp.exp(cum[:, -1:] - cum).reshape(b, c, g, r)
        sg = jnp.exp(cum[:, -1]).reshape(b, g, r)[..., None, None] * sg + jnp.einsum('bsgn,bsgrp->bgrpn', bc, xdt * dec[..., None])
        return sg.reshape(b, nh, p, n), y.reshape(b, c, nh, p)

    s, y = lax.scan(step, s0.astype(F32), (to_chunks(x, c), to_chunks(dt, c), to_chunks(bm, c), to_chunks(cm, c)))
    return from_chunks(y, l), s


def mamba_prompt(x, h, b, l, w_in, conv_w, conv_b, dt_bias, a_log, d_skip, norm_g, w_out):
    proj = _mm(h, w_in)
    y, s = _ssd_prompt(proj, jnp.zeros((b, SSM_CONV - 1, SSM_CONV_DIM), F32),
                       jnp.zeros((b, SSM_HEADS * SSM_HEAD_DIM, SSM_STATE), F32),
                       conv_w, conv_b, dt_bias, a_log, d_skip, norm_g, b, l)
    assert l >= SSM_CONV - 1
    new_conv = proj.reshape(b, l, -1)[:, l - (SSM_CONV - 1):, SSM_D_INNER:SSM_D_INNER + SSM_CONV_DIM]
    return _mm(y, w_out, res=x), new_conv, s.reshape(b, SSM_HEADS, SSM_HEAD_DIM, SSM_STATE)


def mamba_sample(x, h, b, l, conv_state, ssm_state, w_in, conv_w, conv_b, dt_bias, a_log, d_skip, norm_g, w_out):
    z, xbc, dt = split_cols(_mm(h, w_in).reshape(b, l, -1), (SSM_D_INNER, SSM_CONV_DIM, SSM_HEADS))
    xpad = jnp.concatenate([conv_state.astype(xbc.dtype), xbc], axis=1)
    acc = conv_b.astype(F32)
    for j in range(SSM_CONV):
        acc = acc + xpad[:, j:j + l].astype(F32) * conv_w[j].astype(F32)
    xbc = jax.nn.silu(acc)
    new_conv = xpad[:, xpad.shape[1] - (SSM_CONV - 1):]
    xs, bm, cm = split_cols(xbc, (SSM_D_INNER, SSM_GROUPS * SSM_STATE, SSM_GROUPS * SSM_STATE))
    xs = xs.reshape(b, l, SSM_HEADS, SSM_HEAD_DIM)
    bm = bm.reshape(b, l, SSM_GROUPS, SSM_STATE)
    cm = cm.reshape(b, l, SSM_GROUPS, SSM_STATE)
    dt = jax.nn.softplus(dt.astype(F32) + dt_bias.astype(F32))
    a = -jnp.exp(a_log.astype(F32))
    y, s = ssd_chunked(xs, dt, a, bm, cm, ssm_state)
    y = y + d_skip.astype(F32)[:, None] * xs
    y = (y.reshape(b, l, SSM_D_INNER) * jax.nn.silu(z.astype(F32))).reshape(b, l, SSM_GROUPS, SSM_D_INNER // SSM_GROUPS)
    y = (y * lax.rsqrt(jnp.mean(y * y, axis=-1, keepdims=True) + EPS)).reshape(b, l, SSM_D_INNER) * norm_g.astype(F32)
    return _mm(y.reshape(b * l, SSM_D_INNER).astype(BF16), w_out, res=x), new_conv, s


def ffn_ple(x, p, g_ffn, w_ffn_in, w_ffn_out, g_ple, w_ple_gate, w_ple_up):
    x = _mm(_mm_swiglu(_rms(x, g_ffn, BF16), w_ffn_in), w_ffn_out, res=x)
    return _mm_ple(_rms(x, g_ple, BF16), w_ple_gate, p, w_ple_up, x)


def kernel(x_prompt, x_sample, cache_nsa_kv, cache_swa_kv, state_hgrn, state_ssm, cache_conv, page_table, p_prompt, p_sample, norm_mix, norm_ffn, w_ab_in, w_cmp1, w_cmp2, cmp_pe, hg_lb_logits, hg_norm, w_ab_out, w_ssm_in, ssm_conv_w, ssm_conv_b, ssm_dt_bias, ssm_a_log, ssm_d, ssm_norm, w_ssm_out, w_ffn_in, w_ffn_out, w_ple_up, w_ple_gate, norm_ple, norm_final):
    depth = norm_mix.shape[0]
    bp, lp, d = x_prompt.shape
    bs, ls, _ = x_sample.shape
    n_pool = cache_nsa_kv.shape[1]
    cache_rows = cache_nsa_kv.reshape(cache_nsa_kv.shape[0] * n_pool, PAGE_SIZE, 4 * NSA_KV_HEADS, NSA_HEAD_DIM)
    lb_all = jnp.cumsum(jax.nn.softmax(hg_lb_logits.astype(F32), axis=0), axis=0)
    w_ffn_out_b = w_ffn_out.astype(BF16)
    w_ssm_out_b = w_ssm_out.astype(BF16)
    w_ab_in_b = w_ab_in.astype(BF16)
    xp = x_prompt.reshape(bp * lp, d)
    xs = x_sample.reshape(bs * ls, d)
    nsa_p, nsa_s, swa_p, swa_s, hg_p, hg_s, ssm_p, ssm_s, cv_p, cv_s = [], [], [], [], [], [], [], [], [], []
    for i in range(depth):
        hp = _rms(xp, norm_mix[i], BF16)
        hs = _rms(xs, norm_mix[i], BF16)
        if i % 2 == 0:
            a = i // 2
            wa = (lb_all[a], Stacked(w_ab_in_b, a), w_cmp1[a], w_cmp2[a], cmp_pe[a], hg_norm[a],
                  Stacked(w_ab_out, a))
            xp, r_p, b_p, s_p = ab_mixer_prompt(xp, hp, bp, lp, jnp.zeros((bp, HG_HEADS, HG_DK, HG_DV), F32), *wa)
            xs, r_s, b_s, s_s = ab_mixer_sample(xs, hs, bs, ls, cache_rows, a * n_pool, page_table,
                                                cache_swa_kv[a], state_hgrn[a], *wa)
            nsa_p.append(r_p)
            nsa_s.append(r_s)
            swa_p.append(b_p)
            swa_s.append(b_s)
            hg_p.append(s_p)
            hg_s.append(s_s)
        else:
            c = i // 2
            wc = (Stacked(w_ssm_in, c), ssm_conv_w[c], ssm_conv_b[c], ssm_dt_bias[c], ssm_a_log[c], ssm_d[c],
                  ssm_norm[c], Stacked(w_ssm_out_b, c))
            xp, c_p, t_p = mamba_prompt(xp, hp, bp, lp, *wc)
            xs, c_s, t_s = mamba_sample(xs, hs, bs, ls, cache_conv[c], state_ssm[c], *wc)
            cv_p.append(c_p)
            cv_s.append(c_s)
            ssm_p.append(t_p)
            ssm_s.append(t_s)
        wf = (norm_ffn[i], Stacked(w_ffn_in, i), Stacked(w_ffn_out_b, i), norm_ple[i],
              Stacked(w_ple_gate, i), Stacked(w_ple_up, i))
        xp = ffn_ple(xp, p_prompt[i].reshape(bp * lp, PLE_DIM).astype(BF16), *wf)
        xs = ffn_ple(xs, p_sample[i].reshape(bs * ls, PLE_DIM).astype(BF16), *wf)
    y_prompt = _rms(xp, norm_final, F32).reshape(bp, lp, d)
    y_sample = _rms(xs, norm_final, F32).reshape(bs, ls, d)
    return (y_prompt, y_sample, jnp.stack(nsa_p), jnp.stack(nsa_s), jnp.stack(swa_p), jnp.stack(swa_s),
            jnp.stack(hg_p), jnp.stack(hg_s), jnp.stack(ssm_p), jnp.stack(ssm_s), jnp.stack(cv_p), jnp.stack(cv_s))
```
